```python
import math
import jax, jax.numpy as jnp
from jax import lax
import numpy as np

D_MODEL = 1024
BATCH = 1
SEQ = 16384
DEPTH = 1

EPS = 1e-6
HY_WIDTH = 512
SHORT_CONV = 3
FILTER_BANDS = 16
FILTER_EMB = 1 + 2 * FILTER_BANDS
FILTER_HIDDEN = 64
FILTER_DECAY_TARGET = 1e-2
FAST_DECAY_PCT = 0.3
SLOW_DECAY_PCT = 1.5
FILTER_SHIFT = 0.05
MIN_DECAY = math.log(FILTER_DECAY_TARGET) / SLOW_DECAY_PCT
MAX_DECAY = math.log(FILTER_DECAY_TARGET) / FAST_DECAY_PCT
N_HEADS = 8
QK_NOPE = 64
QK_ROPE = 32
V_HEAD = 64
Q_LORA = 384
KV_LORA = 256
ROPE_THETA = 10000.0
Q_BLOCK = 128
MLA_WIDTH = N_HEADS * V_HEAD
N_BRANCH = 2
OFF_Q = 3 * HY_WIDTH
OFF_KV = OFF_Q + Q_LORA
OFF_KR = OFF_KV + KV_LORA
OFF_GATE = OFF_KR + QK_ROPE
IN_COLS = OFF_GATE + N_BRANCH * D_MODEL
N_GROUPS = 4
EXPERTS_PER_GROUP = 8
N_EXPERTS = N_GROUPS * EXPERTS_PER_GROUP
TOP_K = 2
D_EXPERT = 256

kernel_name = "hybrid_hyena_mla_hiermoe_adaln_block"


def _rmsnorm(x, g):
    xf = x.astype(jnp.float32)
    y = xf * lax.rsqrt(jnp.mean(xf * xf, axis=-1, keepdims=True) + EPS)
    return (y * g.astype(jnp.float32)).astype(x.dtype)


def _centred_short_conv(u, w, b):
    up = jnp.pad(u, ((0, 0), (1, 1), (0, 0)))
    return up[:, :-2] * w[0] + up[:, 1:-1] * w[1] + up[:, 2:] * w[2] + b


def _hyena_filter(L, w1, b1, f1, w2, b2, f2, w3):
    f32 = jnp.float32
    t = jnp.linspace(0.0, 1.0, L, dtype=f32)[:, None]
    t_r = jnp.arange(L, dtype=f32)[:, None]
    bands = jnp.linspace(1e-4, FILTER_BANDS - 1, FILTER_BANDS, dtype=f32)[None, :]
    ang = 2.0 * math.pi * bands * t_r / L
    z = jnp.concatenate([t, jnp.cos(ang), -jnp.sin(ang)], axis=-1)
    h = jnp.sin(f1.astype(f32) * (z @ w1.astype(f32) + b1.astype(f32)))
    h = jnp.sin(f2.astype(f32) * (h @ w2.astype(f32) + b2.astype(f32)))
    h = (h @ w3.astype(f32)).reshape(L, 2, HY_WIDTH)
    deltas = jnp.linspace(MIN_DECAY, MAX_DECAY, HY_WIDTH, dtype=f32)
    window = jnp.exp(-t * jnp.abs(deltas)[None, :]) + FILTER_SHIFT
    h = h * window[:, None, :]
    h_fwd, h_bwd = h[:, 0], h[:, 1]
    g = jnp.concatenate([h_fwd, jnp.zeros((1, HY_WIDTH), f32), h_bwd[:0:-1]], axis=0)
    return g / jnp.sum(jnp.abs(g), axis=0, keepdims=True)


def _bidir_long_conv(u, g, bias):
    L = u.shape[1]
    uf32 = u.astype(jnp.float32)
    uf = jnp.fft.rfft(uf32, n=2 * L, axis=1)
    gf = jnp.fft.rfft(g, axis=0)
    y = jnp.fft.irfft(uf * gf[None], n=2 * L, axis=1)[:, :L]
    return (y + bias.astype(jnp.float32) * uf32).astype(u.dtype)


def _rope_tables(L):
    pos = jnp.arange(L, dtype=jnp.float32)
    inv = ROPE_THETA ** (-jnp.arange(0, QK_ROPE, 2, dtype=jnp.float32) / QK_ROPE)
    ang = pos[:, None] * inv[None, :]
    return jnp.cos(ang), jnp.sin(ang)


def _rope(x, cos, sin):
    half = x.shape[-1] // 2
    x1 = x[..., :half].astype(jnp.float32)
    x2 = x[..., half:].astype(jnp.float32)
    return jnp.concatenate([x1 * cos - x2 * sin, x1 * sin + x2 * cos], axis=-1).astype(x.dtype)


def _mla(cq_raw, ckv_raw, kr_raw, g_q, g_kv, w_uq, w_uk, w_uv, cos, sin):
    B, L, _ = cq_raw.shape
    q = (_rmsnorm(cq_raw, g_q) @ w_uq).reshape(B, L, N_HEADS, QK_NOPE + QK_ROPE)
    q_nope = q[..., :QK_NOPE]
    q_rope = _rope(q[..., QK_NOPE:], cos[:, None, :], sin[:, None, :])
    c_kv = _rmsnorm(ckv_raw, g_kv)
    k_nope = jnp.einsum('blc,chd->blhd', c_kv, w_uk)
    v = jnp.einsum('blc,chd->blhd', c_kv, w_uv)
    k_rope = _rope(kr_raw, cos, sin)
    nb = L // Q_BLOCK
    qn = q_nope.reshape(B, nb, Q_BLOCK, N_HEADS, QK_NOPE).swapaxes(0, 1)
    qr = q_rope.reshape(B, nb, Q_BLOCK, N_HEADS, QK_ROPE).swapaxes(0, 1)
    scale = (QK_NOPE + QK_ROPE) ** -0.5

    def block(args):
        qn_b, qr_b = args
        s = (jnp.einsum('bqhd,bkhd->bhqk', qn_b, k_nope)
             + jnp.einsum('bqhr,bkr->bhqk', qr_b, k_rope))
        p = jax.nn.softmax(s.astype(jnp.float32) * scale, axis=-1).astype(v.dtype)
        return jnp.einsum('bhqk,bkhd->bqhd', p, v)

    o = lax.map(block, (qn, qr))
    return o.swapaxes(0, 1).reshape(B, L, MLA_WIDTH)


def _token_mixer(h, w_in, hy_conv_w, hy_conv_b, filt_w1, filt_b1, filt_freq1, filt_w2,
                 filt_b2, filt_freq2, filt_w3, hy_bias, q_norm_g, kv_norm_g, w_uq, w_uk,
                 w_uv, w_hy_out, w_mla_out, w_mix_out, cos, sin):
    B, L, _ = h.shape
    proj = h @ w_in
    hy = _centred_short_conv(proj[..., :OFF_Q], hy_conv_w, hy_conv_b)
    x0, x1, v = hy[..., :HY_WIDTH], hy[..., HY_WIDTH:2 * HY_WIDTH], hy[..., 2 * HY_WIDTH:]
    g = _hyena_filter(L, filt_w1, filt_b1, filt_freq1, filt_w2, filt_b2, filt_freq2, filt_w3)
    y_hy = x0 * _bidir_long_conv(v * x1, g, hy_bias)
    y_mla = _mla(proj[..., OFF_Q:OFF_KV], proj[..., OFF_KV:OFF_KR], proj[..., OFF_KR:OFF_GATE],
                 q_norm_g, kv_norm_g, w_uq, w_uk, w_uv, cos, sin)
    gates = jax.nn.sigmoid(proj[..., OFF_GATE:].reshape(B, L, N_BRANCH, D_MODEL))
    mix = gates[..., 0, :] * (y_hy @ w_hy_out) + gates[..., 1, :] * (y_mla @ w_mla_out)
    return mix @ w_mix_out


def _hier_moe(h, w_group, b_group, w_router, b_router, w_gate, w_up, w_down):
    B, L, D = h.shape
    t = h.reshape(B * L, D)
    n = t.shape[0]
    g_logits = (t @ w_group + b_group).astype(jnp.float32)
    p_group = jax.nn.softmax(g_logits, axis=-1)
    g_idx = jnp.argmax(g_logits, axis=-1)
    p_sel = jnp.max(p_group, axis=-1)
    e_logits = (t @ w_router + b_router).astype(jnp.float32).reshape(n, N_GROUPS, EXPERTS_PER_GROUP)
    e_in_group = e_logits[jnp.arange(n), g_idx]
    top_v, top_i = lax.top_k(e_in_group, TOP_K)
    w = jax.nn.softmax(top_v, axis=-1) * p_sel[:, None]
    e_idx = g_idx[:, None] * EXPERTS_PER_GROUP + top_i
    combine = jnp.einsum('nk,nke->ne', w, jax.nn.one_hot(e_idx, N_EXPERTS, dtype=jnp.float32))
    out = jnp.zeros((n, D), jnp.float32)
    for e in range(N_EXPERTS):
        a = jax.nn.silu(t @ w_gate[e]) * (t @ w_up[e])
        out = out + combine[:, e:e + 1] * (a @ w_down[e])
    return out.astype(h.dtype).reshape(B, L, D)


def setup_inputs(seed: int = 0) -> dict:
    key = jax.random.key(seed)
    keys = jax.random.split(key, 40)
    ctr = [0]

    def nk():
        k = keys[ctr[0]]
        ctr[0] += 1
        return k

    def nrm(shape, fan_in, s=1.0):
        return s * jax.random.normal(nk(), shape, jnp.float32) * fan_in ** -0.5

    def gain(shape):
        return 1.0 + 0.05 * jax.random.normal(nk(), shape, jnp.float32)

    def small(shape, s=0.02):
        return s * jax.random.normal(nk(), shape, jnp.float32)

    Dp, D = DEPTH, D_MODEL
    return {
        "x": jax.random.normal(nk(), (BATCH, SEQ, D), jnp.float32),
        "c": jax.random.normal(nk(), (BATCH, D), jnp.float32),
        "ada_w": nrm((Dp, D, 6 * D), D, 0.5),
        "ada_b": small((Dp, 6 * D)),
        "norm_mix_g": gain((Dp, D)),
        "w_in": nrm((Dp, D, IN_COLS), D),
        "hy_conv_w": nrm((Dp, SHORT_CONV, 3 * HY_WIDTH), SHORT_CONV),
        "hy_conv_b": small((Dp, 3 * HY_WIDTH)),
        "filt_w1": nrm((Dp, FILTER_EMB, FILTER_HIDDEN), FILTER_EMB),
        "filt_b1": small((Dp, FILTER_HIDDEN), 0.1),
        "filt_freq1": gain((Dp, FILTER_HIDDEN)),
        "filt_w2": nrm((Dp, FILTER_HIDDEN, FILTER_HIDDEN), FILTER_HIDDEN),
        "filt_b2": small((Dp, FILTER_HIDDEN), 0.1),
        "filt_freq2": gain((Dp, FILTER_HIDDEN)),
        "filt_w3": nrm((Dp, FILTER_HIDDEN, 2 * HY_WIDTH), FILTER_HIDDEN),
        "hy_bias": small((Dp, HY_WIDTH), 0.1),
        "q_norm_g": gain((Dp, Q_LORA)),
        "kv_norm_g": gain((Dp, KV_LORA)),
        "w_uq": nrm((Dp, Q_LORA, N_HEADS * (QK_NOPE + QK_ROPE)), Q_LORA),
        "w_uk": nrm((Dp, KV_LORA, N_HEADS, QK_NOPE), KV_LORA),
        "w_uv": nrm((Dp, KV_LORA, N_HEADS, V_HEAD), KV_LORA),
        "w_hy_out": nrm((Dp, HY_WIDTH, D), HY_WIDTH),
        "w_mla_out": nrm((Dp, MLA_WIDTH, D), MLA_WIDTH),
        "w_mix_out": nrm((Dp, D, D), D),
        "norm_ffn_g": gain((Dp, D)),
        "w_group": nrm((Dp, D, N_GROUPS), D),
        "b_group": small((Dp, N_GROUPS), 0.01),
        "w_router": nrm((Dp, D, N_EXPERTS), D),
        "b_router": small((Dp, N_EXPERTS), 0.01),
        "w_gate": nrm((Dp, N_EXPERTS, D, D_EXPERT), D),
        "w_up": nrm((Dp, N_EXPERTS, D, D_EXPERT), D),
        "w_down": nrm((Dp, N_EXPERTS, D_EXPERT, D), D_EXPERT),
        "final_norm_g": gain((D,)),
    }


def reference(x, c, ada_w, ada_b, norm_mix_g, w_in, hy_conv_w, hy_conv_b, filt_w1, filt_b1,
              filt_freq1, filt_w2, filt_b2, filt_freq2, filt_w3, hy_bias, q_norm_g, kv_norm_g,
              w_uq, w_uk, w_uv, w_hy_out, w_mla_out, w_mix_out, norm_ffn_g, w_group, b_group,
              w_router, b_router, w_gate, w_up, w_down, final_norm_g):
    L = x.shape[1]
    cos, sin = _rope_tables(L)
    c_act = jax.nn.silu(c)
    for l in range(DEPTH):
        mod = (c_act @ ada_w[l] + ada_b[l])[:, None, :]
        sh1, sc1, gt1, sh2, sc2, gt2 = jnp.split(mod, 6, axis=-1)
        h = _rmsnorm(x, norm_mix_g[l]) * (1.0 + sc1) + sh1
        x = x + gt1 * _token_mixer(h, w_in[l], hy_conv_w[l], hy_conv_b[l], filt_w1[l], filt_b1[l],
                                   filt_freq1[l], filt_w2[l], filt_b2[l], filt_freq2[l], filt_w3[l],
                                   hy_bias[l], q_norm_g[l], kv_norm_g[l], w_uq[l], w_uk[l], w_uv[l],
                                   w_hy_out[l], w_mla_out[l], w_mix_out[l], cos, sin)
        h = _rmsnorm(x, norm_ffn_g[l]) * (1.0 + sc2) + sh2
        x = x + gt2 * _hier_moe(h, w_group[l], b_group[l], w_router[l], b_router[l],
                                w_gate[l], w_up[l], w_down[l])
    return _rmsnorm(x, final_norm_g)
```

```python
import functools
import math

import jax
import jax.numpy as jnp
from jax import lax
from jax.experimental import pallas as pl
from jax.experimental.pallas import tpu as pltpu

F32 = jnp.float32
BF16 = jnp.bfloat16
HIGHEST = lax.Precision.HIGHEST

D_MODEL = 1024
EPS = 1e-6
HY_WIDTH = 512
FILTER_BANDS = 16
FILTER_EMB = 1 + 2 * FILTER_BANDS
FILTER_HIDDEN = 64
FILTER_DECAY_TARGET = 1e-2
FAST_DECAY_PCT = 0.3
SLOW_DECAY_PCT = 1.5
FILTER_SHIFT = 0.05
MIN_DECAY = math.log(FILTER_DECAY_TARGET) / SLOW_DECAY_PCT
MAX_DECAY = math.log(FILTER_DECAY_TARGET) / FAST_DECAY_PCT
N_HEADS = 8
QK_NOPE = 64
QK_ROPE = 32
V_HEAD = 64
Q_LORA = 384
KV_LORA = 256
ROPE_THETA = 10000.0
MLA_WIDTH = N_HEADS * V_HEAD
OFF_Q = 3 * HY_WIDTH
OFF_KV = OFF_Q + Q_LORA
OFF_KR = OFF_KV + KV_LORA
OFF_GATE = OFF_KR + QK_ROPE
N_GROUPS = 4
EXPERTS_PER_GROUP = 8
N_EXPERTS = N_GROUPS * EXPERTS_PER_GROUP
D_EXPERT = 256

LANES = 128
HEAD_SLAB = LANES
QK_SLABS = N_HEADS * HEAD_SLAB
ONES_LANE = V_HEAD
DFT_N1 = 256
DFT_K1 = DFT_N1 // 2 + 1
DFT_K1P = 144
NEG_BIG = -1e30
VMEM_LIMIT = 56 * 1024 * 1024

C_HY = 0
C_Q = C_HY + 3 * HY_WIDTH
C_KV = C_Q + Q_LORA
C_GATE = C_KV + KV_LORA
C_KR = C_GATE + 2 * D_MODEL
C_END = C_KR + HEAD_SLAB


def _cparams(*sem):
    return pltpu.CompilerParams(dimension_semantics=sem, vmem_limit_bytes=VMEM_LIMIT)


def _rms(x):
    return x * lax.rsqrt(jnp.mean(x * x, axis=-1, keepdims=True) + EPS)


def _ada_kernel(c_ref, w_ref, b_ref, o_ref):
    c = c_ref[...]
    ca = c * jax.nn.sigmoid(c)
    o_ref[...] = jnp.dot(ca, w_ref[...], preferred_element_type=F32, precision=HIGHEST) + b_ref[...]


def _ada(c8, w, b):
    d, n = w.shape
    tn = 1024
    return pl.pallas_call(
        _ada_kernel,
        grid=(n // tn,),
        in_specs=[pl.BlockSpec((8, d), lambda j: (0, 0)),
                  pl.BlockSpec((d, tn), lambda j: (0, j)),
                  pl.BlockSpec((1, tn), lambda j: (0, j))],
        out_specs=pl.BlockSpec((8, tn), lambda j: (0, j)),
        out_shape=jax.ShapeDtypeStruct((8, n), F32),
        compiler_params=_cparams("parallel"),
        name="ada",
    )(c8, w, b)


def _rope_slab(z, cos_t, sin_a, sin_b):
    return z * cos_t + pltpu.roll(z, HEAD_SLAB - QK_ROPE // 2, 1) * sin_a + pltpu.roll(z, QK_ROPE // 2, 1) * sin_b


def _inproj_kernel(x_ref, mod_ref, g_ref, wcat_ref, gq_ref, gkv_ref, wuq_ref, wuk_ref, wuv_ref,
                   cos_ref, sina_ref, sinb_ref,
                   hyp_ref, gate_ref, q_ref, k_ref, v_ref, *, qscale):
    d = D_MODEL
    x = x_ref[...]
    sh1 = mod_ref[0:1, 0:d]
    sc1 = mod_ref[0:1, d:2 * d]
    h = _rms(x) * g_ref[...] * (1.0 + sc1) + sh1
    proj = jnp.dot(h.astype(BF16), wcat_ref[...], preferred_element_type=F32)
    hyp_ref[...] = proj[:, C_HY:C_Q]
    gate_ref[...] = jax.nn.sigmoid(proj[:, C_GATE:C_KR]).astype(BF16)
    cq = _rms(proj[:, C_Q:C_KV]) * gq_ref[...]
    ckv = (_rms(proj[:, C_KV:C_GATE]) * gkv_ref[...]).astype(BF16)
    q = jnp.dot(cq.astype(BF16), wuq_ref[...], preferred_element_type=F32)
    kn = jnp.dot(ckv, wuk_ref[...], preferred_element_type=F32)
    vv = jnp.dot(ckv, wuv_ref[...], preferred_element_type=F32)
    cos_t = cos_ref[...]
    sin_a = sina_ref[...]
    sin_b = sinb_ref[...]
    kr = _rope_slab(proj[:, C_KR:C_END], cos_t, sin_a, sin_b)
    lane = lax.broadcasted_iota(jnp.int32, (1, HEAD_SLAB), 1)
    ones_col = jnp.where(lane == ONES_LANE, 1.0, 0.0).astype(F32)
    for hd in range(N_HEADS):
        sl = slice(hd * HEAD_SLAB, (hd + 1) * HEAD_SLAB)
        q_ref[:, sl] = (_rope_slab(q[:, sl], cos_t, sin_a, sin_b) * qscale).astype(BF16)
        k_ref[:, sl] = (kn[:, sl] + kr).astype(BF16)
        v_ref[:, sl] = (vv[:, sl] + ones_col).astype(BF16)


def _inproj(x2, mod, g1, wcat, gq, gkv, wuq, wuk, wuv, cos_t, sin_a, sin_b, tm):
    L, d = x2.shape
    qscale = (QK_NOPE + QK_ROPE) ** -0.5 * math.log2(math.e)
    full = lambda a: pl.BlockSpec(a.shape, lambda i: (0,) * a.ndim)
    row = lambda w: pl.BlockSpec((tm, w), lambda i: (i, 0))
    return pl.pallas_call(
        functools.partial(_inproj_kernel, qscale=qscale),
        grid=(L // tm,),
        in_specs=[row(d), full(mod), full(g1), full(wcat), full(gq), full(gkv), full(wuq), full(wuk),
                  full(wuv), row(HEAD_SLAB), row(HEAD_SLAB), row(HEAD_SLAB)],
        out_specs=[row(3 * HY_WIDTH), row(2 * D_MODEL), row(QK_SLABS), row(QK_SLABS), row(QK_SLABS)],
        out_shape=[jax.ShapeDtypeStruct((L, 3 * HY_WIDTH), F32),
                   jax.ShapeDtypeStruct((L, 2 * D_MODEL), BF16),
                   jax.ShapeDtypeStruct((L, QK_SLABS), BF16),
                   jax.ShapeDtypeStruct((L, QK_SLABS), BF16),
                   jax.ShapeDtypeStruct((L, QK_SLABS), BF16)],
        compiler_params=_cparams("parallel"),
        name="inproj",
    )(x2, mod, g1, wcat, gq, gkv, wuq, wuk, wuv, cos_t, sin_a, sin_b)


def _hyprep_kernel(p_ref, prev_ref, next_ref, w_ref, b_ref, u_ref, x0_ref):
    i = pl.program_id(0)
    n = pl.num_programs(0)
    p = p_ref[...]
    tm = p.shape[0]
    row = lax.broadcasted_iota(jnp.int32, (tm, 1), 0)
    prev_row = jnp.where(i == 0, 0.0, prev_ref[7:8, :])
    next_row = jnp.where(i == n - 1, 0.0, next_ref[0:1, :])
    up = jnp.where(row == 0, prev_row, pltpu.roll(p, 1, 0))
    dn = jnp.where(row == tm - 1, next_row, pltpu.roll(p, tm - 1, 0))
    hy = up * w_ref[0:1, :] + p * w_ref[1:2, :] + dn * w_ref[2:3, :] + b_ref[...]
    w = HY_WIDTH
    x0_ref[...] = hy[:, 0:w].astype(BF16)
    u_ref[...] = (hy[:, 2 * w:3 * w] * hy[:, w:2 * w]).astype(BF16)


def _hyprep(hyp, conv_w, conv_b, tm):
    L, c = hyp.shape
    nb8 = L // 8
    r8 = tm // 8
    return pl.pallas_call(
        _hyprep_kernel,
        grid=(L // tm,),
        in_specs=[pl.BlockSpec((tm, c), lambda i: (i, 0)),
                  pl.BlockSpec((8, c), lambda i: (jnp.maximum(i * r8 - 1, 0), 0)),
                  pl.BlockSpec((8, c), lambda i: (jnp.minimum((i + 1) * r8, nb8 - 1), 0)),
                  pl.BlockSpec(conv_w.shape, lambda i: (0, 0)),
                  pl.BlockSpec(conv_b.shape, lambda i: (0, 0))],
        out_specs=[pl.BlockSpec((tm, HY_WIDTH), lambda i: (i, 0)),
                   pl.BlockSpec((tm, HY_WIDTH), lambda i: (i, 0))],
        out_shape=[jax.ShapeDtypeStruct((L, HY_WIDTH), BF16),
                   jax.ShapeDtypeStruct((L, HY_WIDTH), BF16)],
        compiler_params=_cparams("parallel"),
        name="hyprep",
    )(hyp, hyp, hyp, conv_w, conv_b)


def _filt_kernel(z_ref, w1_ref, b1_ref, f1_ref, w2_ref, b2_ref, f2_ref, w3_ref, dl_ref,
                 h_ref, nrm_ref):
    i = pl.program_id(0)
    z = z_ref[...]
    tm = z.shape[0]
    h = jnp.sin(f1_ref[...] * (jnp.dot(z, w1_ref[...], preferred_element_type=F32, precision=HIGHEST)
                               + b1_ref[...]))
    h = jnp.sin(f2_ref[...] * (jnp.dot(h, w2_ref[...], preferred_element_type=F32, precision=HIGHEST)
                               + b2_ref[...]))
    h3 = jnp.dot(h, w3_ref[...], preferred_element_type=F32, precision=HIGHEST)
    t = z[:, 0:1]
    window = jnp.exp(-t * dl_ref[...]) + FILTER_SHIFT
    hf = h3[:, 0:HY_WIDTH] * window
    hb = h3[:, HY_WIDTH:2 * HY_WIDTH] * window
    row = lax.broadcasted_iota(jnp.int32, (tm, 1), 0)
    hb = jnp.where((row == 0) & (i == 0), 0.0, hb)
    h_ref[:, 0:HY_WIDTH] = hf.astype(BF16)
    h_ref[:, HY_WIDTH:2 * HY_WIDTH] = hb.astype(BF16)
    part = jnp.sum(jnp.abs(hf) + jnp.abs(hb), axis=0, keepdims=True)

    @pl.when(i == 0)
    def _():
        nrm_ref[...] = jnp.zeros_like(nrm_ref)

    nrm_ref[...] += part


def _filt(z, w1, b1, f1, w2, b2, f2, w3, dl, tm):
    L = z.shape[0]
    full = lambda a: pl.BlockSpec(a.shape, lambda i: (0,) * a.ndim)
    return pl.pallas_call(
        _filt_kernel,
        grid=(L // tm,),
        in_specs=[pl.BlockSpec((tm, z.shape[1]), lambda i: (i, 0)), full(w1), full(b1), full(f1),
                  full(w2), full(b2), full(f2), full(w3), full(dl)],
        out_specs=[pl.BlockSpec((tm, 2 * HY_WIDTH), lambda i: (i, 0)),
                   pl.BlockSpec((1, HY_WIDTH), lambda i: (0, 0))],
        out_shape=[jax.ShapeDtypeStruct((L, 2 * HY_WIDTH), BF16),
                   jax.ShapeDtypeStruct((1, HY_WIDTH), F32)],
        compiler_params=_cparams("arbitrary"),
        name="filt",
    )(z, w1, b1, f1, w2, b2, f2, w3, dl)


def _mm_kernel(a_ref, b_ref, o_ref):
    o_ref[...] = jnp.dot(a_ref[...], b_ref[...], preferred_element_type=F32).astype(o_ref.dtype)


def _dft1(f1m, xv, tn):
    m, k = f1m.shape
    n = xv.shape[1]
    return pl.pallas_call(
        _mm_kernel,
        grid=(n // tn,),
        in_specs=[pl.BlockSpec((m, k), lambda j: (0, 0)),
                  pl.BlockSpec((k, tn), lambda j: (0, j))],
        out_specs=pl.BlockSpec((m, tn), lambda j: (0, j)),
        out_shape=jax.ShapeDtypeStruct((m, n), BF16),
        compiler_params=_cparams("parallel"),
        name="dft1",
    )(f1m, xv)


def _spec_kernel(au_ref, ah_ref, mf_ref, mi_ref, nrm_ref, z_ref, *, kb):
    w = HY_WIDTH
    n2 = au_ref.shape[2]
    inv = 1.0 / nrm_ref[...]
    for j in range(kb):
        a_u = au_ref[:, j].reshape(2 * n2, w)
        a_h = ah_ref[:, j].reshape(2 * n2, 2 * w)
        mf = mf_ref[j]
        xu = jnp.dot(mf, a_u, preferred_element_type=F32)
        xh = jnp.dot(mf, a_h, preferred_element_type=F32)
        ur, ui = xu[0:n2], xu[n2:2 * n2]
        gr = (xh[0:n2, 0:w] + xh[0:n2, w:2 * w]) * inv
        gi = (xh[n2:2 * n2, 0:w] - xh[n2:2 * n2, w:2 * w]) * inv
        yr = ur * gr - ui * gi
        yi = ur * gi + ui * gr
        y = jnp.concatenate([yr, yi], axis=0).astype(BF16)
        z = jnp.dot(mi_ref[j], y, preferred_element_type=F32)
        z_ref[0, j] = z[0:n2].astype(BF16)
        z_ref[1, j] = z[n2:2 * n2].astype(BF16)


def _spec(au, ah, mf, mi, nrm, kb):
    _, k1p, n2, w = au.shape
    return pl.pallas_call(
        functools.partial(_spec_kernel, kb=kb),
        grid=(k1p // kb,),
        in_specs=[pl.BlockSpec((2, kb, n2, w), lambda s: (0, s, 0, 0)),
                  pl.BlockSpec((2, kb, n2, 2 * w), lambda s: (0, s, 0, 0)),
                  pl.BlockSpec((kb, 2 * n2, 2 * n2), lambda s: (s, 0, 0)),
                  pl.BlockSpec((kb, 2 * n2, 2 * n2), lambda s: (s, 0, 0)),
                  pl.BlockSpec((1, w), lambda s: (0, 0))],
        out_specs=pl.BlockSpec((2, kb, n2, w), lambda s: (0, s, 0, 0)),
        out_shape=jax.ShapeDtypeStruct((2, k1p, n2, w), BF16),
        compiler_params=_cparams("parallel"),
        name="spec",
    )(au, ah, mf, mi, nrm)


def _idft1_kernel(c_ref, z_ref, u_ref, x0_ref, bias_ref, o_ref, *, inv_n):
    y = jnp.dot(c_ref[...], z_ref[...], preferred_element_type=F32) * inv_n
    u = u_ref[...].astype(F32)
    o_ref[...] = (x0_ref[...].astype(F32) * (y + bias_ref[...] * u)).astype(BF16)


def _idft1(c2, zv, uv, x0v, bias_t, tn, inv_n):
    m, k = c2.shape
    n = zv.shape[1]
    return pl.pallas_call(
        functools.partial(_idft1_kernel, inv_n=inv_n),
        grid=(n // tn,),
        in_specs=[pl.BlockSpec((m, k), lambda j: (0, 0)),
                  pl.BlockSpec((k, tn), lambda j: (0, j)),
                  pl.BlockSpec((m, tn), lambda j: (0, j)),
                  pl.BlockSpec((m, tn), lambda j: (0, j)),
                  pl.BlockSpec((1, tn), lambda j: (0, 0))],
        out_specs=pl.BlockSpec((m, tn), lambda j: (0, j)),
        out_shape=jax.ShapeDtypeStruct((m, n), BF16),
        compiler_params=_cparams("parallel"),
        name="idft1",
    )(c2, zv, uv, x0v, bias_t)


def _attn_kernel(q_ref, k_ref, v_ref, o_ref, *, tk):
    q = q_ref[...]
    tq = q.shape[0]
    nk = k_ref.shape[0] // tk

    def body(j, carry):
        m, acc = carry
        off = pl.multiple_of(j * tk, tk)
        kc = k_ref[pl.ds(off, tk), :]
        vc = v_ref[pl.ds(off, tk), :]
        s = lax.dot_general(q, kc, (((1,), (1,)), ((), ())), preferred_element_type=F32)
        m_new = jnp.maximum(m, jnp.max(s, axis=1, keepdims=True))
        alpha = jnp.exp2(m - m_new)
        p = jnp.exp2(s - m_new)
        acc = alpha * acc + jnp.dot(p.astype(BF16), vc, preferred_element_type=F32)
        return m_new, acc

    m0 = jnp.full((tq, 1), NEG_BIG, F32)
    acc0 = jnp.zeros((tq, HEAD_SLAB), F32)
    _, acc = lax.fori_loop(0, nk, body, (m0, acc0))
    o_ref[...] = (acc / acc[:, ONES_LANE:ONES_LANE + 1]).astype(BF16)


def _attn(q, k, v, tq, tk):
    L = q.shape[0]
    return pl.pallas_call(
        functools.partial(_attn_kernel, tk=tk),
        grid=(N_HEADS, L // tq),
        in_specs=[pl.BlockSpec((tq, HEAD_SLAB), lambda h, i: (i, h)),
                  pl.BlockSpec((L, HEAD_SLAB), lambda h, i: (0, h)),
                  pl.BlockSpec((L, HEAD_SLAB), lambda h, i: (0, h))],
        out_specs=pl.BlockSpec((tq, HEAD_SLAB), lambda h, i: (i, h)),
        out_shape=jax.ShapeDtypeStruct((L, QK_SLABS), BF16),
        compiler_params=_cparams("parallel", "parallel"),
        name="attn",
    )(q, k, v)


def _merge_kernel(x_ref, yh_ref, ym_ref, gate_ref, mod_ref, why_ref, wmla_ref, wmix_ref, g2_ref,
                  wrt_ref, brt_ref, x1_ref, h2_ref, comb_ref):
    d = D_MODEL
    a = jnp.dot(yh_ref[...], why_ref[...], preferred_element_type=F32)
    b = jnp.dot(ym_ref[...], wmla_ref[...], preferred_element_type=F32)
    mix = gate_ref[:, 0:d].astype(F32) * a + gate_ref[:, d:2 * d].astype(F32) * b
    o = jnp.dot(mix.astype(BF16), wmix_ref[...], preferred_element_type=F32)
    gt1 = mod_ref[0:1, 2 * d:3 * d]
    sh2 = mod_ref[0:1, 3 * d:4 * d]
    sc2 = mod_ref[0:1, 4 * d:5 * d]
    x1 = x_ref[...] + gt1 * o
    x1_ref[...] = x1
    h2 = _rms(x1) * g2_ref[...] * (1.0 + sc2) + sh2
    h2_ref[...] = h2.astype(BF16)
    lg = jnp.dot(h2, wrt_ref[...], preferred_element_type=F32, precision=HIGHEST) + brt_ref[...]
    e = lg[:, 0:LANES]
    gl = lg[:, LANES:2 * LANES]
    lane = lax.broadcasted_iota(jnp.int32, e.shape, 1).astype(F32)
    gvalid = lane < N_GROUPS
    glm = jnp.where(gvalid, gl, NEG_BIG)
    gmax = jnp.max(glm, axis=1, keepdims=True)
    gidx = jnp.min(jnp.where(glm == gmax, lane, float(LANES)), axis=1, keepdims=True)
    psum = jnp.sum(jnp.where(gvalid, jnp.exp(glm - gmax), 0.0), axis=1, keepdims=True)
    p_sel = 1.0 / psum
    lo = gidx * EXPERTS_PER_GROUP
    em = jnp.where((lane >= lo) & (lane < lo + EXPERTS_PER_GROUP), e, NEG_BIG)
    v1 = jnp.max(em, axis=1, keepdims=True)
    i1 = jnp.min(jnp.where(em == v1, lane, float(LANES)), axis=1, keepdims=True)
    em2 = jnp.where(lane == i1, NEG_BIG, em)
    v2 = jnp.max(em2, axis=1, keepdims=True)
    i2 = jnp.min(jnp.where(em2 == v2, lane, float(LANES)), axis=1, keepdims=True)
    t = jnp.exp(v2 - v1)
    w1 = p_sel / (1.0 + t)
    w2 = p_sel * t / (1.0 + t)
    comb_ref[...] = jnp.where(lane == i1, w1, 0.0) + jnp.where(lane == i2, w2, 0.0)


def _merge(x2, yh, ym, gates, mod, why, wmla, wmix, g2, wrt, brt, tm):
    L, d = x2.shape
    full = lambda a: pl.BlockSpec(a.shape, lambda i: (0,) * a.ndim)
    row = lambda w: pl.BlockSpec((tm, w), lambda i: (i, 0))
    return pl.pallas_call(
        _merge_kernel,
        grid=(L // tm,),
        in_specs=[row(d), row(HY_WIDTH), row(QK_SLABS), row(2 * d), full(mod), full(why), full(wmla),
                  full(wmix), full(g2), full(wrt), full(brt)],
        out_specs=[row(d), row(d), row(LANES)],
        out_shape=[jax.ShapeDtypeStruct((L, d), F32),
                   jax.ShapeDtypeStruct((L, d), BF16),
                   jax.ShapeDtypeStruct((L, LANES), F32)],
        compiler_params=_cparams("parallel"),
        name="merge",
    )(x2, yh, ym, gates, mod, why, wmla, wmix, g2, wrt, brt)


def _moe_kernel(h_ref, comb_ref, wgu_ref, wd_ref, x1_ref, mod_ref, fg_ref, o_ref, acc_ref):
    d = D_MODEL
    e = pl.program_id(1)

    @pl.when(e == 0)
    def _():
        acc_ref[...] = jnp.zeros_like(acc_ref)

    h = h_ref[...]
    gu = jnp.dot(h, wgu_ref[0], preferred_element_type=F32)
    g = gu[:, 0:D_EXPERT]
    a = g * jax.nn.sigmoid(g) * gu[:, D_EXPERT:2 * D_EXPERT]
    comb = comb_ref[...]
    lane = lax.broadcasted_iota(jnp.int32, comb.shape, 1)
    col = jnp.sum(jnp.where(lane == e, comb, 0.0), axis=1, keepdims=True)
    acc_ref[...] += jnp.dot((a * col).astype(BF16), wd_ref[0], preferred_element_type=F32)

    @pl.when(e == pl.num_programs(1) - 1)
    def _():
        gt2 = mod_ref[0:1, 5 * d:6 * d]
        x2 = x1_ref[...] + gt2 * acc_ref[...]
        o_ref[...] = _rms(x2) * fg_ref[...]


def _moe(h2, comb, wgu, wd, x1, mod, fg, tm):
    L, d = x1.shape
    ne = wgu.shape[0]
    return pl.pallas_call(
        _moe_kernel,
        grid=(L // tm, ne),
        in_specs=[pl.BlockSpec((tm, d), lambda i, e: (i, 0)),
                  pl.BlockSpec((tm, LANES), lambda i, e: (i, 0)),
                  pl.BlockSpec((1, d, 2 * D_EXPERT), lambda i, e: (e, 0, 0)),
                  pl.BlockSpec((1, D_EXPERT, d), lambda i, e: (e, 0, 0)),
                  pl.BlockSpec((tm, d), lambda i, e: (i, 0)),
                  pl.BlockSpec(mod.shape, lambda i, e: (0, 0)),
                  pl.BlockSpec(fg.shape, lambda i, e: (0, 0))],
        out_specs=pl.BlockSpec((tm, d), lambda i, e: (i, 0)),
        out_shape=jax.ShapeDtypeStruct((L, d), F32),
        scratch_shapes=[pltpu.VMEM((tm, d), F32)],
        compiler_params=_cparams("parallel", "arbitrary"),
        name="moe",
    )(h2, comb, wgu, wd, x1, mod, fg)


def _pad_heads(w, width):
    lead = w.shape[:-1]
    w = w.reshape(lead + (N_HEADS, width))
    w = jnp.pad(w, [(0, 0)] * len(lead) + [(0, 0), (0, HEAD_SLAB - width)])
    return w.reshape(lead + (N_HEADS * HEAD_SLAB,))


def _rope_slab_tables(L):
    half = QK_ROPE // 2
    pos = jnp.arange(L, dtype=F32)
    inv = ROPE_THETA ** (-jnp.arange(0, QK_ROPE, 2, dtype=F32) / QK_ROPE)
    ang = pos[:, None] * inv[None, :]
    cos, sin = jnp.cos(ang), jnp.sin(ang)
    z = lambda w: jnp.zeros((L, w), F32)
    cos_t = jnp.concatenate([jnp.ones((L, QK_NOPE), F32), cos, cos, z(HEAD_SLAB - QK_NOPE - QK_ROPE)], axis=1)
    sin_a = jnp.concatenate([z(QK_NOPE), -sin, z(HEAD_SLAB - QK_NOPE - half)], axis=1)
    sin_b = jnp.concatenate([z(QK_NOPE + half), sin, z(HEAD_SLAB - QK_NOPE - QK_ROPE)], axis=1)
    return cos_t, sin_a, sin_b


def _filter_features(L):
    t = jnp.linspace(0.0, 1.0, L, dtype=F32)[:, None]
    t_r = jnp.arange(L, dtype=F32)[:, None]
    bands = jnp.linspace(1e-4, FILTER_BANDS - 1, FILTER_BANDS, dtype=F32)[None, :]
    ang = 2.0 * math.pi * bands * t_r / L
    z = jnp.concatenate([t, jnp.cos(ang), -jnp.sin(ang)], axis=-1)
    return jnp.pad(z, ((0, 0), (0, FILTER_HIDDEN - FILTER_EMB)))


def _dft_tables(L):
    n = 2 * L
    n1 = DFT_N1
    n2 = n // n1
    two_pi = 2.0 * math.pi
    k1 = jnp.arange(DFT_K1P, dtype=jnp.int32)[:, None]
    m1 = jnp.arange(n1 // 2, dtype=jnp.int32)[None, :]
    ang1 = two_pi * ((k1 * m1) % n1).astype(F32) / n1
    live = (k1 < DFT_K1).astype(F32)
    f1m = jnp.concatenate([jnp.cos(ang1) * live, -jnp.sin(ang1) * live], axis=0).astype(BF16)
    kk1 = jnp.arange(DFT_K1P, dtype=jnp.int32)[:, None, None]
    k2 = jnp.arange(n2, dtype=jnp.int32)[None, :, None]
    m2 = jnp.arange(n2, dtype=jnp.int32)[None, None, :]
    ang2 = two_pi * ((m2 * (n1 * k2 + kk1)) % n).astype(F32) / n
    live2 = (kk1 < DFT_K1).astype(F32)
    c2, s2 = jnp.cos(ang2) * live2, jnp.sin(ang2) * live2
    mr, mi = c2, -s2
    mf = jnp.concatenate([jnp.concatenate([mr, -mi], axis=2),
                          jnp.concatenate([mi, mr], axis=2)], axis=1).astype(BF16)
    pr, pi_ = jnp.swapaxes(c2, 1, 2), jnp.swapaxes(s2, 1, 2)
    minv = jnp.concatenate([jnp.concatenate([pr, -pi_], axis=2),
                            jnp.concatenate([pi_, pr], axis=2)], axis=1).astype(BF16)
    o1 = jnp.arange(n1 // 2, dtype=jnp.int32)[:, None]
    q1 = jnp.arange(DFT_K1P, dtype=jnp.int32)[None, :]
    ang3 = two_pi * ((o1 * q1) % n1).astype(F32) / n1
    wgt = jnp.where((q1 == 0) | (q1 == n1 // 2), 1.0, 2.0) * (q1 < DFT_K1).astype(F32)
    cmat = jnp.concatenate([wgt * jnp.cos(ang3), -wgt * jnp.sin(ang3)], axis=1).astype(BF16)
    return f1m, mf, minv, cmat, n2


def _pick(n, pref):
    t = pref
    while n % t:
        t //= 2
    return t


def kernel(x, c, ada_w, ada_b, norm_mix_g, w_in, hy_conv_w, hy_conv_b, filt_w1, filt_b1, filt_freq1, filt_w2, filt_b2, filt_freq2, filt_w3, hy_bias, q_norm_g, kv_norm_g, w_uq, w_uk, w_uv, w_hy_out, w_mla_out, w_mix_out, norm_ffn_g, w_group, b_group, w_router, b_router, w_gate, w_up, w_down, final_norm_g):
    B, L, d = x.shape
    assert B == 1 and d == D_MODEL and ada_w.shape[0] == 1
    x2 = x.reshape(L, d)
    row = lambda v: v.reshape(1, -1)

    mod = _ada(jnp.pad(c, ((0, 8 - B), (0, 0))), ada_w[0], row(ada_b[0]))

    wi = w_in[0]
    kr_slab = jnp.pad(wi[:, OFF_KR:OFF_GATE], ((0, 0), (QK_NOPE, HEAD_SLAB - QK_NOPE - QK_ROPE)))
    wcat = jnp.concatenate([wi[:, 0:OFF_Q], wi[:, OFF_Q:OFF_KV], wi[:, OFF_KV:OFF_KR],
                            wi[:, OFF_GATE:], kr_slab], axis=1).astype(BF16)
    wuq = _pad_heads(w_uq[0], QK_NOPE + QK_ROPE).astype(BF16)
    wuk = _pad_heads(w_uk[0].reshape(KV_LORA, N_HEADS * QK_NOPE), QK_NOPE).astype(BF16)
    wuv = _pad_heads(w_uv[0].reshape(KV_LORA, N_HEADS * V_HEAD), V_HEAD).astype(BF16)
    cos_t, sin_a, sin_b = _rope_slab_tables(L)

    tm = _pick(L, 256)
    hyp, gates, q, k, v = _inproj(x2, mod, row(norm_mix_g[0]), wcat, row(q_norm_g[0]), row(kv_norm_g[0]),
                                  wuq, wuk, wuv, cos_t, sin_a, sin_b, tm)

    u, x0 = _hyprep(hyp, hy_conv_w[0], row(hy_conv_b[0]), _pick(L, 512))
    zf = _filter_features(L)
    w1p = jnp.pad(filt_w1[0], ((0, FILTER_HIDDEN - FILTER_EMB), (0, 0)))
    dl = jnp.abs(jnp.linspace(MIN_DECAY, MAX_DECAY, HY_WIDTH, dtype=F32))[None, :]
    hfilt, nrm = _filt(zf, w1p, row(filt_b1[0]), row(filt_freq1[0]), filt_w2[0], row(filt_b2[0]),
                       row(filt_freq2[0]), filt_w3[0], dl, _pick(L, 512))
    f1m, mf, minv, cmat, n2 = _dft_tables(L)
    half = DFT_N1 // 2
    au = _dft1(f1m, u.reshape(half, n2 * HY_WIDTH), _pick(n2 * HY_WIDTH, 4096))
    ah = _dft1(f1m, hfilt.reshape(half, n2 * 2 * HY_WIDTH), _pick(n2 * 2 * HY_WIDTH, 4096))
    au = au.reshape(2, DFT_K1P, n2, HY_WIDTH)
    ah = ah.reshape(2, DFT_K1P, n2, 2 * HY_WIDTH)
    zs = _spec(au, ah, mf, minv, nrm, 3)
    tn = _pick(n2 * HY_WIDTH, 4096)
    bias_t = jnp.tile(row(hy_bias[0]), (1, tn // HY_WIDTH))
    yh = _idft1(cmat, zs.reshape(2 * DFT_K1P, n2 * HY_WIDTH), u.reshape(half, n2 * HY_WIDTH),
                x0.reshape(half, n2 * HY_WIDTH), bias_t, tn, 1.0 / (2 * L))
    yh = yh.reshape(L, HY_WIDTH)

    ym = _attn(q, k, v, _pick(L, 512), _pick(L, 512))

    wmla = jnp.pad(w_mla_out[0].reshape(N_HEADS, V_HEAD, d),
                   ((0, 0), (0, HEAD_SLAB - V_HEAD), (0, 0))).reshape(QK_SLABS, d).astype(BF16)
    wrt = jnp.concatenate([jnp.pad(w_router[0], ((0, 0), (0, LANES - N_EXPERTS))),
                           jnp.pad(w_group[0], ((0, 0), (0, LANES - N_GROUPS)))], axis=1)
    brt = jnp.concatenate([jnp.pad(b_router[0], (0, LANES - N_EXPERTS)),
                           jnp.pad(b_group[0], (0, LANES - N_GROUPS))])[None, :]
    x1, h2, comb = _merge(x2, yh, ym, gates, mod, w_hy_out[0].astype(BF16), wmla,
                          w_mix_out[0].astype(BF16), row(norm_ffn_g[0]), wrt, brt, _pick(L, 256))

    wgu = jnp.concatenate([w_gate[0], w_up[0]], axis=2).astype(BF16)
    out = _moe(h2, comb, wgu, w_down[0].astype(BF16), x1, mod, row(final_norm_g), _pick(L, 1024))
    return out.reshape(B, L, d)
```

```python
import functools
import math

import jax
import jax.numpy as jnp
from jax import lax
from jax.experimental import pallas as pl
from jax.experimental.pallas import tpu as pltpu

F32 = jnp.float32
BF16 = jnp.bfloat16
HIGHEST = lax.Precision.HIGHEST

D_MODEL = 1024
EPS = 1e-6
HY_WIDTH = 512
FILTER_BANDS = 16
FILTER_EMB = 1 + 2 * FILTER_BANDS
FILTER_HIDDEN = 64
FILTER_DECAY_TARGET = 1e-2
FAST_DECAY_PCT = 0.3
SLOW_DECAY_PCT = 1.5
FILTER_SHIFT = 0.05
MIN_DECAY = math.log(FILTER_DECAY_TARGET) / SLOW_DECAY_PCT
MAX_DECAY = math.log(FILTER_DECAY_TARGET) / FAST_DECAY_PCT
N_HEADS = 8
QK_NOPE = 64
QK_ROPE = 32
V_HEAD = 64
Q_LORA = 384
KV_LORA = 256
ROPE_THETA = 10000.0
MLA_WIDTH = N_HEADS * V_HEAD
OFF_Q = 3 * HY_WIDTH
OFF_KV = OFF_Q + Q_LORA
OFF_KR = OFF_KV + KV_LORA
OFF_GATE = OFF_KR + QK_ROPE
N_GROUPS = 4
EXPERTS_PER_GROUP = 8
N_EXPERTS = N_GROUPS * EXPERTS_PER_GROUP
D_EXPERT = 256

LANES = 128
HEAD_SLAB = LANES
QK_SLABS = N_HEADS * HEAD_SLAB
ONES_LANE = V_HEAD
DFT_N1 = 256
DFT_K1 = DFT_N1 // 2 + 1
DFT_K1P = 144
NEG_BIG = -1e30
VMEM_LIMIT = 56 * 1024 * 1024

C_HY = 0
C_Q = C_HY + 3 * HY_WIDTH
C_KV = C_Q + Q_LORA
C_GATE = C_KV + KV_LORA
C_KR = C_GATE + 2 * D_MODEL
C_END = C_KR + HEAD_SLAB


def _cparams(*sem):
    return pltpu.CompilerParams(dimension_semantics=sem, vmem_limit_bytes=VMEM_LIMIT)


def _rms(x):
    return x * lax.rsqrt(jnp.mean(x * x, axis=-1, keepdims=True) + EPS)


def _ada_kernel(c_ref, w_ref, b_ref, o_ref):
    c = c_ref[...]
    ca = c * jax.nn.sigmoid(c)
    o_ref[...] = jnp.dot(ca, w_ref[...], preferred_element_type=F32, precision=HIGHEST) + b_ref[...]


def _ada(c8, w, b):
    d, n = w.shape
    tn = 1024
    return pl.pallas_call(
        _ada_kernel,
        grid=(n // tn,),
        in_specs=[pl.BlockSpec((8, d), lambda j: (0, 0)),
                  pl.BlockSpec((d, tn), lambda j: (0, j)),
                  pl.BlockSpec((1, tn), lambda j: (0, j))],
        out_specs=pl.BlockSpec((8, tn), lambda j: (0, j)),
        out_shape=jax.ShapeDtypeStruct((8, n), F32),
        compiler_params=_cparams("parallel"),
        name="ada",
    )(c8, w, b)


def _rope_slab(z, cos_t, sin_a, sin_b):
    return z * cos_t + pltpu.roll(z, HEAD_SLAB - QK_ROPE // 2, 1) * sin_a + pltpu.roll(z, QK_ROPE // 2, 1) * sin_b


def _inproj_kernel(x_ref, mod_ref, g_ref, wcat_ref, gq_ref, gkv_ref, wuq_ref, wuk_ref, wuv_ref,
                   cos_ref, sina_ref, sinb_ref,
                   hyp_ref, gate_ref, q_ref, k_ref, v_ref, *, qscale):
    d = D_MODEL
    x = x_ref[...]
    sh1 = mod_ref[0:1, 0:d]
    sc1 = mod_ref[0:1, d:2 * d]
    h = _rms(x) * g_ref[...] * (1.0 + sc1) + sh1
    proj = jnp.dot(h.astype(BF16), wcat_ref[...], preferred_element_type=F32)
    hyp_ref[...] = proj[:, C_HY:C_Q]
    gate_ref[...] = jax.nn.sigmoid(proj[:, C_GATE:C_KR]).astype(BF16)
    cq = _rms(proj[:, C_Q:C_KV]) * gq_ref[...]
    ckv = (_rms(proj[:, C_KV:C_GATE]) * gkv_ref[...]).astype(BF16)
    q = jnp.dot(cq.astype(BF16), wuq_ref[...], preferred_element_type=F32)
    kn = jnp.dot(ckv, wuk_ref[...], preferred_element_type=F32)
    vv = jnp.dot(ckv, wuv_ref[...], preferred_element_type=F32)
    cos_t = cos_ref[...]
    sin_a = sina_ref[...]
    sin_b = sinb_ref[...]
    kr = _rope_slab(proj[:, C_KR:C_END], cos_t, sin_a, sin_b)
    lane = lax.broadcasted_iota(jnp.int32, (1, HEAD_SLAB), 1)
    ones_col = jnp.where(lane == ONES_LANE, 1.0, 0.0).astype(F32)
    for hd in range(N_HEADS):
        sl = slice(hd * HEAD_SLAB, (hd + 1) * HEAD_SLAB)
        q_ref[:, sl] = (_rope_slab(q[:, sl], cos_t, sin_a, sin_b) * qscale).astype(BF16)
        k_ref[:, sl] = (kn[:, sl] + kr).astype(BF16)
        v_ref[:, sl] = (vv[:, sl] + ones_col).astype(BF16)


def _inproj(x2, mod, g1, wcat, gq, gkv, wuq, wuk, wuv, cos_t, sin_a, sin_b, tm):
    L, d = x2.shape
    qscale = (QK_NOPE + QK_ROPE) ** -0.5 * math.log2(math.e)
    full = lambda a: pl.BlockSpec(a.shape, lambda i: (0,) * a.ndim)
    row = lambda w: pl.BlockSpec((tm, w), lambda i: (i, 0))
    return pl.pallas_call(
        functools.partial(_inproj_kernel, qscale=qscale),
        grid=(L // tm,),
        in_specs=[row(d), full(mod), full(g1), full(wcat), full(gq), full(gkv), full(wuq), full(wuk),
                  full(wuv), row(HEAD_SLAB), row(HEAD_SLAB), row(HEAD_SLAB)],
        out_specs=[row(3 * HY_WIDTH), row(2 * D_MODEL), row(QK_SLABS), row(QK_SLABS), row(QK_SLABS)],
        out_shape=[jax.ShapeDtypeStruct((L, 3 * HY_WIDTH), F32),
                   jax.ShapeDtypeStruct((L, 2 * D_MODEL), BF16),
                   jax.ShapeDtypeStruct((L, QK_SLABS), BF16),
                   jax.ShapeDtypeStruct((L, QK_SLABS), BF16),
                   jax.ShapeDtypeStruct((L, QK_SLABS), BF16)],
        compiler_params=_cparams("parallel"),
        name="inproj",
    )(x2, mod, g1, wcat, gq, gkv, wuq, wuk, wuv, cos_t, sin_a, sin_b)


def _hyprep_kernel(p_ref, prev_ref, next_ref, w_ref, b_ref, u_ref, x0_ref):
    i = pl.program_id(0)
    n = pl.num_programs(0)
    p = p_ref[...]
    tm = p.shape[0]
    row = lax.broadcasted_iota(jnp.int32, (tm, 1), 0)
    prev_row = jnp.where(i == 0, 0.0, prev_ref[7:8, :])
    next_row = jnp.where(i == n - 1, 0.0, next_ref[0:1, :])
    up = jnp.where(row == 0, prev_row, pltpu.roll(p, 1, 0))
    dn = jnp.where(row == tm - 1, next_row, pltpu.roll(p, tm - 1, 0))
    hy = up * w_ref[0:1, :] + p * w_ref[1:2, :] + dn * w_ref[2:3, :] + b_ref[...]
    w = HY_WIDTH
    x0_ref[...] = hy[:, 0:w].astype(BF16)
    u_ref[...] = (hy[:, 2 * w:3 * w] * hy[:, w:2 * w]).astype(BF16)


def _hyprep(hyp, conv_w, conv_b, tm):
    L, c = hyp.shape
    nb8 = L // 8
    r8 = tm // 8
    return pl.pallas_call(
        _hyprep_kernel,
        grid=(L // tm,),
        in_specs=[pl.BlockSpec((tm, c), lambda i: (i, 0)),
                  pl.BlockSpec((8, c), lambda i: (jnp.maximum(i * r8 - 1, 0), 0)),
                  pl.BlockSpec((8, c), lambda i: (jnp.minimum((i + 1) * r8, nb8 - 1), 0)),
                  pl.BlockSpec(conv_w.shape, lambda i: (0, 0)),
                  pl.BlockSpec(conv_b.shape, lambda i: (0, 0))],
        out_specs=[pl.BlockSpec((tm, HY_WIDTH), lambda i: (i, 0)),
                   pl.BlockSpec((tm, HY_WIDTH), lambda i: (i, 0))],
        out_shape=[jax.ShapeDtypeStruct((L, HY_WIDTH), BF16),
                   jax.ShapeDtypeStruct((L, HY_WIDTH), BF16)],
        compiler_params=_cparams("parallel"),
        name="hyprep",
    )(hyp, hyp, hyp, conv_w, conv_b)


def _filt_kernel(z_ref, w1_ref, b1_ref, f1_ref, w2_ref, b2_ref, f2_ref, w3_ref, dl_ref,
                 h_ref, nrm_ref):
    i = pl.program_id(0)
    z = z_ref[...]
    tm = z.shape[0]
    h = jnp.sin(f1_ref[...] * (jnp.dot(z, w1_ref[...], preferred_element_type=F32, precision=HIGHEST)
                               + b1_ref[...]))
    h = jnp.sin(f2_ref[...] * (jnp.dot(h, w2_ref[...], preferred_element_type=F32, precision=HIGHEST)
                               + b2_ref[...]))
    h3 = jnp.dot(h, w3_ref[...], preferred_element_type=F32, precision=HIGHEST)
    t = z[:, 0:1]
    window = jnp.exp(-t * dl_ref[...]) + FILTER_SHIFT
    hf = h3[:, 0:HY_WIDTH] * window
    hb = h3[:, HY_WIDTH:2 * HY_WIDTH] * window
    row = lax.broadcasted_iota(jnp.int32, (tm, 1), 0)
    hb = jnp.where((row == 0) & (i == 0), 0.0, hb)
    h_ref[:, 0:HY_WIDTH] = hf.astype(BF16)
    h_ref[:, HY_WIDTH:2 * HY_WIDTH] = hb.astype(BF16)
    part = jnp.sum(jnp.abs(hf) + jnp.abs(hb), axis=0, keepdims=True)

    @pl.when(i == 0)
    def _():
        nrm_ref[...] = jnp.zeros_like(nrm_ref)

    nrm_ref[...] += part


def _filt(z, w1, b1, f1, w2, b2, f2, w3, dl, tm):
    L = z.shape[0]
    full = lambda a: pl.BlockSpec(a.shape, lambda i: (0,) * a.ndim)
    return pl.pallas_call(
        _filt_kernel,
        grid=(L // tm,),
        in_specs=[pl.BlockSpec((tm, z.shape[1]), lambda i: (i, 0)), full(w1), full(b1), full(f1),
                  full(w2), full(b2), full(f2), full(w3), full(dl)],
        out_specs=[pl.BlockSpec((tm, 2 * HY_WIDTH), lambda i: (i, 0)),
                   pl.BlockSpec((1, HY_WIDTH), lambda i: (0, 0))],
        out_shape=[jax.ShapeDtypeStruct((L, 2 * HY_WIDTH), BF16),
                   jax.ShapeDtypeStruct((1, HY_WIDTH), F32)],
        compiler_params=_cparams("arbitrary"),
        name="filt",
    )(z, w1, b1, f1, w2, b2, f2, w3, dl)


def _mm_kernel(a_ref, b_ref, o_ref):
    o_ref[...] = jnp.dot(a_ref[...], b_ref[...], preferred_element_type=F32).astype(o_ref.dtype)


def _dft1(f1m, xv, tn):
    m, k = f1m.shape
    n = xv.shape[1]
    return pl.pallas_call(
        _mm_kernel,
        grid=(n // tn,),
        in_specs=[pl.BlockSpec((m, k), lambda j: (0, 0)),
                  pl.BlockSpec((k, tn), lambda j: (0, j))],
        out_specs=pl.BlockSpec((m, tn), lambda j: (0, j)),
        out_shape=jax.ShapeDtypeStruct((m, n), BF16),
        compiler_params=_cparams("parallel"),
        name="dft1",
    )(f1m, xv)


def _spec_kernel(au_ref, ah_ref, mf_ref, mi_ref, nrm_ref, z_ref, *, kb):
    w = HY_WIDTH
    n2 = au_ref.shape[2]
    inv = 1.0 / nrm_ref[...]
    for j in range(kb):
        a_u = au_ref[:, j].reshape(2 * n2, w)
        a_h = ah_ref[:, j].reshape(2 * n2, 2 * w)
        mf = mf_ref[j]
        xu = jnp.dot(mf, a_u, preferred_element_type=F32)
        xh = jnp.dot(mf, a_h, preferred_element_type=F32)
        ur, ui = xu[0:n2], xu[n2:2 * n2]
        gr = (xh[0:n2, 0:w] + xh[0:n2, w:2 * w]) * inv
        gi = (xh[n2:2 * n2, 0:w] - xh[n2:2 * n2, w:2 * w]) * inv
        yr = ur * gr - ui * gi
        yi = ur * gi + ui * gr
        y = jnp.concatenate([yr, yi], axis=0).astype(BF16)
        z = jnp.dot(mi_ref[j], y, preferred_element_type=F32)
        z_ref[0, j] = z[0:n2].astype(BF16)
        z_ref[1, j] = z[n2:2 * n2].astype(BF16)


def _spec(au, ah, mf, mi, nrm, kb):
    _, k1p, n2, w = au.shape
    return pl.pallas_call(
        functools.partial(_spec_kernel, kb=kb),
        grid=(k1p // kb,),
        in_specs=[pl.BlockSpec((2, kb, n2, w), lambda s: (0, s, 0, 0)),
                  pl.BlockSpec((2, kb, n2, 2 * w), lambda s: (0, s, 0, 0)),
                  pl.BlockSpec((kb, 2 * n2, 2 * n2), lambda s: (s, 0, 0)),
                  pl.BlockSpec((kb, 2 * n2, 2 * n2), lambda s: (s, 0, 0)),
                  pl.BlockSpec((1, w), lambda s: (0, 0))],
        out_specs=pl.BlockSpec((2, kb, n2, w), lambda s: (0, s, 0, 0)),
        out_shape=jax.ShapeDtypeStruct((2, k1p, n2, w), BF16),
        compiler_params=_cparams("parallel"),
        name="spec",
    )(au, ah, mf, mi, nrm)


def _idft1_kernel(c_ref, z_ref, u_ref, x0_ref, bias_ref, o_ref, *, inv_n):
    y = jnp.dot(c_ref[...], z_ref[...], preferred_element_type=F32) * inv_n
    u = u_ref[...].astype(F32)
    o_ref[...] = (x0_ref[...].astype(F32) * (y + bias_ref[...] * u)).astype(BF16)


def _idft1(c2, zv, uv, x0v, bias_t, tn, inv_n):
    m, k = c2.shape
    n = zv.shape[1]
    return pl.pallas_call(
        functools.partial(_idft1_kernel, inv_n=inv_n),
        grid=(n // tn,),
        in_specs=[pl.BlockSpec((m, k), lambda j: (0, 0)),
                  pl.BlockSpec((k, tn), lambda j: (0, j)),
                  pl.BlockSpec((m, tn), lambda j: (0, j)),
                  pl.BlockSpec((m, tn), lambda j: (0, j)),
                  pl.BlockSpec((1, tn), lambda j: (0, 0))],
        out_specs=pl.BlockSpec((m, tn), lambda j: (0, j)),
        out_shape=jax.ShapeDtypeStruct((m, n), BF16),
        compiler_params=_cparams("parallel"),
        name="idft1",
    )(c2, zv, uv, x0v, bias_t)


def _attn_kernel(q_ref, k_ref, v_ref, o_ref, *, tk, nc, unroll):
    tq = q_ref.shape[0]
    rc = tq // nc
    nk = k_ref.shape[0] // tk
    qs = [q_ref[c * rc:(c + 1) * rc, :] for c in range(nc)]

    def body(j, carry):
        ms, accs = carry
        off = pl.multiple_of(j * tk, tk)
        kc = k_ref[pl.ds(off, tk), :]
        vc = v_ref[pl.ds(off, tk), :]
        ss = [lax.dot_general(qs[c], kc, (((1,), (1,)), ((), ())), preferred_element_type=F32)
              for c in range(nc)]
        new_m, new_acc = [], []
        for c in range(nc):
            s = ss[c]
            m_new = jnp.maximum(ms[c], jnp.max(s, axis=1, keepdims=True))
            alpha = jnp.exp2(ms[c] - m_new)
            p = jnp.exp2(s - m_new)
            new_acc.append(alpha * accs[c] + jnp.dot(p.astype(BF16), vc, preferred_element_type=F32))
            new_m.append(m_new)
        return tuple(new_m), tuple(new_acc)

    m0 = tuple(jnp.full((rc, 1), NEG_BIG, F32) for _ in range(nc))
    acc0 = tuple(jnp.zeros((rc, HEAD_SLAB), F32) for _ in range(nc))
    _, accs = lax.fori_loop(0, nk, body, (m0, acc0), unroll=unroll)
    for c in range(nc):
        acc = accs[c]
        o_ref[c * rc:(c + 1) * rc, :] = (acc / acc[:, ONES_LANE:ONES_LANE + 1]).astype(BF16)


def _attn(q, k, v, tq, tk, nc, unroll):
    L = q.shape[0]
    return pl.pallas_call(
        functools.partial(_attn_kernel, tk=tk, nc=nc, unroll=unroll),
        grid=(N_HEADS, L // tq),
        in_specs=[pl.BlockSpec((tq, HEAD_SLAB), lambda h, i: (i, h)),
                  pl.BlockSpec((L, HEAD_SLAB), lambda h, i: (0, h)),
                  pl.BlockSpec((L, HEAD_SLAB), lambda h, i: (0, h))],
        out_specs=pl.BlockSpec((tq, HEAD_SLAB), lambda h, i: (i, h)),
        out_shape=jax.ShapeDtypeStruct((L, QK_SLABS), BF16),
        compiler_params=_cparams("parallel", "parallel"),
        name="attn",
    )(q, k, v)


def _merge_kernel(x_ref, yh_ref, ym_ref, gate_ref, mod_ref, why_ref, wmla_ref, wmix_ref, g2_ref,
                  wrt_ref, brt_ref, x1_ref, h2_ref, comb_ref):
    d = D_MODEL
    a = jnp.dot(yh_ref[...], why_ref[...], preferred_element_type=F32)
    b = jnp.dot(ym_ref[...], wmla_ref[...], preferred_element_type=F32)
    mix = gate_ref[:, 0:d].astype(F32) * a + gate_ref[:, d:2 * d].astype(F32) * b
    o = jnp.dot(mix.astype(BF16), wmix_ref[...], preferred_element_type=F32)
    gt1 = mod_ref[0:1, 2 * d:3 * d]
    sh2 = mod_ref[0:1, 3 * d:4 * d]
    sc2 = mod_ref[0:1, 4 * d:5 * d]
    x1 = x_ref[...] + gt1 * o
    x1_ref[...] = x1
    h2 = _rms(x1) * g2_ref[...] * (1.0 + sc2) + sh2
    h2_ref[...] = h2.astype(BF16)
    lg = jnp.dot(h2, wrt_ref[...], preferred_element_type=F32, precision=HIGHEST) + brt_ref[...]
    e = lg[:, 0:LANES]
    gl = lg[:, LANES:2 * LANES]
    lane = lax.broadcasted_iota(jnp.int32, e.shape, 1).astype(F32)
    gvalid = lane < N_GROUPS
    glm = jnp.where(gvalid, gl, NEG_BIG)
    gmax = jnp.max(glm, axis=1, keepdims=True)
    gidx = jnp.min(jnp.where(glm == gmax, lane, float(LANES)), axis=1, keepdims=True)
    psum = jnp.sum(jnp.where(gvalid, jnp.exp(glm - gmax), 0.0), axis=1, keepdims=True)
    p_sel = 1.0 / psum
    lo = gidx * EXPERTS_PER_GROUP
    em = jnp.where((lane >= lo) & (lane < lo + EXPERTS_PER_GROUP), e, NEG_BIG)
    v1 = jnp.max(em, axis=1, keepdims=True)
    i1 = jnp.min(jnp.where(em == v1, lane, float(LANES)), axis=1, keepdims=True)
    em2 = jnp.where(lane == i1, NEG_BIG, em)
    v2 = jnp.max(em2, axis=1, keepdims=True)
    i2 = jnp.min(jnp.where(em2 == v2, lane, float(LANES)), axis=1, keepdims=True)
    t = jnp.exp(v2 - v1)
    w1 = p_sel / (1.0 + t)
    w2 = p_sel * t / (1.0 + t)
    comb_ref[...] = jnp.where(lane == i1, w1, 0.0) + jnp.where(lane == i2, w2, 0.0)


def _merge(x2, yh, ym, gates, mod, why, wmla, wmix, g2, wrt, brt, tm):
    L, d = x2.shape
    full = lambda a: pl.BlockSpec(a.shape, lambda i: (0,) * a.ndim)
    row = lambda w: pl.BlockSpec((tm, w), lambda i: (i, 0))
    return pl.pallas_call(
        _merge_kernel,
        grid=(L // tm,),
        in_specs=[row(d), row(HY_WIDTH), row(QK_SLABS), row(2 * d), full(mod), full(why), full(wmla),
                  full(wmix), full(g2), full(wrt), full(brt)],
        out_specs=[row(d), row(d), row(LANES)],
        out_shape=[jax.ShapeDtypeStruct((L, d), F32),
                   jax.ShapeDtypeStruct((L, d), BF16),
                   jax.ShapeDtypeStruct((L, LANES), F32)],
        compiler_params=_cparams("parallel"),
        name="merge",
    )(x2, yh, ym, gates, mod, why, wmla, wmix, g2, wrt, brt)


def _moe_kernel(h_ref, comb_ref, wgu_ref, wd_ref, x1_ref, mod_ref, fg_ref, o_ref, acc_ref):
    d = D_MODEL
    e = pl.program_id(1)

    @pl.when(e == 0)
    def _():
        acc_ref[...] = jnp.zeros_like(acc_ref)

    h = h_ref[...]
    gu = jnp.dot(h, wgu_ref[0], preferred_element_type=F32)
    g = gu[:, 0:D_EXPERT]
    a = g * jax.nn.sigmoid(g) * gu[:, D_EXPERT:2 * D_EXPERT]
    comb = comb_ref[...]
    lane = lax.broadcasted_iota(jnp.int32, comb.shape, 1)
    col = jnp.sum(jnp.where(lane == e, comb, 0.0), axis=1, keepdims=True)
    acc_ref[...] += jnp.dot((a * col).astype(BF16), wd_ref[0], preferred_element_type=F32)

    @pl.when(e == pl.num_programs(1) - 1)
    def _():
        gt2 = mod_ref[0:1, 5 * d:6 * d]
        x2 = x1_ref[...] + gt2 * acc_ref[...]
        o_ref[...] = _rms(x2) * fg_ref[...]


def _moe(h2, comb, wgu, wd, x1, mod, fg, tm):
    L, d = x1.shape
    ne = wgu.shape[0]
    return pl.pallas_call(
        _moe_kernel,
        grid=(L // tm, ne),
        in_specs=[pl.BlockSpec((tm, d), lambda i, e: (i, 0)),
                  pl.BlockSpec((tm, LANES), lambda i, e: (i, 0)),
                  pl.BlockSpec((1, d, 2 * D_EXPERT), lambda i, e: (e, 0, 0)),
                  pl.BlockSpec((1, D_EXPERT, d), lambda i, e: (e, 0, 0)),
                  pl.BlockSpec((tm, d), lambda i, e: (i, 0)),
                  pl.BlockSpec(mod.shape, lambda i, e: (0, 0)),
                  pl.BlockSpec(fg.shape, lambda i, e: (0, 0))],
        out_specs=pl.BlockSpec((tm, d), lambda i, e: (i, 0)),
        out_shape=jax.ShapeDtypeStruct((L, d), F32),
        scratch_shapes=[pltpu.VMEM((tm, d), F32)],
        compiler_params=_cparams("parallel", "arbitrary"),
        name="moe",
    )(h2, comb, wgu, wd, x1, mod, fg)


def _pad_heads(w, width):
    lead = w.shape[:-1]
    w = w.reshape(lead + (N_HEADS, width))
    w = jnp.pad(w, [(0, 0)] * len(lead) + [(0, 0), (0, HEAD_SLAB - width)])
    return w.reshape(lead + (N_HEADS * HEAD_SLAB,))


def _rope_slab_tables(L):
    half = QK_ROPE // 2
    pos = jnp.arange(L, dtype=F32)
    inv = ROPE_THETA ** (-jnp.arange(0, QK_ROPE, 2, dtype=F32) / QK_ROPE)
    ang = pos[:, None] * inv[None, :]
    cos, sin = jnp.cos(ang), jnp.sin(ang)
    z = lambda w: jnp.zeros((L, w), F32)
    cos_t = jnp.concatenate([jnp.ones((L, QK_NOPE), F32), cos, cos, z(HEAD_SLAB - QK_NOPE - QK_ROPE)], axis=1)
    sin_a = jnp.concatenate([z(QK_NOPE), -sin, z(HEAD_SLAB - QK_NOPE - half)], axis=1)
    sin_b = jnp.concatenate([z(QK_NOPE + half), sin, z(HEAD_SLAB - QK_NOPE - QK_ROPE)], axis=1)
    return cos_t, sin_a, sin_b


def _filter_features(L):
    t = jnp.linspace(0.0, 1.0, L, dtype=F32)[:, None]
    t_r = jnp.arange(L, dtype=F32)[:, None]
    bands = jnp.linspace(1e-4, FILTER_BANDS - 1, FILTER_BANDS, dtype=F32)[None, :]
    ang = 2.0 * math.pi * bands * t_r / L
    z = jnp.concatenate([t, jnp.cos(ang), -jnp.sin(ang)], axis=-1)
    return jnp.pad(z, ((0, 0), (0, FILTER_HIDDEN - FILTER_EMB)))


def _dft_tables(L):
    n = 2 * L
    n1 = DFT_N1
    n2 = n // n1
    two_pi = 2.0 * math.pi
    k1 = jnp.arange(DFT_K1P, dtype=jnp.int32)[:, None]
    m1 = jnp.arange(n1 // 2, dtype=jnp.int32)[None, :]
    ang1 = two_pi * ((k1 * m1) % n1).astype(F32) / n1
    live = (k1 < DFT_K1).astype(F32)
    f1m = jnp.concatenate([jnp.cos(ang1) * live, -jnp.sin(ang1) * live], axis=0).astype(BF16)
    kk1 = jnp.arange(DFT_K1P, dtype=jnp.int32)[:, None, None]
    k2 = jnp.arange(n2, dtype=jnp.int32)[None, :, None]
    m2 = jnp.arange(n2, dtype=jnp.int32)[None, None, :]
    ang2 = two_pi * ((m2 * (n1 * k2 + kk1)) % n).astype(F32) / n
    live2 = (kk1 < DFT_K1).astype(F32)
    c2, s2 = jnp.cos(ang2) * live2, jnp.sin(ang2) * live2
    mr, mi = c2, -s2
    mf = jnp.concatenate([jnp.concatenate([mr, -mi], axis=2),
                          jnp.concatenate([mi, mr], axis=2)], axis=1).astype(BF16)
    pr, pi_ = jnp.swapaxes(c2, 1, 2), jnp.swapaxes(s2, 1, 2)
    minv = jnp.concatenate([jnp.concatenate([pr, -pi_], axis=2),
                            jnp.concatenate([pi_, pr], axis=2)], axis=1).astype(BF16)
    o1 = jnp.arange(n1 // 2, dtype=jnp.int32)[:, None]
    q1 = jnp.arange(DFT_K1P, dtype=jnp.int32)[None, :]
    ang3 = two_pi * ((o1 * q1) % n1).astype(F32) / n1
    wgt = jnp.where((q1 == 0) | (q1 == n1 // 2), 1.0, 2.0) * (q1 < DFT_K1).astype(F32)
    cmat = jnp.concatenate([wgt * jnp.cos(ang3), -wgt * jnp.sin(ang3)], axis=1).astype(BF16)
    return f1m, mf, minv, cmat, n2


def _pick(n, pref):
    t = pref
    while n % t:
        t //= 2
    return t


def kernel(x, c, ada_w, ada_b, norm_mix_g, w_in, hy_conv_w, hy_conv_b, filt_w1, filt_b1, filt_freq1, filt_w2, filt_b2, filt_freq2, filt_w3, hy_bias, q_norm_g, kv_norm_g, w_uq, w_uk, w_uv, w_hy_out, w_mla_out, w_mix_out, norm_ffn_g, w_group, b_group, w_router, b_router, w_gate, w_up, w_down, final_norm_g):
    B, L, d = x.shape
    assert B == 1 and d == D_MODEL and ada_w.shape[0] == 1
    x2 = x.reshape(L, d)
    row = lambda v: v.reshape(1, -1)

    mod = _ada(jnp.pad(c, ((0, 8 - B), (0, 0))), ada_w[0], row(ada_b[0]))

    wi = w_in[0]
    kr_slab = jnp.pad(wi[:, OFF_KR:OFF_GATE], ((0, 0), (QK_NOPE, HEAD_SLAB - QK_NOPE - QK_ROPE)))
    wcat = jnp.concatenate([wi[:, 0:OFF_Q], wi[:, OFF_Q:OFF_KV], wi[:, OFF_KV:OFF_KR],
                            wi[:, OFF_GATE:], kr_slab], axis=1).astype(BF16)
    wuq = _pad_heads(w_uq[0], QK_NOPE + QK_ROPE).astype(BF16)
    wuk = _pad_heads(w_uk[0].reshape(KV_LORA, N_HEADS * QK_NOPE), QK_NOPE).astype(BF16)
    wuv = _pad_heads(w_uv[0].reshape(KV_LORA, N_HEADS * V_HEAD), V_HEAD).astype(BF16)
    cos_t, sin_a, sin_b = _rope_slab_tables(L)

    tm = _pick(L, 256)
    hyp, gates, q, k, v = _inproj(x2, mod, row(norm_mix_g[0]), wcat, row(q_norm_g[0]), row(kv_norm_g[0]),
                                  wuq, wuk, wuv, cos_t, sin_a, sin_b, tm)

    u, x0 = _hyprep(hyp, hy_conv_w[0], row(hy_conv_b[0]), _pick(L, 512))
    zf = _filter_features(L)
    w1p = jnp.pad(filt_w1[0], ((0, FILTER_HIDDEN - FILTER_EMB), (0, 0)))
    dl = jnp.abs(jnp.linspace(MIN_DECAY, MAX_DECAY, HY_WIDTH, dtype=F32))[None, :]
    hfilt, nrm = _filt(zf, w1p, row(filt_b1[0]), row(filt_freq1[0]), filt_w2[0], row(filt_b2[0]),
                       row(filt_freq2[0]), filt_w3[0], dl, _pick(L, 512))
    f1m, mf, minv, cmat, n2 = _dft_tables(L)
    half = DFT_N1 // 2
    au = _dft1(f1m, u.reshape(half, n2 * HY_WIDTH), _pick(n2 * HY_WIDTH, 4096))
    ah = _dft1(f1m, hfilt.reshape(half, n2 * 2 * HY_WIDTH), _pick(n2 * 2 * HY_WIDTH, 4096))
    au = au.reshape(2, DFT_K1P, n2, HY_WIDTH)
    ah = ah.reshape(2, DFT_K1P, n2, 2 * HY_WIDTH)
    zs = _spec(au, ah, mf, minv, nrm, 3)
    tn = _pick(n2 * HY_WIDTH, 4096)
    bias_t = jnp.tile(row(hy_bias[0]), (1, tn // HY_WIDTH))
    yh = _idft1(cmat, zs.reshape(2 * DFT_K1P, n2 * HY_WIDTH), u.reshape(half, n2 * HY_WIDTH),
                x0.reshape(half, n2 * HY_WIDTH), bias_t, tn, 1.0 / (2 * L))
    yh = yh.reshape(L, HY_WIDTH)

    ym = _attn(q, k, v, _pick(L, 1024), _pick(L, 1024), 1, 2)

    wmla = jnp.pad(w_mla_out[0].reshape(N_HEADS, V_HEAD, d),
                   ((0, 0), (0, HEAD_SLAB - V_HEAD), (0, 0))).reshape(QK_SLABS, d).astype(BF16)
    wrt = jnp.concatenate([jnp.pad(w_router[0], ((0, 0), (0, LANES - N_EXPERTS))),
                           jnp.pad(w_group[0], ((0, 0), (0, LANES - N_GROUPS)))], axis=1)
    brt = jnp.concatenate([jnp.pad(b_router[0], (0, LANES - N_EXPERTS)),
                           jnp.pad(b_group[0], (0, LANES - N_GROUPS))])[None, :]
    x1, h2, comb = _merge(x2, yh, ym, gates, mod, w_hy_out[0].astype(BF16), wmla,
                          w_mix_out[0].astype(BF16), row(norm_ffn_g[0]), wrt, brt, _pick(L, 256))

    wgu = jnp.concatenate([w_gate[0], w_up[0]], axis=2).astype(BF16)
    out = _moe(h2, comb, wgu, w_down[0].astype(BF16), x1, mod, row(final_norm_g), _pick(L, 1024))
    return out.reshape(B, L, d)
```

```python
import functools
import math

import jax
import jax.numpy as jnp
from jax import lax
from jax.experimental import pallas as pl
from jax.experimental.pallas import tpu as pltpu

F32 = jnp.float32
BF16 = jnp.bfloat16
HIGHEST = lax.Precision.HIGHEST

D_MODEL = 1024
EPS = 1e-6
HY_WIDTH = 512
FILTER_BANDS = 16
FILTER_EMB = 1 + 2 * FILTER_BANDS
FILTER_HIDDEN = 64
FILTER_DECAY_TARGET = 1e-2
FAST_DECAY_PCT = 0.3
SLOW_DECAY_PCT = 1.5
FILTER_SHIFT = 0.05
MIN_DECAY = math.log(FILTER_DECAY_TARGET) / SLOW_DECAY_PCT
MAX_DECAY = math.log(FILTER_DECAY_TARGET) / FAST_DECAY_PCT
N_HEADS = 8
QK_NOPE = 64
QK_ROPE = 32
V_HEAD = 64
Q_LORA = 384
KV_LORA = 256
ROPE_THETA = 10000.0
MLA_WIDTH = N_HEADS * V_HEAD
OFF_Q = 3 * HY_WIDTH
OFF_KV = OFF_Q + Q_LORA
OFF_KR = OFF_KV + KV_LORA
OFF_GATE = OFF_KR + QK_ROPE
N_GROUPS = 4
EXPERTS_PER_GROUP = 8
N_EXPERTS = N_GROUPS * EXPERTS_PER_GROUP
D_EXPERT = 256

LANES = 128
HEAD_SLAB = LANES
QK_SLABS = N_HEADS * HEAD_SLAB
ONES_LANE = V_HEAD
DFT_N1 = 256
DFT_K1 = DFT_N1 // 2 + 1
DFT_K1P = 144
NEG_BIG = -1e30
VMEM_LIMIT = 56 * 1024 * 1024

C_HY = 0
C_Q = C_HY + 3 * HY_WIDTH
C_KV = C_Q + Q_LORA
C_GATE = C_KV + KV_LORA
C_KR = C_GATE + 2 * D_MODEL
C_END = C_KR + HEAD_SLAB


def _cparams(*sem):
    return pltpu.CompilerParams(dimension_semantics=sem, vmem_limit_bytes=VMEM_LIMIT)


def _rms(x):
    return x * lax.rsqrt(jnp.mean(x * x, axis=-1, keepdims=True) + EPS)


def _ada_kernel(c_ref, w_ref, b_ref, o_ref):
    c = c_ref[...]
    ca = c * jax.nn.sigmoid(c)
    o_ref[...] = jnp.dot(ca, w_ref[...], preferred_element_type=F32, precision=HIGHEST) + b_ref[...]


def _ada(c8, w, b):
    d, n = w.shape
    tn = 1024
    return pl.pallas_call(
        _ada_kernel,
        grid=(n // tn,),
        in_specs=[pl.BlockSpec((8, d), lambda j: (0, 0)),
                  pl.BlockSpec((d, tn), lambda j: (0, j)),
                  pl.BlockSpec((1, tn), lambda j: (0, j))],
        out_specs=pl.BlockSpec((8, tn), lambda j: (0, j)),
        out_shape=jax.ShapeDtypeStruct((8, n), F32),
        compiler_params=_cparams("parallel"),
        name="ada",
    )(c8, w, b)


def _rope_slab(z, cos_t, sin_a, sin_b):
    return z * cos_t + pltpu.roll(z, HEAD_SLAB - QK_ROPE // 2, 1) * sin_a + pltpu.roll(z, QK_ROPE // 2, 1) * sin_b


def _inproj_kernel(x_ref, mod_ref, g_ref, wcat_ref, gq_ref, gkv_ref, wuq_ref, wuk_ref, wuv_ref,
                   cos_ref, sina_ref, sinb_ref,
                   hyp_ref, gate_ref, q_ref, k_ref, v_ref, *, qscale):
    d = D_MODEL
    x = x_ref[...]
    sh1 = mod_ref[0:1, 0:d]
    sc1 = mod_ref[0:1, d:2 * d]
    h = _rms(x) * g_ref[...] * (1.0 + sc1) + sh1
    proj = jnp.dot(h.astype(BF16), wcat_ref[...], preferred_element_type=F32)
    hyp_ref[...] = proj[:, C_HY:C_Q]
    gate_ref[...] = jax.nn.sigmoid(proj[:, C_GATE:C_KR]).astype(BF16)
    cq = _rms(proj[:, C_Q:C_KV]) * gq_ref[...]
    ckv = (_rms(proj[:, C_KV:C_GATE]) * gkv_ref[...]).astype(BF16)
    q = jnp.dot(cq.astype(BF16), wuq_ref[...], preferred_element_type=F32)
    kn = jnp.dot(ckv, wuk_ref[...], preferred_element_type=F32)
    vv = jnp.dot(ckv, wuv_ref[...], preferred_element_type=F32)
    cos_t = cos_ref[...]
    sin_a = sina_ref[...]
    sin_b = sinb_ref[...]
    kr = _rope_slab(proj[:, C_KR:C_END], cos_t, sin_a, sin_b)
    lane = lax.broadcasted_iota(jnp.int32, (1, HEAD_SLAB), 1)
    ones_col = jnp.where(lane == ONES_LANE, 1.0, 0.0).astype(F32)
    for hd in range(N_HEADS):
        sl = slice(hd * HEAD_SLAB, (hd + 1) * HEAD_SLAB)
        q_ref[:, sl] = (_rope_slab(q[:, sl], cos_t, sin_a, sin_b) * qscale).astype(BF16)
        k_ref[:, sl] = (kn[:, sl] + kr).astype(BF16)
        v_ref[:, sl] = (vv[:, sl] + ones_col).astype(BF16)


def _inproj(x2, mod, g1, wcat, gq, gkv, wuq, wuk, wuv, cos_t, sin_a, sin_b, tm):
    L, d = x2.shape
    qscale = (QK_NOPE + QK_ROPE) ** -0.5 * math.log2(math.e)
    full = lambda a: pl.BlockSpec(a.shape, lambda i: (0,) * a.ndim)
    row = lambda w: pl.BlockSpec((tm, w), lambda i: (i, 0))
    return pl.pallas_call(
        functools.partial(_inproj_kernel, qscale=qscale),
        grid=(L // tm,),
        in_specs=[row(d), full(mod), full(g1), full(wcat), full(gq), full(gkv), full(wuq), full(wuk),
                  full(wuv), row(HEAD_SLAB), row(HEAD_SLAB), row(HEAD_SLAB)],
        out_specs=[row(3 * HY_WIDTH), row(2 * D_MODEL), row(QK_SLABS), row(QK_SLABS), row(QK_SLABS)],
        out_shape=[jax.ShapeDtypeStruct((L, 3 * HY_WIDTH), F32),
                   jax.ShapeDtypeStruct((L, 2 * D_MODEL), BF16),
                   jax.ShapeDtypeStruct((L, QK_SLABS), BF16),
                   jax.ShapeDtypeStruct((L, QK_SLABS), BF16),
                   jax.ShapeDtypeStruct((L, QK_SLABS), BF16)],
        compiler_params=_cparams("parallel"),
        name="inproj",
    )(x2, mod, g1, wcat, gq, gkv, wuq, wuk, wuv, cos_t, sin_a, sin_b)


def _hyprep_kernel(p_ref, prev_ref, next_ref, w_ref, b_ref, u_ref, x0_ref):
    i = pl.program_id(0)
    n = pl.num_programs(0)
    p = p_ref[...]
    tm = p.shape[0]
    row = lax.broadcasted_iota(jnp.int32, (tm, 1), 0)
    prev_row = jnp.where(i == 0, 0.0, prev_ref[7:8, :])
    next_row = jnp.where(i == n - 1, 0.0, next_ref[0:1, :])
    up = jnp.where(row == 0, prev_row, pltpu.roll(p, 1, 0))
    dn = jnp.where(row == tm - 1, next_row, pltpu.roll(p, tm - 1, 0))
    hy = up * w_ref[0:1, :] + p * w_ref[1:2, :] + dn * w_ref[2:3, :] + b_ref[...]
    w = HY_WIDTH
    x0_ref[...] = hy[:, 0:w].astype(BF16)
    u_ref[...] = (hy[:, 2 * w:3 * w] * hy[:, w:2 * w]).astype(BF16)


def _hyprep(hyp, conv_w, conv_b, tm):
    L, c = hyp.shape
    nb8 = L // 8
    r8 = tm // 8
    return pl.pallas_call(
        _hyprep_kernel,
        grid=(L // tm,),
        in_specs=[pl.BlockSpec((tm, c), lambda i: (i, 0)),
                  pl.BlockSpec((8, c), lambda i: (jnp.maximum(i * r8 - 1, 0), 0)),
                  pl.BlockSpec((8, c), lambda i: (jnp.minimum((i + 1) * r8, nb8 - 1), 0)),
                  pl.BlockSpec(conv_w.shape, lambda i: (0, 0)),
                  pl.BlockSpec(conv_b.shape, lambda i: (0, 0))],
        out_specs=[pl.BlockSpec((tm, HY_WIDTH), lambda i: (i, 0)),
                   pl.BlockSpec((tm, HY_WIDTH), lambda i: (i, 0))],
        out_shape=[jax.ShapeDtypeStruct((L, HY_WIDTH), BF16),
                   jax.ShapeDtypeStruct((L, HY_WIDTH), BF16)],
        compiler_params=_cparams("parallel"),
        name="hyprep",
    )(hyp, hyp, hyp, conv_w, conv_b)


def _filt_kernel(z_ref, w1_ref, b1_ref, f1_ref, w2_ref, b2_ref, f2_ref, w3_ref, dl_ref,
                 h_ref, nrm_ref):
    i = pl.program_id(0)
    z = z_ref[...]
    tm = z.shape[0]
    h = jnp.sin(f1_ref[...] * (jnp.dot(z, w1_ref[...], preferred_element_type=F32, precision=HIGHEST)
                               + b1_ref[...]))
    h = jnp.sin(f2_ref[...] * (jnp.dot(h, w2_ref[...], preferred_element_type=F32, precision=HIGHEST)
                               + b2_ref[...]))
    h3 = jnp.dot(h, w3_ref[...], preferred_element_type=F32, precision=HIGHEST)
    t = z[:, 0:1]
    window = jnp.exp(-t * dl_ref[...]) + FILTER_SHIFT
    hf = h3[:, 0:HY_WIDTH] * window
    hb = h3[:, HY_WIDTH:2 * HY_WIDTH] * window
    row = lax.broadcasted_iota(jnp.int32, (tm, 1), 0)
    hb = jnp.where((row == 0) & (i == 0), 0.0, hb)
    h_ref[:, 0:HY_WIDTH] = hf.astype(BF16)
    h_ref[:, HY_WIDTH:2 * HY_WIDTH] = hb.astype(BF16)
    part = jnp.sum(jnp.abs(hf) + jnp.abs(hb), axis=0, keepdims=True)

    @pl.when(i == 0)
    def _():
        nrm_ref[...] = jnp.zeros_like(nrm_ref)

    nrm_ref[...] += part


def _filt(z, w1, b1, f1, w2, b2, f2, w3, dl, tm):
    L = z.shape[0]
    full = lambda a: pl.BlockSpec(a.shape, lambda i: (0,) * a.ndim)
    return pl.pallas_call(
        _filt_kernel,
        grid=(L // tm,),
        in_specs=[pl.BlockSpec((tm, z.shape[1]), lambda i: (i, 0)), full(w1), full(b1), full(f1),
                  full(w2), full(b2), full(f2), full(w3), full(dl)],
        out_specs=[pl.BlockSpec((tm, 2 * HY_WIDTH), lambda i: (i, 0)),
                   pl.BlockSpec((1, HY_WIDTH), lambda i: (0, 0))],
        out_shape=[jax.ShapeDtypeStruct((L, 2 * HY_WIDTH), BF16),
                   jax.ShapeDtypeStruct((1, HY_WIDTH), F32)],
        compiler_params=_cparams("arbitrary"),
        name="filt",
    )(z, w1, b1, f1, w2, b2, f2, w3, dl)


def _mm_kernel(a_ref, b_ref, o_ref):
    o_ref[...] = jnp.dot(a_ref[...], b_ref[...], preferred_element_type=F32).astype(o_ref.dtype)


def _dft1(f1m, xv, tn):
    m, k = f1m.shape
    n = xv.shape[1]
    return pl.pallas_call(
        _mm_kernel,
        grid=(n // tn,),
        in_specs=[pl.BlockSpec((m, k), lambda j: (0, 0)),
                  pl.BlockSpec((k, tn), lambda j: (0, j))],
        out_specs=pl.BlockSpec((m, tn), lambda j: (0, j)),
        out_shape=jax.ShapeDtypeStruct((m, n), BF16),
        compiler_params=_cparams("parallel"),
        name="dft1",
    )(f1m, xv)


def _spec_kernel(au_ref, ah_ref, mf_ref, mi_ref, nrm_ref, z_ref, *, kb):
    w = HY_WIDTH
    n2 = au_ref.shape[2]
    inv = 1.0 / nrm_ref[...]
    for j in range(kb):
        a_u = au_ref[:, j].reshape(2 * n2, w)
        a_h = ah_ref[:, j].reshape(2 * n2, 2 * w)
        mf = mf_ref[j]
        xu = jnp.dot(mf, a_u, preferred_element_type=F32)
        xh = jnp.dot(mf, a_h, preferred_element_type=F32)
        ur, ui = xu[0:n2], xu[n2:2 * n2]
        gr = (xh[0:n2, 0:w] + xh[0:n2, w:2 * w]) * inv
        gi = (xh[n2:2 * n2, 0:w] - xh[n2:2 * n2, w:2 * w]) * inv
        yr = ur * gr - ui * gi
        yi = ur * gi + ui * gr
        y = jnp.concatenate([yr, yi], axis=0).astype(BF16)
        z = jnp.dot(mi_ref[j], y, preferred_element_type=F32)
        z_ref[0, j] = z[0:n2].astype(BF16)
        z_ref[1, j] = z[n2:2 * n2].astype(BF16)


def _spec(au, ah, mf, mi, nrm, kb):
    _, k1p, n2, w = au.shape
    return pl.pallas_call(
        functools.partial(_spec_kernel, kb=kb),
        grid=(k1p // kb,),
        in_specs=[pl.BlockSpec((2, kb, n2, w), lambda s: (0, s, 0, 0)),
                  pl.BlockSpec((2, kb, n2, 2 * w), lambda s: (0, s, 0, 0)),
                  pl.BlockSpec((kb, 2 * n2, 2 * n2), lambda s: (s, 0, 0)),
                  pl.BlockSpec((kb, 2 * n2, 2 * n2), lambda s: (s, 0, 0)),
                  pl.BlockSpec((1, w), lambda s: (0, 0))],
        out_specs=pl.BlockSpec((2, kb, n2, w), lambda s: (0, s, 0, 0)),
        out_shape=jax.ShapeDtypeStruct((2, k1p, n2, w), BF16),
        compiler_params=_cparams("parallel"),
        name="spec",
    )(au, ah, mf, mi, nrm)


def _idft1_kernel(c_ref, z_ref, u_ref, x0_ref, bias_ref, o_ref, *, inv_n):
    y = jnp.dot(c_ref[...], z_ref[...], preferred_element_type=F32) * inv_n
    u = u_ref[...].astype(F32)
    o_ref[...] = (x0_ref[...].astype(F32) * (y + bias_ref[...] * u)).astype(BF16)


def _idft1(c2, zv, uv, x0v, bias_t, tn, inv_n):
    m, k = c2.shape
    n = zv.shape[1]
    return pl.pallas_call(
        functools.partial(_idft1_kernel, inv_n=inv_n),
        grid=(n // tn,),
        in_specs=[pl.BlockSpec((m, k), lambda j: (0, 0)),
                  pl.BlockSpec((k, tn), lambda j: (0, j)),
                  pl.BlockSpec((m, tn), lambda j: (0, j)),
                  pl.BlockSpec((m, tn), lambda j: (0, j)),
                  pl.BlockSpec((1, tn), lambda j: (0, 0))],
        out_specs=pl.BlockSpec((m, tn), lambda j: (0, j)),
        out_shape=jax.ShapeDtypeStruct((m, n), BF16),
        compiler_params=_cparams("parallel"),
        name="idft1",
    )(c2, zv, uv, x0v, bias_t)


def _attn_kernel(q_ref, k_ref, v_ref, o_ref, *, tk, nc, unroll):
    tq = q_ref.shape[0]
    rc = tq // nc
    nk = k_ref.shape[0] // tk
    qs = [q_ref[c * rc:(c + 1) * rc, :] for c in range(nc)]

    def body(j, carry):
        ms, accs = carry
        off = pl.multiple_of(j * tk, tk)
        kc = k_ref[pl.ds(off, tk), :]
        vc = v_ref[pl.ds(off, tk), :]
        ss = [lax.dot_general(qs[c], kc, (((1,), (1,)), ((), ())), preferred_element_type=F32)
              for c in range(nc)]
        new_m, new_acc = [], []
        for c in range(nc):
            s = ss[c]
            m_new = jnp.maximum(ms[c], jnp.max(s, axis=1, keepdims=True))
            alpha = jnp.exp2(ms[c] - m_new)
            p = jnp.exp2(s - m_new)
            new_acc.append(alpha * accs[c] + jnp.dot(p.astype(BF16), vc, preferred_element_type=F32))
            new_m.append(m_new)
        return tuple(new_m), tuple(new_acc)

    m0 = tuple(jnp.full((rc, 1), NEG_BIG, F32) for _ in range(nc))
    acc0 = tuple(jnp.zeros((rc, HEAD_SLAB), F32) for _ in range(nc))
    _, accs = lax.fori_loop(0, nk, body, (m0, acc0), unroll=unroll)
    for c in range(nc):
        acc = accs[c]
        o_ref[c * rc:(c + 1) * rc, :] = (acc / acc[:, ONES_LANE:ONES_LANE + 1]).astype(BF16)


def _attn(q, k, v, tq, tk, nc, unroll):
    L = q.shape[0]
    return pl.pallas_call(
        functools.partial(_attn_kernel, tk=tk, nc=nc, unroll=unroll),
        grid=(N_HEADS, L // tq),
        in_specs=[pl.BlockSpec((tq, HEAD_SLAB), lambda h, i: (i, h)),
                  pl.BlockSpec((L, HEAD_SLAB), lambda h, i: (0, h)),
                  pl.BlockSpec((L, HEAD_SLAB), lambda h, i: (0, h))],
        out_specs=pl.BlockSpec((tq, HEAD_SLAB), lambda h, i: (i, h)),
        out_shape=jax.ShapeDtypeStruct((L, QK_SLABS), BF16),
        compiler_params=_cparams("parallel", "parallel"),
        name="attn",
    )(q, k, v)


def _merge_kernel(x_ref, yh_ref, ym_ref, gate_ref, mod_ref, why_ref, wmla_ref, wmix_ref, g2_ref,
                  wrt_ref, brt_ref, x1_ref, h2_ref, comb_ref):
    d = D_MODEL
    a = jnp.dot(yh_ref[...], why_ref[...], preferred_element_type=F32)
    b = jnp.dot(ym_ref[...], wmla_ref[...], preferred_element_type=F32)
    mix = gate_ref[:, 0:d].astype(F32) * a + gate_ref[:, d:2 * d].astype(F32) * b
    o = jnp.dot(mix.astype(BF16), wmix_ref[...], preferred_element_type=F32)
    gt1 = mod_ref[0:1, 2 * d:3 * d]
    sh2 = mod_ref[0:1, 3 * d:4 * d]
    sc2 = mod_ref[0:1, 4 * d:5 * d]
    x1 = x_ref[...] + gt1 * o
    x1_ref[...] = x1
    h2 = _rms(x1) * g2_ref[...] * (1.0 + sc2) + sh2
    h2_ref[...] = h2.astype(BF16)
    lg = jnp.dot(h2, wrt_ref[...], preferred_element_type=F32, precision=HIGHEST) + brt_ref[...]
    e = lg[:, 0:LANES]
    gl = lg[:, LANES:2 * LANES]
    lane = lax.broadcasted_iota(jnp.int32, e.shape, 1).astype(F32)
    gvalid = lane < N_GROUPS
    glm = jnp.where(gvalid, gl, NEG_BIG)
    gmax = jnp.max(glm, axis=1, keepdims=True)
    gidx = jnp.min(jnp.where(glm == gmax, lane, float(LANES)), axis=1, keepdims=True)
    psum = jnp.sum(jnp.where(gvalid, jnp.exp(glm - gmax), 0.0), axis=1, keepdims=True)
    p_sel = 1.0 / psum
    lo = gidx * EXPERTS_PER_GROUP
    em = jnp.where((lane >= lo) & (lane < lo + EXPERTS_PER_GROUP), e, NEG_BIG)
    v1 = jnp.max(em, axis=1, keepdims=True)
    i1 = jnp.min(jnp.where(em == v1, lane, float(LANES)), axis=1, keepdims=True)
    em2 = jnp.where(lane == i1, NEG_BIG, em)
    v2 = jnp.max(em2, axis=1, keepdims=True)
    i2 = jnp.min(jnp.where(em2 == v2, lane, float(LANES)), axis=1, keepdims=True)
    t = jnp.exp(v2 - v1)
    w1 = p_sel / (1.0 + t)
    w2 = p_sel * t / (1.0 + t)
    comb_ref[...] = jnp.where(lane == i1, w1, 0.0) + jnp.where(lane == i2, w2, 0.0)


def _merge(x2, yh, ym, gates, mod, why, wmla, wmix, g2, wrt, brt, tm):
    L, d = x2.shape
    full = lambda a: pl.BlockSpec(a.shape, lambda i: (0,) * a.ndim)
    row = lambda w: pl.BlockSpec((tm, w), lambda i: (i, 0))
    return pl.pallas_call(
        _merge_kernel,
        grid=(L // tm,),
        in_specs=[row(d), row(HY_WIDTH), row(QK_SLABS), row(2 * d), full(mod), full(why), full(wmla),
                  full(wmix), full(g2), full(wrt), full(brt)],
        out_specs=[row(d), row(d), row(LANES)],
        out_shape=[jax.ShapeDtypeStruct((L, d), F32),
                   jax.ShapeDtypeStruct((L, d), BF16),
                   jax.ShapeDtypeStruct((L, LANES), F32)],
        compiler_params=_cparams("parallel"),
        name="merge",
    )(x2, yh, ym, gates, mod, why, wmla, wmix, g2, wrt, brt)


def _moe_kernel(h_ref, comb_ref, wgu_ref, wd_ref, x1_ref, mod_ref, fg_ref, o_ref, acc_ref):
    d = D_MODEL
    e = pl.program_id(1)

    @pl.when(e == 0)
    def _():
        acc_ref[...] = jnp.zeros_like(acc_ref)

    h = h_ref[...]
    comb = comb_ref[...]
    lane = lax.broadcasted_iota(jnp.int32, comb.shape, 1)
    ge = wgu_ref.shape[0]
    parts = []
    for j in range(ge):
        gu = jnp.dot(h, wgu_ref[j], preferred_element_type=F32)
        g = gu[:, 0:D_EXPERT]
        a = g * jax.nn.sigmoid(g) * gu[:, D_EXPERT:2 * D_EXPERT]
        col = jnp.sum(jnp.where(lane == e * ge + j, comb, 0.0), axis=1, keepdims=True)
        parts.append((a * col).astype(BF16))
    acc_ref[...] += jnp.dot(jnp.concatenate(parts, axis=1), wd_ref[...], preferred_element_type=F32)

    @pl.when(e == pl.num_programs(1) - 1)
    def _():
        gt2 = mod_ref[0:1, 5 * d:6 * d]
        x2 = x1_ref[...] + gt2 * acc_ref[...]
        o_ref[...] = _rms(x2) * fg_ref[...]


def _moe(h2, comb, wgu, wd, x1, mod, fg, tm, ge):
    L, d = x1.shape
    ne = wgu.shape[0]
    return pl.pallas_call(
        _moe_kernel,
        grid=(L // tm, ne // ge),
        in_specs=[pl.BlockSpec((tm, d), lambda i, e: (i, 0)),
                  pl.BlockSpec((tm, LANES), lambda i, e: (i, 0)),
                  pl.BlockSpec((ge, d, 2 * D_EXPERT), lambda i, e: (e, 0, 0)),
                  pl.BlockSpec((ge * D_EXPERT, d), lambda i, e: (e, 0)),
                  pl.BlockSpec((tm, d), lambda i, e: (i, 0)),
                  pl.BlockSpec(mod.shape, lambda i, e: (0, 0)),
                  pl.BlockSpec(fg.shape, lambda i, e: (0, 0))],
        out_specs=pl.BlockSpec((tm, d), lambda i, e: (i, 0)),
        out_shape=jax.ShapeDtypeStruct((L, d), F32),
        scratch_shapes=[pltpu.VMEM((tm, d), F32)],
        compiler_params=_cparams("parallel", "arbitrary"),
        name="moe",
    )(h2, comb, wgu, wd, x1, mod, fg)


def _pad_heads(w, width):
    lead = w.shape[:-1]
    w = w.reshape(lead + (N_HEADS, width))
    w = jnp.pad(w, [(0, 0)] * len(lead) + [(0, 0), (0, HEAD_SLAB - width)])
    return w.reshape(lead + (N_HEADS * HEAD_SLAB,))


def _rope_slab_tables(L):
    half = QK_ROPE // 2
    pos = jnp.arange(L, dtype=F32)
    inv = ROPE_THETA ** (-jnp.arange(0, QK_ROPE, 2, dtype=F32) / QK_ROPE)
    ang = pos[:, None] * inv[None, :]
    cos, sin = jnp.cos(ang), jnp.sin(ang)
    z = lambda w: jnp.zeros((L, w), F32)
    cos_t = jnp.concatenate([jnp.ones((L, QK_NOPE), F32), cos, cos, z(HEAD_SLAB - QK_NOPE - QK_ROPE)], axis=1)
    sin_a = jnp.concatenate([z(QK_NOPE), -sin, z(HEAD_SLAB - QK_NOPE - half)], axis=1)
    sin_b = jnp.concatenate([z(QK_NOPE + half), sin, z(HEAD_SLAB - QK_NOPE - QK_ROPE)], axis=1)
    return cos_t, sin_a, sin_b


def _filter_features(L):
    t = jnp.linspace(0.0, 1.0, L, dtype=F32)[:, None]
    t_r = jnp.arange(L, dtype=F32)[:, None]
    bands = jnp.linspace(1e-4, FILTER_BANDS - 1, FILTER_BANDS, dtype=F32)[None, :]
    ang = 2.0 * math.pi * bands * t_r / L
    z = jnp.concatenate([t, jnp.cos(ang), -jnp.sin(ang)], axis=-1)
    return jnp.pad(z, ((0, 0), (0, FILTER_HIDDEN - FILTER_EMB)))


def _dft_tables(L):
    n = 2 * L
    n1 = DFT_N1
    n2 = n // n1
    two_pi = 2.0 * math.pi
    k1 = jnp.arange(DFT_K1P, dtype=jnp.int32)[:, None]
    m1 = jnp.arange(n1 // 2, dtype=jnp.int32)[None, :]
    ang1 = two_pi * ((k1 * m1) % n1).astype(F32) / n1
    live = (k1 < DFT_K1).astype(F32)
    f1m = jnp.concatenate([jnp.cos(ang1) * live, -jnp.sin(ang1) * live], axis=0).astype(BF16)
    kk1 = jnp.arange(DFT_K1P, dtype=jnp.int32)[:, None, None]
    k2 = jnp.arange(n2, dtype=jnp.int32)[None, :, None]
    m2 = jnp.arange(n2, dtype=jnp.int32)[None, None, :]
    ang2 = two_pi * ((m2 * (n1 * k2 + kk1)) % n).astype(F32) / n
    live2 = (kk1 < DFT_K1).astype(F32)
    c2, s2 = jnp.cos(ang2) * live2, jnp.sin(ang2) * live2
    mr, mi = c2, -s2
    mf = jnp.concatenate([jnp.concatenate([mr, -mi], axis=2),
                          jnp.concatenate([mi, mr], axis=2)], axis=1).astype(BF16)
    pr, pi_ = jnp.swapaxes(c2, 1, 2), jnp.swapaxes(s2, 1, 2)
    minv = jnp.concatenate([jnp.concatenate([pr, -pi_], axis=2),
                            jnp.concatenate([pi_, pr], axis=2)], axis=1).astype(BF16)
    o1 = jnp.arange(n1 // 2, dtype=jnp.int32)[:, None]
    q1 = jnp.arange(DFT_K1P, dtype=jnp.int32)[None, :]
    ang3 = two_pi * ((o1 * q1) % n1).astype(F32) / n1
    wgt = jnp.where((q1 == 0) | (q1 == n1 // 2), 1.0, 2.0) * (q1 < DFT_K1).astype(F32)
    cmat = jnp.concatenate([wgt * jnp.cos(ang3), -wgt * jnp.sin(ang3)], axis=1).astype(BF16)
    return f1m, mf, minv, cmat, n2


def _pick(n, pref):
    t = pref
    while n % t:
        t //= 2
    return t


def kernel(x, c, ada_w, ada_b, norm_mix_g, w_in, hy_conv_w, hy_conv_b, filt_w1, filt_b1, filt_freq1, filt_w2, filt_b2, filt_freq2, filt_w3, hy_bias, q_norm_g, kv_norm_g, w_uq, w_uk, w_uv, w_hy_out, w_mla_out, w_mix_out, norm_ffn_g, w_group, b_group, w_router, b_router, w_gate, w_up, w_down, final_norm_g):
    B, L, d = x.shape
    assert B == 1 and d == D_MODEL and ada_w.shape[0] == 1
    x2 = x.reshape(L, d)
    row = lambda v: v.reshape(1, -1)

    mod = _ada(jnp.pad(c, ((0, 8 - B), (0, 0))), ada_w[0], row(ada_b[0]))

    wi = w_in[0]
    kr_slab = jnp.pad(wi[:, OFF_KR:OFF_GATE], ((0, 0), (QK_NOPE, HEAD_SLAB - QK_NOPE - QK_ROPE)))
    wcat = jnp.concatenate([wi[:, 0:OFF_Q], wi[:, OFF_Q:OFF_KV], wi[:, OFF_KV:OFF_KR],
                            wi[:, OFF_GATE:], kr_slab], axis=1).astype(BF16)
    wuq = _pad_heads(w_uq[0], QK_NOPE + QK_ROPE).astype(BF16)
    wuk = _pad_heads(w_uk[0].reshape(KV_LORA, N_HEADS * QK_NOPE), QK_NOPE).astype(BF16)
    wuv = _pad_heads(w_uv[0].reshape(KV_LORA, N_HEADS * V_HEAD), V_HEAD).astype(BF16)
    cos_t, sin_a, sin_b = _rope_slab_tables(L)

    tm = _pick(L, 256)
    hyp, gates, q, k, v = _inproj(x2, mod, row(norm_mix_g[0]), wcat, row(q_norm_g[0]), row(kv_norm_g[0]),
                                  wuq, wuk, wuv, cos_t, sin_a, sin_b, tm)

    u, x0 = _hyprep(hyp, hy_conv_w[0], row(hy_conv_b[0]), _pick(L, 512))
    zf = _filter_features(L)
    w1p = jnp.pad(filt_w1[0], ((0, FILTER_HIDDEN - FILTER_EMB), (0, 0)))
    dl = jnp.abs(jnp.linspace(MIN_DECAY, MAX_DECAY, HY_WIDTH, dtype=F32))[None, :]
    hfilt, nrm = _filt(zf, w1p, row(filt_b1[0]), row(filt_freq1[0]), filt_w2[0], row(filt_b2[0]),
                       row(filt_freq2[0]), filt_w3[0], dl, _pick(L, 512))
    f1m, mf, minv, cmat, n2 = _dft_tables(L)
    half = DFT_N1 // 2
    au = _dft1(f1m, u.reshape(half, n2 * HY_WIDTH), _pick(n2 * HY_WIDTH, 4096))
    ah = _dft1(f1m, hfilt.reshape(half, n2 * 2 * HY_WIDTH), _pick(n2 * 2 * HY_WIDTH, 4096))
    au = au.reshape(2, DFT_K1P, n2, HY_WIDTH)
    ah = ah.reshape(2, DFT_K1P, n2, 2 * HY_WIDTH)
    zs = _spec(au, ah, mf, minv, nrm, 3)
    tn = _pick(n2 * HY_WIDTH, 4096)
    bias_t = jnp.tile(row(hy_bias[0]), (1, tn // HY_WIDTH))
    yh = _idft1(cmat, zs.reshape(2 * DFT_K1P, n2 * HY_WIDTH), u.reshape(half, n2 * HY_WIDTH),
                x0.reshape(half, n2 * HY_WIDTH), bias_t, tn, 1.0 / (2 * L))
    yh = yh.reshape(L, HY_WIDTH)

    ym = _attn(q, k, v, _pick(L, 1024), _pick(L, 1024), 1, 4)

    wmla = jnp.pad(w_mla_out[0].reshape(N_HEADS, V_HEAD, d),
                   ((0, 0), (0, HEAD_SLAB - V_HEAD), (0, 0))).reshape(QK_SLABS, d).astype(BF16)
    wrt = jnp.concatenate([jnp.pad(w_router[0], ((0, 0), (0, LANES - N_EXPERTS))),
                           jnp.pad(w_group[0], ((0, 0), (0, LANES - N_GROUPS)))], axis=1)
    brt = jnp.concatenate([jnp.pad(b_router[0], (0, LANES - N_EXPERTS)),
                           jnp.pad(b_group[0], (0, LANES - N_GROUPS))])[None, :]
    x1, h2, comb = _merge(x2, yh, ym, gates, mod, w_hy_out[0].astype(BF16), wmla,
                          w_mix_out[0].astype(BF16), row(norm_ffn_g[0]), wrt, brt, _pick(L, 256))

    wgu = jnp.concatenate([w_gate[0], w_up[0]], axis=2).astype(BF16)
    wd = w_down[0].reshape(N_EXPERTS * D_EXPERT, d).astype(BF16)
    out = _moe(h2, comb, wgu, wd, x1, mod, row(final_norm_g), _pick(L, 1024), 4)
    return out.reshape(B, L, d)
```

```python
import functools
import math

import numpy as np
import jax
import jax.numpy as jnp
from jax import lax
from jax.experimental import pallas as pl
from jax.experimental.pallas import tpu as pltpu

F32 = jnp.float32
BF16 = jnp.bfloat16
HIGHEST = lax.Precision.HIGHEST

D_MODEL = 1024
EPS = 1e-6
HY_WIDTH = 512
FILTER_BANDS = 16
FILTER_EMB = 1 + 2 * FILTER_BANDS
FILTER_HIDDEN = 64
FILTER_DECAY_TARGET = 1e-2
FAST_DECAY_PCT = 0.3
SLOW_DECAY_PCT = 1.5
FILTER_SHIFT = 0.05
MIN_DECAY = math.log(FILTER_DECAY_TARGET) / SLOW_DECAY_PCT
MAX_DECAY = math.log(FILTER_DECAY_TARGET) / FAST_DECAY_PCT
N_HEADS = 8
QK_NOPE = 64
QK_ROPE = 32
V_HEAD = 64
Q_LORA = 384
KV_LORA = 256
ROPE_THETA = 10000.0
MLA_WIDTH = N_HEADS * V_HEAD
OFF_Q = 3 * HY_WIDTH
OFF_KV = OFF_Q + Q_LORA
OFF_KR = OFF_KV + KV_LORA
OFF_GATE = OFF_KR + QK_ROPE
N_GROUPS = 4
EXPERTS_PER_GROUP = 8
N_EXPERTS = N_GROUPS * EXPERTS_PER_GROUP
D_EXPERT = 256

LANES = 128
SUBLANES = 8
HEAD_SLAB = LANES
QK_SLABS = N_HEADS * HEAD_SLAB
ONES_LANE = V_HEAD
DFT_N2 = LANES
HY_PARAM_ROWS = 16
NEG_BIG = -1e30
VMEM_LIMIT = 56 * 1024 * 1024

C_Q = 0
C_KV = C_Q + Q_LORA
C_GATE = C_KV + KV_LORA
C_KR = C_GATE + 2 * D_MODEL
C_END = C_KR + HEAD_SLAB


def _cparams(*sem):
    return pltpu.CompilerParams(dimension_semantics=sem, vmem_limit_bytes=VMEM_LIMIT)


def _rms(x):
    return x * lax.rsqrt(jnp.mean(x * x, axis=-1, keepdims=True) + EPS)


def _dot_nt(a, b):
    return lax.dot_general(a, b, (((1,), (1,)), ((), ())), preferred_element_type=F32)


def _dot_tn(a, b):
    return lax.dot_general(a, b, (((0,), (0,)), ((), ())), preferred_element_type=F32)


def _ada_kernel(c_ref, w_ref, b_ref, o_ref):
    c = c_ref[...]
    ca = c * jax.nn.sigmoid(c)
    o_ref[...] = jnp.dot(ca, w_ref[...], preferred_element_type=F32, precision=HIGHEST) + b_ref[...]


def _ada(c8, w, b):
    d, n = w.shape
    tn = 1024
    return pl.pallas_call(
        _ada_kernel,
        grid=(n // tn,),
        in_specs=[pl.BlockSpec((8, d), lambda j: (0, 0)),
                  pl.BlockSpec((d, tn), lambda j: (0, j)),
                  pl.BlockSpec((1, tn), lambda j: (0, j))],
        out_specs=pl.BlockSpec((8, tn), lambda j: (0, j)),
        out_shape=jax.ShapeDtypeStruct((8, n), F32),
        compiler_params=_cparams("parallel"),
        name="ada",
    )(c8, w, b)


def _rope_slab(z, cos_t, sin_a, sin_b):
    return z * cos_t + pltpu.roll(z, HEAD_SLAB - QK_ROPE // 2, 1) * sin_a + pltpu.roll(z, QK_ROPE // 2, 1) * sin_b


def _inproj_kernel(x_ref, mod_ref, g_ref, wcat_ref, whyt_ref, gq_ref, gkv_ref, wuq_ref, wuk_ref, wuv_ref,
                   cos_ref, sina_ref, sinb_ref,
                   hypt_ref, gate_ref, q_ref, k_ref, v_ref, *, qscale):
    d = D_MODEL
    x = x_ref[...]
    sh1 = mod_ref[0:1, 0:d]
    sc1 = mod_ref[0:1, d:2 * d]
    hb = (_rms(x) * g_ref[...] * (1.0 + sc1) + sh1).astype(BF16)
    hypt_ref[...] = _dot_nt(whyt_ref[...], hb).astype(BF16)
    proj = jnp.dot(hb, wcat_ref[...], preferred_element_type=F32)
    gate_ref[...] = jax.nn.sigmoid(proj[:, C_GATE:C_KR]).astype(BF16)
    cq = _rms(proj[:, C_Q:C_KV]) * gq_ref[...]
    ckv = (_rms(proj[:, C_KV:C_GATE]) * gkv_ref[...]).astype(BF16)
    q = jnp.dot(cq.astype(BF16), wuq_ref[...], preferred_element_type=F32)
    kn = jnp.dot(ckv, wuk_ref[...], preferred_element_type=F32)
    vv = jnp.dot(ckv, wuv_ref[...], preferred_element_type=F32)
    cos_t = cos_ref[...]
    sin_a = sina_ref[...]
    sin_b = sinb_ref[...]
    kr = _rope_slab(proj[:, C_KR:C_END], cos_t, sin_a, sin_b)
    lane = lax.broadcasted_iota(jnp.int32, (1, HEAD_SLAB), 1)
    ones_col = jnp.where(lane == ONES_LANE, 1.0, 0.0).astype(F32)
    for hd in range(N_HEADS):
        sl = slice(hd * HEAD_SLAB, (hd + 1) * HEAD_SLAB)
        q_ref[:, sl] = (_rope_slab(q[:, sl], cos_t, sin_a, sin_b) * qscale).astype(BF16)
        k_ref[:, sl] = (kn[:, sl] + kr).astype(BF16)
        v_ref[:, sl] = (vv[:, sl] + ones_col).astype(BF16)


def _inproj(x2, mod, g1, wcat, whyt, gq, gkv, wuq, wuk, wuv, cos_t, sin_a, sin_b, tm):
    L, d = x2.shape
    qscale = (QK_NOPE + QK_ROPE) ** -0.5 * math.log2(math.e)
    full = lambda a: pl.BlockSpec(a.shape, lambda i: (0,) * a.ndim)
    row = lambda w: pl.BlockSpec((tm, w), lambda i: (i, 0))
    return pl.pallas_call(
        functools.partial(_inproj_kernel, qscale=qscale),
        grid=(L // tm,),
        in_specs=[row(d), full(mod), full(g1), full(wcat), full(whyt), full(gq), full(gkv), full(wuq),
                  full(wuk), full(wuv), row(HEAD_SLAB), row(HEAD_SLAB), row(HEAD_SLAB)],
        out_specs=[pl.BlockSpec((3 * HY_WIDTH, tm), lambda i: (0, i)),
                   row(2 * D_MODEL), row(QK_SLABS), row(QK_SLABS), row(QK_SLABS)],
        out_shape=[jax.ShapeDtypeStruct((3 * HY_WIDTH, L), BF16),
                   jax.ShapeDtypeStruct((L, 2 * D_MODEL), BF16),
                   jax.ShapeDtypeStruct((L, QK_SLABS), BF16),
                   jax.ShapeDtypeStruct((L, QK_SLABS), BF16),
                   jax.ShapeDtypeStruct((L, QK_SLABS), BF16)],
        compiler_params=_cparams("parallel"),
        name="inproj",
    )(x2, mod, g1, wcat, whyt, gq, gkv, wuq, wuk, wuv, cos_t, sin_a, sin_b)


def _filt_kernel(z_ref, w1_ref, b1_ref, f1_ref, w2_ref, b2_ref, f2_ref, w3_ref, dl_ref,
                 h_ref, nrm_ref):
    i = pl.program_id(0)
    z = z_ref[...]
    tm = z.shape[1]
    h = jnp.sin(f1_ref[...] * (jnp.dot(w1_ref[...], z, preferred_element_type=F32, precision=HIGHEST)
                               + b1_ref[...]))
    h = jnp.sin(f2_ref[...] * (jnp.dot(w2_ref[...], h, preferred_element_type=F32, precision=HIGHEST)
                               + b2_ref[...]))
    h3 = jnp.dot(w3_ref[...], h.astype(BF16), preferred_element_type=F32)
    t = z[0:1, :]
    window = jnp.exp(-dl_ref[...] * t) + FILTER_SHIFT
    hf = h3[0:HY_WIDTH] * window
    hb = h3[HY_WIDTH:2 * HY_WIDTH] * window
    col = lax.broadcasted_iota(jnp.int32, (1, tm), 1)
    hb = jnp.where((col == 0) & (i == 0), 0.0, hb)
    h_ref[0:HY_WIDTH, :] = hf.astype(BF16)
    h_ref[HY_WIDTH:2 * HY_WIDTH, :] = hb.astype(BF16)
    part = jnp.sum(jnp.abs(hf) + jnp.abs(hb), axis=1, keepdims=True)

    @pl.when(i == 0)
    def _():
        nrm_ref[...] = jnp.zeros_like(nrm_ref)

    nrm_ref[...] += part


def _filt(zt, w1t, b1, f1, w2t, b2, f2, w3t, dl, tm):
    L = zt.shape[1]
    full = lambda a: pl.BlockSpec(a.shape, lambda i: (0,) * a.ndim)
    return pl.pallas_call(
        _filt_kernel,
        grid=(L // tm,),
        in_specs=[pl.BlockSpec((zt.shape[0], tm), lambda i: (0, i)), full(w1t), full(b1), full(f1),
                  full(w2t), full(b2), full(f2), full(w3t), full(dl)],
        out_specs=[pl.BlockSpec((2 * HY_WIDTH, tm), lambda i: (0, i)),
                   pl.BlockSpec((HY_WIDTH, 1), lambda i: (0, 0))],
        out_shape=[jax.ShapeDtypeStruct((2 * HY_WIDTH, L), BF16),
                   jax.ShapeDtypeStruct((HY_WIDTH, 1), F32)],
        compiler_params=_cparams("arbitrary"),
        name="filt",
    )(zt, w1t, b1, f1, w2t, b2, f2, w3t, dl)


def _hyena_kernel(x0_ref, x1_ref, v_ref, hf_ref, hb_ref, prm_ref, f1_ref, tr_ref, ti_ref, f2_ref,
                  f2i_ref, cm_ref, sg_ref, o_ref, *, inv_n):
    cb, n1h, w = x0_ref.shape
    k1p = tr_ref.shape[0]
    lane = lax.broadcasted_iota(jnp.int32, (n1h, w), 1)
    rowi = lax.broadcasted_iota(jnp.int32, (n1h, w), 0)
    tr = tr_ref[...]
    ti = ti_ref[...]

    def conv(x, p, j):
        r = pltpu.roll(x, 1, 1)
        up = jnp.where(lane == 0, jnp.where(rowi == 0, 0.0, pltpu.roll(r, 1, 0)), r)
        l = pltpu.roll(x, w - 1, 1)
        dn = jnp.where(lane == w - 1, jnp.where(rowi == n1h - 1, 0.0, pltpu.roll(l, n1h - 1, 0)), l)
        return up * p[j:j + 1] + x * p[j + 1:j + 2] + dn * p[j + 2:j + 3] + p[j + 3:j + 4]

    def twiddle(a, s):
        ar = a[0:k1p, s * w:(s + 1) * w]
        ai = a[k1p:2 * k1p, s * w:(s + 1) * w]
        return jnp.concatenate([ar * tr - ai * ti, ar * ti + ai * tr], axis=1).astype(BF16)

    prm = [prm_ref[c] for c in range(cb)]
    x0s, us, sigs = [], [], []
    for c in range(cb):
        p = prm[c]
        x0s.append(conv(x0_ref[c].astype(F32), p, 0))
        u = conv(v_ref[c].astype(F32), p, 8) * conv(x1_ref[c].astype(F32), p, 4)
        us.append(u)
        sigs += [u.astype(BF16), hf_ref[c], hb_ref[c]]
    a = jnp.dot(f1_ref[...], jnp.concatenate(sigs, axis=1), preferred_element_type=F32)
    b = jnp.concatenate([twiddle(a, s) for s in range(3 * cb)], axis=0)
    s = jnp.dot(b, f2_ref[...], preferred_element_type=F32)
    ys = []
    for c in range(cb):
        su = s[(3 * c) * k1p:(3 * c + 1) * k1p]
        sf = s[(3 * c + 1) * k1p:(3 * c + 2) * k1p]
        sb = s[(3 * c + 2) * k1p:(3 * c + 3) * k1p]
        inv = 1.0 / prm[c][13:14]
        gr = (sf[:, 0:w] + sb[:, 0:w]) * inv
        gi = (sf[:, w:2 * w] - sb[:, w:2 * w]) * inv
        ur, ui = su[:, 0:w], su[:, w:2 * w]
        ys.append(jnp.concatenate([ur * gr - ui * gi, ur * gi + ui * gr], axis=1).astype(BF16))
    zt = jnp.dot(jnp.concatenate(ys, axis=0), f2i_ref[...], preferred_element_type=F32)
    zss, nyq = [], []
    for c in range(cb):
        ztr = zt[c * k1p:(c + 1) * k1p, 0:w]
        zti = zt[c * k1p:(c + 1) * k1p, w:2 * w]
        zr = ztr * tr + zti * ti
        zi = zti * tr - ztr * ti
        zss.append(jnp.concatenate([zr[0:n1h], zi[0:n1h]], axis=0).astype(BF16))
        nyq.append(zr[n1h:n1h + 1])
    yy = jnp.dot(cm_ref[...], jnp.concatenate(zss, axis=1), preferred_element_type=F32)
    for c in range(cb):
        conv_out = (yy[:, c * w:(c + 1) * w] + sg_ref[...] * nyq[c]) * inv_n
        o_ref[c] = (x0s[c] * (conv_out + prm[c][12:13] * us[c])).astype(BF16)


def _hyena(hyp3, h3, prm, tabs, cb, inv_n):
    c3, n1h, w = hyp3.shape
    c = c3 // 3
    nb = c // cb
    f1m, tr, ti, f2, f2i, cmat, sgn = tabs
    full = lambda a: pl.BlockSpec(a.shape, lambda i: (0,) * a.ndim)
    blk = lambda off: pl.BlockSpec((cb, n1h, w), lambda i: (off * nb + i, 0, 0))
    return pl.pallas_call(
        functools.partial(_hyena_kernel, inv_n=inv_n),
        grid=(nb,),
        in_specs=[blk(0), blk(1), blk(2), blk(0), blk(1),
                  pl.BlockSpec((cb, HY_PARAM_ROWS, w), lambda i: (i, 0, 0)),
                  full(f1m), full(tr), full(ti), full(f2), full(f2i), full(cmat), full(sgn)],
        out_specs=pl.BlockSpec((cb, n1h, w), lambda i: (i, 0, 0)),
        out_shape=jax.ShapeDtypeStruct((c, n1h, w), BF16),
        compiler_params=_cparams("parallel"),
        name="hyena",
    )(hyp3, hyp3, hyp3, h3, h3, prm, f1m, tr, ti, f2, f2i, cmat, sgn)


def _attn_kernel(q_ref, k_ref, v_ref, o_ref, *, tk, unroll):
    tq = q_ref.shape[0]
    nk = k_ref.shape[0] // tk
    q = q_ref[...]

    def body(j, carry):
        m, acc = carry
        off = pl.multiple_of(j * tk, tk)
        kc = k_ref[pl.ds(off, tk), :]
        vc = v_ref[pl.ds(off, tk), :]
        s = _dot_nt(q, kc)
        m_new = jnp.maximum(m, jnp.max(s, axis=1, keepdims=True))
        alpha = jnp.exp2(m - m_new)
        p = jnp.exp2(s - m_new)
        acc = alpha * acc + jnp.dot(p.astype(BF16), vc, preferred_element_type=F32)
        return m_new, acc

    m0 = jnp.full((tq, 1), NEG_BIG, F32)
    acc0 = jnp.zeros((tq, HEAD_SLAB), F32)
    _, acc = lax.fori_loop(0, nk, body, (m0, acc0), unroll=unroll)
    o_ref[...] = (acc / acc[:, ONES_LANE:ONES_LANE + 1]).astype(BF16)


def _attn(q, k, v, tq, tk, unroll):
    L = q.shape[0]
    return pl.pallas_call(
        functools.partial(_attn_kernel, tk=tk, unroll=unroll),
        grid=(N_HEADS, L // tq),
        in_specs=[pl.BlockSpec((tq, HEAD_SLAB), lambda h, i: (i, h)),
                  pl.BlockSpec((L, HEAD_SLAB), lambda h, i: (0, h)),
                  pl.BlockSpec((L, HEAD_SLAB), lambda h, i: (0, h))],
        out_specs=pl.BlockSpec((tq, HEAD_SLAB), lambda h, i: (i, h)),
        out_shape=jax.ShapeDtypeStruct((L, QK_SLABS), BF16),
        compiler_params=_cparams("parallel", "parallel"),
        name="attn",
    )(q, k, v)


def _merge_kernel(x_ref, yht_ref, ym_ref, gate_ref, mod_ref, why_ref, wmla_ref, wmix_ref, g2_ref,
                  wrt_ref, brt_ref, x1_ref, h2_ref, comb_ref):
    d = D_MODEL
    a = _dot_tn(yht_ref[...], why_ref[...])
    b = jnp.dot(ym_ref[...], wmla_ref[...], preferred_element_type=F32)
    mix = gate_ref[:, 0:d].astype(F32) * a + gate_ref[:, d:2 * d].astype(F32) * b
    o = jnp.dot(mix.astype(BF16), wmix_ref[...], preferred_element_type=F32)
    gt1 = mod_ref[0:1, 2 * d:3 * d]
    sh2 = mod_ref[0:1, 3 * d:4 * d]
    sc2 = mod_ref[0:1, 4 * d:5 * d]
    x1 = x_ref[...] + gt1 * o
    x1_ref[...] = x1
    h2 = _rms(x1) * g2_ref[...] * (1.0 + sc2) + sh2
    h2_hi = h2.astype(BF16)
    h2_ref[...] = h2_hi
    h2_lo = (h2 - h2_hi.astype(F32)).astype(BF16)
    w_hi = wrt_ref[0]
    e = (jnp.dot(h2_hi, w_hi, preferred_element_type=F32)
         + (jnp.dot(h2_lo, w_hi, preferred_element_type=F32)
            + jnp.dot(h2_hi, wrt_ref[1], preferred_element_type=F32))) + brt_ref[...]
    lane = lax.broadcasted_iota(jnp.int32, e.shape, 1).astype(F32)
    gvalid = (lane >= N_EXPERTS) & (lane < N_EXPERTS + N_GROUPS)
    glm = jnp.where(gvalid, e, NEG_BIG)
    gmax = jnp.max(glm, axis=1, keepdims=True)
    gidx = jnp.min(jnp.where(glm == gmax, lane, float(LANES)), axis=1, keepdims=True) - float(N_EXPERTS)
    psum = jnp.sum(jnp.where(gvalid, jnp.exp(glm - gmax), 0.0), axis=1, keepdims=True)
    p_sel = 1.0 / psum
    lo = gidx * EXPERTS_PER_GROUP
    em = jnp.where((lane >= lo) & (lane < lo + EXPERTS_PER_GROUP), e, NEG_BIG)
    v1 = jnp.max(em, axis=1, keepdims=True)
    i1 = jnp.min(jnp.where(em == v1, lane, float(LANES)), axis=1, keepdims=True)
    em2 = jnp.where(lane == i1, NEG_BIG, em)
    v2 = jnp.max(em2, axis=1, keepdims=True)
    i2 = jnp.min(jnp.where(em2 == v2, lane, float(LANES)), axis=1, keepdims=True)
    t = jnp.exp(v2 - v1)
    w1 = p_sel / (1.0 + t)
    w2 = p_sel * t / (1.0 + t)
    comb_ref[...] = jnp.where(lane == i1, w1, 0.0) + jnp.where(lane == i2, w2, 0.0)


def _merge(x2, yht, ym, gates, mod, why, wmla, wmix, g2, wrt, brt, tm):
    L, d = x2.shape
    full = lambda a: pl.BlockSpec(a.shape, lambda i: (0,) * a.ndim)
    row = lambda w: pl.BlockSpec((tm, w), lambda i: (i, 0))
    return pl.pallas_call(
        _merge_kernel,
        grid=(L // tm,),
        in_specs=[row(d), pl.BlockSpec((HY_WIDTH, tm), lambda i: (0, i)), row(QK_SLABS), row(2 * d),
                  full(mod), full(why), full(wmla), full(wmix), full(g2), full(wrt), full(brt)],
        out_specs=[row(d), row(d), row(LANES)],
        out_shape=[jax.ShapeDtypeStruct((L, d), F32),
                   jax.ShapeDtypeStruct((L, d), BF16),
                   jax.ShapeDtypeStruct((L, LANES), F32)],
        compiler_params=_cparams("parallel"),
        name="merge",
    )(x2, yht, ym, gates, mod, why, wmla, wmix, g2, wrt, brt)


def _moe_kernel(h_ref, comb_ref, wgu_ref, wd_ref, x1_ref, mod_ref, fg_ref, o_ref, acc_ref):
    d = D_MODEL
    e = pl.program_id(1)

    @pl.when(e == 0)
    def _():
        acc_ref[...] = jnp.zeros_like(acc_ref)

    h = h_ref[...]
    comb = comb_ref[...]
    lane = lax.broadcasted_iota(jnp.int32, comb.shape, 1)
    ge = wgu_ref.shape[0]
    parts = []
    for j in range(ge):
        gu = jnp.dot(h, wgu_ref[j], preferred_element_type=F32)
        g = gu[:, 0:D_EXPERT]
        a = g * jax.nn.sigmoid(g) * gu[:, D_EXPERT:2 * D_EXPERT]
        col = jnp.sum(jnp.where(lane == e * ge + j, comb, 0.0), axis=1, keepdims=True)
        parts.append((a * col).astype(BF16))
    acc_ref[...] += jnp.dot(jnp.concatenate(parts, axis=1), wd_ref[...], preferred_element_type=F32)

    @pl.when(e == pl.num_programs(1) - 1)
    def _():
        gt2 = mod_ref[0:1, 5 * d:6 * d]
        x2 = x1_ref[...] + gt2 * acc_ref[...]
        o_ref[...] = _rms(x2) * fg_ref[...]


def _moe(h2, comb, wgu, wd, x1, mod, fg, tm, ge):
    L, d = x1.shape
    ne = wgu.shape[0]
    return pl.pallas_call(
        _moe_kernel,
        grid=(L // tm, ne // ge),
        in_specs=[pl.BlockSpec((tm, d), lambda i, e: (i, 0)),
                  pl.BlockSpec((tm, LANES), lambda i, e: (i, 0)),
                  pl.BlockSpec((ge, d, 2 * D_EXPERT), lambda i, e: (e, 0, 0)),
                  pl.BlockSpec((ge * D_EXPERT, d), lambda i, e: (e, 0)),
                  pl.BlockSpec((tm, d), lambda i, e: (i, 0)),
                  pl.BlockSpec(mod.shape, lambda i, e: (0, 0)),
                  pl.BlockSpec(fg.shape, lambda i, e: (0, 0))],
        out_specs=pl.BlockSpec((tm, d), lambda i, e: (i, 0)),
        out_shape=jax.ShapeDtypeStruct((L, d), F32),
        scratch_shapes=[pltpu.VMEM((tm, d), F32)],
        compiler_params=_cparams("parallel", "arbitrary"),
        name="moe",
    )(h2, comb, wgu, wd, x1, mod, fg)


def _pad_heads(w, width):
    lead = w.shape[:-1]
    w = w.reshape(lead + (N_HEADS, width))
    w = jnp.pad(w, [(0, 0)] * len(lead) + [(0, 0), (0, HEAD_SLAB - width)])
    return w.reshape(lead + (N_HEADS * HEAD_SLAB,))


@functools.lru_cache(maxsize=None)
def _rope_slab_tables(L):
    half = QK_ROPE // 2
    pos = np.arange(L, dtype=np.float64)
    inv = ROPE_THETA ** (-np.arange(0, QK_ROPE, 2, dtype=np.float64) / QK_ROPE)
    ang = pos[:, None] * inv[None, :]
    cos, sin = np.cos(ang), np.sin(ang)
    z = lambda w: np.zeros((L, w))
    cos_t = np.concatenate([np.ones((L, QK_NOPE)), cos, cos, z(HEAD_SLAB - QK_NOPE - QK_ROPE)], axis=1)
    sin_a = np.concatenate([z(QK_NOPE), -sin, z(HEAD_SLAB - QK_NOPE - half)], axis=1)
    sin_b = np.concatenate([z(QK_NOPE + half), sin, z(HEAD_SLAB - QK_NOPE - QK_ROPE)], axis=1)
    return cos_t.astype(np.float32), sin_a.astype(np.float32), sin_b.astype(np.float32)


@functools.lru_cache(maxsize=None)
def _filter_features(L):
    t = np.linspace(0.0, 1.0, L)[None, :]
    t_r = np.arange(L, dtype=np.float64)[None, :]
    bands = np.linspace(1e-4, FILTER_BANDS - 1, FILTER_BANDS)[:, None]
    ang = 2.0 * math.pi * bands * t_r / L
    z = np.concatenate([t, np.cos(ang), -np.sin(ang)], axis=0)
    return np.pad(z, ((0, FILTER_HIDDEN - FILTER_EMB), (0, 0))).astype(np.float32)


@functools.lru_cache(maxsize=None)
def _dft_tables(L):
    n = 2 * L
    n2 = DFT_N2
    n1 = n // n2
    n1h = n1 // 2
    k1n = n1h + 1
    k1p = -(-k1n // SUBLANES) * SUBLANES
    two_pi = 2.0 * math.pi
    k1 = np.arange(k1p, dtype=np.int64)[:, None]
    live = (k1 < k1n).astype(np.float64)
    m1 = np.arange(n1h, dtype=np.int64)[None, :]
    ang1 = two_pi * ((k1 * m1) % n1) / n1
    f1m = np.concatenate([np.cos(ang1) * live, -np.sin(ang1) * live], axis=0).astype(BF16)
    m2 = np.arange(n2, dtype=np.int64)[None, :]
    ang2 = two_pi * ((k1 * m2) % n) / n
    tr = (np.cos(ang2) * live).astype(np.float32)
    ti = (-np.sin(ang2) * live).astype(np.float32)
    ang3 = two_pi * ((m2.T * m2) % n2) / n2
    f2r, f2i = np.cos(ang3), -np.sin(ang3)
    f2 = np.block([[f2r, f2i], [-f2i, f2r]]).astype(BF16)
    f2inv = np.block([[f2r, -f2i], [f2i, f2r]]).astype(BF16)
    o1 = np.arange(n1h, dtype=np.int64)[:, None]
    q1 = np.arange(n1h, dtype=np.int64)[None, :]
    ang4 = two_pi * ((o1 * q1) % n1) / n1
    wgt = np.where(q1 == 0, 1.0, 2.0)
    cmat = np.concatenate([wgt * np.cos(ang4), -wgt * np.sin(ang4)], axis=1).astype(BF16)
    sgn = np.broadcast_to(np.where(o1 % 2 == 0, 1.0, -1.0), (n1h, n2)).astype(np.float32)
    return (f1m, tr, ti, f2, f2inv, cmat, sgn), n1h


def _pick(n, pref):
    t = pref
    while n % t:
        t //= 2
    return t


def kernel(x, c, ada_w, ada_b, norm_mix_g, w_in, hy_conv_w, hy_conv_b, filt_w1, filt_b1, filt_freq1, filt_w2, filt_b2, filt_freq2, filt_w3, hy_bias, q_norm_g, kv_norm_g, w_uq, w_uk, w_uv, w_hy_out, w_mla_out, w_mix_out, norm_ffn_g, w_group, b_group, w_router, b_router, w_gate, w_up, w_down, final_norm_g):
    B, L, d = x.shape
    assert B == 1 and d == D_MODEL and ada_w.shape[0] == 1 and L % (2 * DFT_N2) == 0
    x2 = x.reshape(L, d)
    row = lambda v: v.reshape(1, -1)
    col = lambda v: v.reshape(-1, 1)

    mod = _ada(jnp.pad(c, ((0, 8 - B), (0, 0))), ada_w[0], row(ada_b[0]))

    wi = w_in[0]
    kr_slab = jnp.pad(wi[:, OFF_KR:OFF_GATE], ((0, 0), (QK_NOPE, HEAD_SLAB - QK_NOPE - QK_ROPE)))
    wcat = jnp.concatenate([wi[:, OFF_Q:OFF_KV], wi[:, OFF_KV:OFF_KR], wi[:, OFF_GATE:], kr_slab],
                           axis=1).astype(BF16)
    whyt = wi[:, 0:OFF_Q].T.astype(BF16)
    wuq = _pad_heads(w_uq[0], QK_NOPE + QK_ROPE).astype(BF16)
    wuk = _pad_heads(w_uk[0].reshape(KV_LORA, N_HEADS * QK_NOPE), QK_NOPE).astype(BF16)
    wuv = _pad_heads(w_uv[0].reshape(KV_LORA, N_HEADS * V_HEAD), V_HEAD).astype(BF16)
    cos_t, sin_a, sin_b = _rope_slab_tables(L)

    hypt, gates, q, k, v = _inproj(x2, mod, row(norm_mix_g[0]), wcat, whyt, row(q_norm_g[0]),
                                   row(kv_norm_g[0]), wuq, wuk, wuv, cos_t, sin_a, sin_b, _pick(L, 512))

    zt = _filter_features(L)
    w1t = jnp.pad(filt_w1[0], ((0, FILTER_HIDDEN - FILTER_EMB), (0, 0))).T
    dl = np.abs(np.linspace(MIN_DECAY, MAX_DECAY, HY_WIDTH))[:, None].astype(np.float32)
    ht, nrm = _filt(zt, w1t, col(filt_b1[0]), col(filt_freq1[0]), filt_w2[0].T, col(filt_b2[0]),
                    col(filt_freq2[0]), filt_w3[0].T.astype(BF16), dl, _pick(L, 1024))
    tabs, n1h = _dft_tables(L)
    cw, cbias = hy_conv_w[0], hy_conv_b[0]
    w = HY_WIDTH
    taps = lambda s: [cw[0, s * w:(s + 1) * w], cw[1, s * w:(s + 1) * w], cw[2, s * w:(s + 1) * w],
                      cbias[s * w:(s + 1) * w]]
    prm = jnp.stack(taps(0) + taps(1) + taps(2) + [hy_bias[0], nrm[:, 0]]
                    + [jnp.zeros((w,), F32)] * (HY_PARAM_ROWS - 14), axis=1)
    prm = jnp.broadcast_to(prm[:, :, None], (w, HY_PARAM_ROWS, DFT_N2))
    yht = _hyena(hypt.reshape(3 * w, n1h, DFT_N2), ht.reshape(2 * w, n1h, DFT_N2), prm, tabs,
                 _pick(w, 8), 1.0 / (2 * L)).reshape(w, L)

    ym = _attn(q, k, v, _pick(L, 1024), _pick(L, 1024), 4)

    wmla = jnp.pad(w_mla_out[0].reshape(N_HEADS, V_HEAD, d),
                   ((0, 0), (0, HEAD_SLAB - V_HEAD), (0, 0))).reshape(QK_SLABS, d).astype(BF16)
    wrt32 = jnp.pad(jnp.concatenate([w_router[0], w_group[0]], axis=1),
                    ((0, 0), (0, LANES - N_EXPERTS - N_GROUPS)))
    wrt_hi32 = lax.reduce_precision(wrt32, exponent_bits=8, mantissa_bits=7)
    wrt = jnp.stack([wrt_hi32.astype(BF16), (wrt32 - wrt_hi32).astype(BF16)])
    brt = jnp.pad(jnp.concatenate([b_router[0], b_group[0]]), (0, LANES - N_EXPERTS - N_GROUPS))[None, :]
    x1, h2, comb = _merge(x2, yht, ym, gates, mod, w_hy_out[0].astype(BF16), wmla,
                          w_mix_out[0].astype(BF16), row(norm_ffn_g[0]), wrt, brt, _pick(L, 512))

    wgu = jnp.concatenate([w_gate[0], w_up[0]], axis=2).astype(BF16)
    wd = w_down[0].reshape(N_EXPERTS * D_EXPERT, d).astype(BF16)
    out = _moe(h2, comb, wgu, wd, x1, mod, row(final_norm_g), _pick(L, 1024), 4)
    return out.reshape(B, L, d)
```

```python
import functools
import math

import numpy as np
import jax
import jax.numpy as jnp
from jax import lax
from jax.experimental import pallas as pl
from jax.experimental.pallas import tpu as pltpu

F32 = jnp.float32
BF16 = jnp.bfloat16
HIGHEST = lax.Precision.HIGHEST

D_MODEL = 1024
EPS = 1e-6
HY_WIDTH = 512
FILTER_BANDS = 16
FILTER_EMB = 1 + 2 * FILTER_BANDS
FILTER_HIDDEN = 64
FILTER_DECAY_TARGET = 1e-2
FAST_DECAY_PCT = 0.3
SLOW_DECAY_PCT = 1.5
FILTER_SHIFT = 0.05
MIN_DECAY = math.log(FILTER_DECAY_TARGET) / SLOW_DECAY_PCT
MAX_DECAY = math.log(FILTER_DECAY_TARGET) / FAST_DECAY_PCT
N_HEADS = 8
QK_NOPE = 64
QK_ROPE = 32
V_HEAD = 64
Q_LORA = 384
KV_LORA = 256
ROPE_THETA = 10000.0
MLA_WIDTH = N_HEADS * V_HEAD
OFF_Q = 3 * HY_WIDTH
OFF_KV = OFF_Q + Q_LORA
OFF_KR = OFF_KV + KV_LORA
OFF_GATE = OFF_KR + QK_ROPE
N_GROUPS = 4
EXPERTS_PER_GROUP = 8
N_EXPERTS = N_GROUPS * EXPERTS_PER_GROUP
D_EXPERT = 256

LANES = 128
SUBLANES = 8
HEAD_SLAB = LANES
QK_SLABS = N_HEADS * HEAD_SLAB
ONES_LANE = V_HEAD
DFT_N2 = LANES
HY_PARAM_ROWS = 16
NEG_BIG = -1e30
VMEM_LIMIT = 56 * 1024 * 1024

C_Q = 0
C_KV = C_Q + Q_LORA
C_GATE = C_KV + KV_LORA
C_KR = C_GATE + 2 * D_MODEL
C_END = C_KR + HEAD_SLAB


def _cparams(*sem):
    return pltpu.CompilerParams(dimension_semantics=sem, vmem_limit_bytes=VMEM_LIMIT)


def _rms(x):
    return x * lax.rsqrt(jnp.mean(x * x, axis=-1, keepdims=True) + EPS)


def _dot_nt(a, b):
    return lax.dot_general(a, b, (((1,), (1,)), ((), ())), preferred_element_type=F32)


def _dot_tn(a, b):
    return lax.dot_general(a, b, (((0,), (0,)), ((), ())), preferred_element_type=F32)


def _ada_kernel(c_ref, w_ref, b_ref, o_ref):
    c = c_ref[...]
    ca = c * jax.nn.sigmoid(c)
    o_ref[...] = jnp.dot(ca, w_ref[...], preferred_element_type=F32, precision=HIGHEST) + b_ref[...]


def _ada(c8, w, b):
    d, n = w.shape
    tn = 1024
    return pl.pallas_call(
        _ada_kernel,
        grid=(n // tn,),
        in_specs=[pl.BlockSpec((8, d), lambda j: (0, 0)),
                  pl.BlockSpec((d, tn), lambda j: (0, j)),
                  pl.BlockSpec((1, tn), lambda j: (0, j))],
        out_specs=pl.BlockSpec((8, tn), lambda j: (0, j)),
        out_shape=jax.ShapeDtypeStruct((8, n), F32),
        compiler_params=_cparams("parallel"),
        name="ada",
    )(c8, w, b)


def _rope_slab(z, cos_t, sin_a, sin_b):
    return z * cos_t + pltpu.roll(z, HEAD_SLAB - QK_ROPE // 2, 1) * sin_a + pltpu.roll(z, QK_ROPE // 2, 1) * sin_b


def _inproj_kernel(x_ref, mod_ref, g_ref, wcat_ref, whyt_ref, gq_ref, gkv_ref, wuq_ref, wuk_ref, wuvt_ref,
                   cos_ref, sina_ref, sinb_ref,
                   hypt_ref, gate_ref, q_ref, k_ref, vt_ref, *, qscale):
    d = D_MODEL
    x = x_ref[...]
    sh1 = mod_ref[0:1, 0:d]
    sc1 = mod_ref[0:1, d:2 * d]
    hb = (_rms(x) * g_ref[...] * (1.0 + sc1) + sh1).astype(BF16)
    hypt_ref[...] = _dot_nt(whyt_ref[...], hb).astype(BF16)
    proj = jnp.dot(hb, wcat_ref[...], preferred_element_type=F32)
    gate_ref[...] = jax.nn.sigmoid(proj[:, C_GATE:C_KR]).astype(BF16)
    cq = _rms(proj[:, C_Q:C_KV]) * gq_ref[...]
    ckv = (_rms(proj[:, C_KV:C_GATE]) * gkv_ref[...]).astype(BF16)
    q = jnp.dot(cq.astype(BF16), wuq_ref[...], preferred_element_type=F32)
    kn = jnp.dot(ckv, wuk_ref[...], preferred_element_type=F32)
    vt = _dot_nt(wuvt_ref[...], ckv)
    cos_t = cos_ref[...]
    sin_a = sina_ref[...]
    sin_b = sinb_ref[...]
    kr = _rope_slab(proj[:, C_KR:C_END], cos_t, sin_a, sin_b)
    srow = lax.broadcasted_iota(jnp.int32, (HEAD_SLAB, 1), 0)
    ones_row = jnp.where(srow == ONES_LANE, 1.0, 0.0).astype(F32)
    for hd in range(N_HEADS):
        sl = slice(hd * HEAD_SLAB, (hd + 1) * HEAD_SLAB)
        q_ref[:, sl] = (_rope_slab(q[:, sl], cos_t, sin_a, sin_b) * qscale).astype(BF16)
        k_ref[:, sl] = (kn[:, sl] + kr).astype(BF16)
        vt_ref[hd, 0] = (vt[sl] + ones_row).astype(BF16)


def _inproj(x2, mod, g1, wcat, whyt, gq, gkv, wuq, wuk, wuvt, cos_t, sin_a, sin_b, tm, tkv):
    L, d = x2.shape
    qscale = (QK_NOPE + QK_ROPE) ** -0.5 * math.log2(math.e)
    full = lambda a: pl.BlockSpec(a.shape, lambda i: (0,) * a.ndim)
    row = lambda w: pl.BlockSpec((tm, w), lambda i: (i, 0))
    per = tkv // tm
    return pl.pallas_call(
        functools.partial(_inproj_kernel, qscale=qscale),
        grid=(L // tm,),
        in_specs=[row(d), full(mod), full(g1), full(wcat), full(whyt), full(gq), full(gkv), full(wuq),
                  full(wuk), full(wuvt), row(HEAD_SLAB), row(HEAD_SLAB), row(HEAD_SLAB)],
        out_specs=[pl.BlockSpec((3 * HY_WIDTH, tm), lambda i: (0, i)),
                   row(2 * D_MODEL), row(QK_SLABS), row(QK_SLABS),
                   pl.BlockSpec((N_HEADS, 1, HEAD_SLAB, tm), lambda i: (0, i // per, 0, i % per))],
        out_shape=[jax.ShapeDtypeStruct((3 * HY_WIDTH, L), BF16),
                   jax.ShapeDtypeStruct((L, 2 * D_MODEL), BF16),
                   jax.ShapeDtypeStruct((L, QK_SLABS), BF16),
                   jax.ShapeDtypeStruct((L, QK_SLABS), BF16),
                   jax.ShapeDtypeStruct((N_HEADS, L // tkv, HEAD_SLAB, tkv), BF16)],
        compiler_params=_cparams("parallel"),
        name="inproj",
    )(x2, mod, g1, wcat, whyt, gq, gkv, wuq, wuk, wuvt, cos_t, sin_a, sin_b)


def _filt_kernel(z_ref, w1_ref, b1_ref, f1_ref, w2_ref, b2_ref, f2_ref, w3_ref, dl_ref,
                 h_ref, nrm_ref):
    i = pl.program_id(0)
    z = z_ref[...]
    tm = z.shape[1]
    h = jnp.sin(f1_ref[...] * (jnp.dot(w1_ref[...], z, preferred_element_type=F32, precision=HIGHEST)
                               + b1_ref[...]))
    h = jnp.sin(f2_ref[...] * (jnp.dot(w2_ref[...], h, preferred_element_type=F32, precision=HIGHEST)
                               + b2_ref[...]))
    h3 = jnp.dot(w3_ref[...], h.astype(BF16), preferred_element_type=F32)
    t = z[0:1, :]
    window = jnp.exp(-dl_ref[...] * t) + FILTER_SHIFT
    hf = h3[0:HY_WIDTH] * window
    hb = h3[HY_WIDTH:2 * HY_WIDTH] * window
    col = lax.broadcasted_iota(jnp.int32, (1, tm), 1)
    hb = jnp.where((col == 0) & (i == 0), 0.0, hb)
    h_ref[0:HY_WIDTH, :] = hf.astype(BF16)
    h_ref[HY_WIDTH:2 * HY_WIDTH, :] = hb.astype(BF16)
    part = jnp.sum(jnp.abs(hf) + jnp.abs(hb), axis=1, keepdims=True)

    @pl.when(i == 0)
    def _():
        nrm_ref[...] = jnp.zeros_like(nrm_ref)

    nrm_ref[...] += part


def _filt(zt, w1t, b1, f1, w2t, b2, f2, w3t, dl, tm):
    L = zt.shape[1]
    full = lambda a: pl.BlockSpec(a.shape, lambda i: (0,) * a.ndim)
    return pl.pallas_call(
        _filt_kernel,
        grid=(L // tm,),
        in_specs=[pl.BlockSpec((zt.shape[0], tm), lambda i: (0, i)), full(w1t), full(b1), full(f1),
                  full(w2t), full(b2), full(f2), full(w3t), full(dl)],
        out_specs=[pl.BlockSpec((2 * HY_WIDTH, tm), lambda i: (0, i)),
                   pl.BlockSpec((HY_WIDTH, 1), lambda i: (0, 0))],
        out_shape=[jax.ShapeDtypeStruct((2 * HY_WIDTH, L), BF16),
                   jax.ShapeDtypeStruct((HY_WIDTH, 1), F32)],
        compiler_params=_cparams("arbitrary"),
        name="filt",
    )(zt, w1t, b1, f1, w2t, b2, f2, w3t, dl)


def _hyena_kernel(x0_ref, x1_ref, v_ref, hf_ref, hb_ref, prm_ref, f1_ref, tr_ref, ti_ref, f2_ref,
                  f2i_ref, cm_ref, sg_ref, o_ref, *, inv_n):
    cb, n1h, w = x0_ref.shape
    k1p = tr_ref.shape[0]
    lane = lax.broadcasted_iota(jnp.int32, (n1h, w), 1)
    rowi = lax.broadcasted_iota(jnp.int32, (n1h, w), 0)
    tr = tr_ref[...]
    ti = ti_ref[...]

    def conv(x, p, j):
        r = pltpu.roll(x, 1, 1)
        up = jnp.where(lane == 0, jnp.where(rowi == 0, 0.0, pltpu.roll(r, 1, 0)), r)
        l = pltpu.roll(x, w - 1, 1)
        dn = jnp.where(lane == w - 1, jnp.where(rowi == n1h - 1, 0.0, pltpu.roll(l, n1h - 1, 0)), l)
        return up * p[j:j + 1] + x * p[j + 1:j + 2] + dn * p[j + 2:j + 3] + p[j + 3:j + 4]

    def twiddle(a, s):
        ar = a[0:k1p, s * w:(s + 1) * w]
        ai = a[k1p:2 * k1p, s * w:(s + 1) * w]
        return jnp.concatenate([ar * tr - ai * ti, ar * ti + ai * tr], axis=1).astype(BF16)

    prm = [prm_ref[c] for c in range(cb)]
    x0s, us, sigs = [], [], []
    for c in range(cb):
        p = prm[c]
        x0s.append(conv(x0_ref[c].astype(F32), p, 0))
        u = conv(v_ref[c].astype(F32), p, 8) * conv(x1_ref[c].astype(F32), p, 4)
        us.append(u)
        sigs += [u.astype(BF16), hf_ref[c], hb_ref[c]]
    a = jnp.dot(f1_ref[...], jnp.concatenate(sigs, axis=1), preferred_element_type=F32)
    b = jnp.concatenate([twiddle(a, s) for s in range(3 * cb)], axis=0)
    s = jnp.dot(b, f2_ref[...], preferred_element_type=F32)
    ys = []
    for c in range(cb):
        su = s[(3 * c) * k1p:(3 * c + 1) * k1p]
        sf = s[(3 * c + 1) * k1p:(3 * c + 2) * k1p]
        sb = s[(3 * c + 2) * k1p:(3 * c + 3) * k1p]
        inv = 1.0 / prm[c][13:14]
        gr = (sf[:, 0:w] + sb[:, 0:w]) * inv
        gi = (sf[:, w:2 * w] - sb[:, w:2 * w]) * inv
        ur, ui = su[:, 0:w], su[:, w:2 * w]
        ys.append(jnp.concatenate([ur * gr - ui * gi, ur * gi + ui * gr], axis=1).astype(BF16))
    zt = jnp.dot(jnp.concatenate(ys, axis=0), f2i_ref[...], preferred_element_type=F32)
    zss, nyq = [], []
    for c in range(cb):
        ztr = zt[c * k1p:(c + 1) * k1p, 0:w]
        zti = zt[c * k1p:(c + 1) * k1p, w:2 * w]
        zr = ztr * tr + zti * ti
        zi = zti * tr - ztr * ti
        zss.append(jnp.concatenate([zr[0:n1h], zi[0:n1h]], axis=0).astype(BF16))
        nyq.append(zr[n1h:n1h + 1])
    yy = jnp.dot(cm_ref[...], jnp.concatenate(zss, axis=1), preferred_element_type=F32)
    for c in range(cb):
        conv_out = (yy[:, c * w:(c + 1) * w] + sg_ref[...] * nyq[c]) * inv_n
        o_ref[c] = (x0s[c] * (conv_out + prm[c][12:13] * us[c])).astype(BF16)


def _hyena(hyp3, h3, prm, tabs, cb, inv_n):
    c3, n1h, w = hyp3.shape
    c = c3 // 3
    nb = c // cb
    f1m, tr, ti, f2, f2i, cmat, sgn = tabs
    full = lambda a: pl.BlockSpec(a.shape, lambda i: (0,) * a.ndim)
    blk = lambda off: pl.BlockSpec((cb, n1h, w), lambda i: (off * nb + i, 0, 0))
    return pl.pallas_call(
        functools.partial(_hyena_kernel, inv_n=inv_n),
        grid=(nb,),
        in_specs=[blk(0), blk(1), blk(2), blk(0), blk(1),
                  pl.BlockSpec((cb, HY_PARAM_ROWS, w), lambda i: (i, 0, 0)),
                  full(f1m), full(tr), full(ti), full(f2), full(f2i), full(cmat), full(sgn)],
        out_specs=pl.BlockSpec((cb, n1h, w), lambda i: (i, 0, 0)),
        out_shape=jax.ShapeDtypeStruct((c, n1h, w), BF16),
        compiler_params=_cparams("parallel"),
        name="hyena",
    )(hyp3, hyp3, hyp3, h3, h3, prm, f1m, tr, ti, f2, f2i, cmat, sgn)


def _attn_kernel(q_ref, k_ref, vt_ref, o_ref, *, unroll):
    tq = q_ref.shape[0]
    nk, _, tk = vt_ref.shape[1:]
    q = q_ref[...]

    def body(j, carry):
        m, acc = carry
        off = pl.multiple_of(j * tk, tk)
        kc = k_ref[pl.ds(off, tk), :]
        st = _dot_nt(kc, q)
        m_new = jnp.maximum(m, jnp.max(st, axis=0, keepdims=True))
        alpha = jnp.exp2(m - m_new)
        p = jnp.exp2(st - m_new)
        acc = alpha * acc + jnp.dot(vt_ref[0, j], p.astype(BF16), preferred_element_type=F32)
        return m_new, acc

    m0 = jnp.full((1, tq), NEG_BIG, F32)
    acc0 = jnp.zeros((HEAD_SLAB, tq), F32)
    _, acc = lax.fori_loop(0, nk, body, (m0, acc0), unroll=unroll)
    o_ref[...] = (acc / acc[ONES_LANE:ONES_LANE + 1, :]).astype(BF16)


def _attn(q, k, vt, tq, unroll):
    L = q.shape[0]
    _, nk, _, tk = vt.shape
    return pl.pallas_call(
        functools.partial(_attn_kernel, unroll=unroll),
        grid=(N_HEADS, L // tq),
        in_specs=[pl.BlockSpec((tq, HEAD_SLAB), lambda h, i: (i, h)),
                  pl.BlockSpec((L, HEAD_SLAB), lambda h, i: (0, h)),
                  pl.BlockSpec((1, nk, HEAD_SLAB, tk), lambda h, i: (h, 0, 0, 0))],
        out_specs=pl.BlockSpec((HEAD_SLAB, tq), lambda h, i: (h, i)),
        out_shape=jax.ShapeDtypeStruct((QK_SLABS, L), BF16),
        compiler_params=_cparams("parallel", "parallel"),
        name="attn",
    )(q, k, vt)


def _merge_kernel(x_ref, yht_ref, ymt_ref, gate_ref, mod_ref, why_ref, wmla_ref, wmix_ref, g2_ref,
                  wrt_ref, brt_ref, x1_ref, h2_ref, comb_ref):
    d = D_MODEL
    a = _dot_tn(yht_ref[...], why_ref[...])
    b = _dot_tn(ymt_ref[...], wmla_ref[...])
    mix = gate_ref[:, 0:d].astype(F32) * a + gate_ref[:, d:2 * d].astype(F32) * b
    o = jnp.dot(mix.astype(BF16), wmix_ref[...], preferred_element_type=F32)
    gt1 = mod_ref[0:1, 2 * d:3 * d]
    sh2 = mod_ref[0:1, 3 * d:4 * d]
    sc2 = mod_ref[0:1, 4 * d:5 * d]
    x1 = x_ref[...] + gt1 * o
    x1_ref[...] = x1
    h2 = _rms(x1) * g2_ref[...] * (1.0 + sc2) + sh2
    h2_hi = h2.astype(BF16)
    h2_ref[...] = h2_hi
    h2_lo = (h2 - h2_hi.astype(F32)).astype(BF16)
    wrt = wrt_ref[...]
    w_hi = wrt.astype(BF16)
    w_lo = (wrt - w_hi.astype(F32)).astype(BF16)
    e = (jnp.dot(h2_hi, w_hi, preferred_element_type=F32)
         + (jnp.dot(h2_lo, w_hi, preferred_element_type=F32)
            + jnp.dot(h2_hi, w_lo, preferred_element_type=F32))) + brt_ref[...]
    lane = lax.broadcasted_iota(jnp.int32, e.shape, 1).astype(F32)
    gvalid = (lane >= N_EXPERTS) & (lane < N_EXPERTS + N_GROUPS)
    glm = jnp.where(gvalid, e, NEG_BIG)
    gmax = jnp.max(glm, axis=1, keepdims=True)
    gidx = jnp.min(jnp.where(glm == gmax, lane, float(LANES)), axis=1, keepdims=True) - float(N_EXPERTS)
    psum = jnp.sum(jnp.where(gvalid, jnp.exp(glm - gmax), 0.0), axis=1, keepdims=True)
    p_sel = 1.0 / psum
    lo = gidx * EXPERTS_PER_GROUP
    em = jnp.where((lane >= lo) & (lane < lo + EXPERTS_PER_GROUP), e, NEG_BIG)
    v1 = jnp.max(em, axis=1, keepdims=True)
    i1 = jnp.min(jnp.where(em == v1, lane, float(LANES)), axis=1, keepdims=True)
    em2 = jnp.where(lane == i1, NEG_BIG, em)
    v2 = jnp.max(em2, axis=1, keepdims=True)
    i2 = jnp.min(jnp.where(em2 == v2, lane, float(LANES)), axis=1, keepdims=True)
    t = jnp.exp(v2 - v1)
    w1 = p_sel / (1.0 + t)
    w2 = p_sel * t / (1.0 + t)
    comb_ref[...] = jnp.where(lane == i1, w1, 0.0) + jnp.where(lane == i2, w2, 0.0)


def _merge(x2, yht, ym, gates, mod, why, wmla, wmix, g2, wrt, brt, tm):
    L, d = x2.shape
    full = lambda a: pl.BlockSpec(a.shape, lambda i: (0,) * a.ndim)
    row = lambda w: pl.BlockSpec((tm, w), lambda i: (i, 0))
    return pl.pallas_call(
        _merge_kernel,
        grid=(L // tm,),
        in_specs=[row(d), pl.BlockSpec((HY_WIDTH, tm), lambda i: (0, i)),
                  pl.BlockSpec((QK_SLABS, tm), lambda i: (0, i)), row(2 * d),
                  full(mod), full(why), full(wmla), full(wmix), full(g2), full(wrt), full(brt)],
        out_specs=[row(d), row(d), row(LANES)],
        out_shape=[jax.ShapeDtypeStruct((L, d), F32),
                   jax.ShapeDtypeStruct((L, d), BF16),
                   jax.ShapeDtypeStruct((L, LANES), F32)],
        compiler_params=_cparams("parallel"),
        name="merge",
    )(x2, yht, ym, gates, mod, why, wmla, wmix, g2, wrt, brt)


def _moe_kernel(h_ref, comb_ref, wgu_ref, wd_ref, x1_ref, mod_ref, fg_ref, o_ref, acc_ref):
    d = D_MODEL
    e = pl.program_id(1)

    @pl.when(e == 0)
    def _():
        acc_ref[...] = jnp.zeros_like(acc_ref)

    h = h_ref[...]
    comb = comb_ref[...]
    lane = lax.broadcasted_iota(jnp.int32, comb.shape, 1)
    ge = wgu_ref.shape[0]
    parts = []
    for j in range(ge):
        gu = jnp.dot(h, wgu_ref[j], preferred_element_type=F32)
        g = gu[:, 0:D_EXPERT]
        a = g * jax.nn.sigmoid(g) * gu[:, D_EXPERT:2 * D_EXPERT]
        col = jnp.sum(jnp.where(lane == e * ge + j, comb, 0.0), axis=1, keepdims=True)
        parts.append((a * col).astype(BF16))
    acc_ref[...] += jnp.dot(jnp.concatenate(parts, axis=1), wd_ref[...], preferred_element_type=F32)

    @pl.when(e == pl.num_programs(1) - 1)
    def _():
        gt2 = mod_ref[0:1, 5 * d:6 * d]
        x2 = x1_ref[...] + gt2 * acc_ref[...]
        o_ref[...] = _rms(x2) * fg_ref[...]


def _moe(h2, comb, wgu, wd, x1, mod, fg, tm, ge):
    L, d = x1.shape
    ne = wgu.shape[0]
    return pl.pallas_call(
        _moe_kernel,
        grid=(L // tm, ne // ge),
        in_specs=[pl.BlockSpec((tm, d), lambda i, e: (i, 0)),
                  pl.BlockSpec((tm, LANES), lambda i, e: (i, 0)),
                  pl.BlockSpec((ge, d, 2 * D_EXPERT), lambda i, e: (e, 0, 0)),
                  pl.BlockSpec((ge * D_EXPERT, d), lambda i, e: (e, 0)),
                  pl.BlockSpec((tm, d), lambda i, e: (i, 0)),
                  pl.BlockSpec(mod.shape, lambda i, e: (0, 0)),
                  pl.BlockSpec(fg.shape, lambda i, e: (0, 0))],
        out_specs=pl.BlockSpec((tm, d), lambda i, e: (i, 0)),
        out_shape=jax.ShapeDtypeStruct((L, d), F32),
        scratch_shapes=[pltpu.VMEM((tm, d), F32)],
        compiler_params=_cparams("parallel", "arbitrary"),
        name="moe",
    )(h2, comb, wgu, wd, x1, mod, fg)


def _pad_heads(w, width):
    lead = w.shape[:-1]
    w = w.reshape(lead + (N_HEADS, width))
    w = jnp.pad(w, [(0, 0)] * len(lead) + [(0, 0), (0, HEAD_SLAB - width)])
    return w.reshape(lead + (N_HEADS * HEAD_SLAB,))


@functools.lru_cache(maxsize=None)
def _rope_slab_tables(L):
    half = QK_ROPE // 2
    pos = np.arange(L, dtype=np.float64)
    inv = ROPE_THETA ** (-np.arange(0, QK_ROPE, 2, dtype=np.float64) / QK_ROPE)
    ang = pos[:, None] * inv[None, :]
    cos, sin = np.cos(ang), np.sin(ang)
    z = lambda w: np.zeros((L, w))
    cos_t = np.concatenate([np.ones((L, QK_NOPE)), cos, cos, z(HEAD_SLAB - QK_NOPE - QK_ROPE)], axis=1)
    sin_a = np.concatenate([z(QK_NOPE), -sin, z(HEAD_SLAB - QK_NOPE - half)], axis=1)
    sin_b = np.concatenate([z(QK_NOPE + half), sin, z(HEAD_SLAB - QK_NOPE - QK_ROPE)], axis=1)
    return cos_t.astype(np.float32), sin_a.astype(np.float32), sin_b.astype(np.float32)


@functools.lru_cache(maxsize=None)
def _filter_features(L):
    t = np.linspace(0.0, 1.0, L)[None, :]
    t_r = np.arange(L, dtype=np.float64)[None, :]
    bands = np.linspace(1e-4, FILTER_BANDS - 1, FILTER_BANDS)[:, None]
    ang = 2.0 * math.pi * bands * t_r / L
    z = np.concatenate([t, np.cos(ang), -np.sin(ang)], axis=0)
    return np.pad(z, ((0, FILTER_HIDDEN - FILTER_EMB), (0, 0))).astype(np.float32)


@functools.lru_cache(maxsize=None)
def _dft_tables(L):
    n = 2 * L
    n2 = DFT_N2
    n1 = n // n2
    n1h = n1 // 2
    k1n = n1h + 1
    k1p = -(-k1n // SUBLANES) * SUBLANES
    two_pi = 2.0 * math.pi
    k1 = np.arange(k1p, dtype=np.int64)[:, None]
    live = (k1 < k1n).astype(np.float64)
    m1 = np.arange(n1h, dtype=np.int64)[None, :]
    ang1 = two_pi * ((k1 * m1) % n1) / n1
    f1m = np.concatenate([np.cos(ang1) * live, -np.sin(ang1) * live], axis=0).astype(BF16)
    m2 = np.arange(n2, dtype=np.int64)[None, :]
    ang2 = two_pi * ((k1 * m2) % n) / n
    tr = (np.cos(ang2) * live).astype(np.float32)
    ti = (-np.sin(ang2) * live).astype(np.float32)
    ang3 = two_pi * ((m2.T * m2) % n2) / n2
    f2r, f2i = np.cos(ang3), -np.sin(ang3)
    f2 = np.block([[f2r, f2i], [-f2i, f2r]]).astype(BF16)
    f2inv = np.block([[f2r, -f2i], [f2i, f2r]]).astype(BF16)
    o1 = np.arange(n1h, dtype=np.int64)[:, None]
    q1 = np.arange(n1h, dtype=np.int64)[None, :]
    ang4 = two_pi * ((o1 * q1) % n1) / n1
    wgt = np.where(q1 == 0, 1.0, 2.0)
    cmat = np.concatenate([wgt * np.cos(ang4), -wgt * np.sin(ang4)], axis=1).astype(BF16)
    sgn = np.broadcast_to(np.where(o1 % 2 == 0, 1.0, -1.0), (n1h, n2)).astype(np.float32)
    return (f1m, tr, ti, f2, f2inv, cmat, sgn), n1h


def _pick(n, pref):
    t = pref
    while n % t:
        t //= 2
    return t


def kernel(x, c, ada_w, ada_b, norm_mix_g, w_in, hy_conv_w, hy_conv_b, filt_w1, filt_b1, filt_freq1, filt_w2, filt_b2, filt_freq2, filt_w3, hy_bias, q_norm_g, kv_norm_g, w_uq, w_uk, w_uv, w_hy_out, w_mla_out, w_mix_out, norm_ffn_g, w_group, b_group, w_router, b_router, w_gate, w_up, w_down, final_norm_g):
    B, L, d = x.shape
    assert B == 1 and d == D_MODEL and ada_w.shape[0] == 1 and L % (2 * DFT_N2) == 0
    x2 = x.reshape(L, d)
    row = lambda v: v.reshape(1, -1)
    col = lambda v: v.reshape(-1, 1)

    mod = _ada(jnp.pad(c, ((0, 8 - B), (0, 0))), ada_w[0], row(ada_b[0]))

    wi = w_in[0]
    kr_slab = jnp.pad(wi[:, OFF_KR:OFF_GATE], ((0, 0), (QK_NOPE, HEAD_SLAB - QK_NOPE - QK_ROPE)))
    wcat = jnp.concatenate([wi[:, OFF_Q:OFF_KV], wi[:, OFF_KV:OFF_KR], wi[:, OFF_GATE:], kr_slab],
                           axis=1).astype(BF16)
    whyt = wi[:, 0:OFF_Q].T.astype(BF16)
    wuq = _pad_heads(w_uq[0], QK_NOPE + QK_ROPE).astype(BF16)
    wuk = _pad_heads(w_uk[0].reshape(KV_LORA, N_HEADS * QK_NOPE), QK_NOPE).astype(BF16)
    wuvt = _pad_heads(w_uv[0].reshape(KV_LORA, N_HEADS * V_HEAD), V_HEAD).T.astype(BF16)
    tkv = _pick(L, 1024)
    cos_t, sin_a, sin_b = _rope_slab_tables(L)

    hypt, gates, q, k, vt = _inproj(x2, mod, row(norm_mix_g[0]), wcat, whyt, row(q_norm_g[0]),
                                    row(kv_norm_g[0]), wuq, wuk, wuvt, cos_t, sin_a, sin_b,
                                    _pick(L, 512), tkv)

    zt = _filter_features(L)
    w1t = jnp.pad(filt_w1[0], ((0, FILTER_HIDDEN - FILTER_EMB), (0, 0))).T
    dl = np.abs(np.linspace(MIN_DECAY, MAX_DECAY, HY_WIDTH))[:, None].astype(np.float32)
    ht, nrm = _filt(zt, w1t, col(filt_b1[0]), col(filt_freq1[0]), filt_w2[0].T, col(filt_b2[0]),
                    col(filt_freq2[0]), filt_w3[0].T.astype(BF16), dl, _pick(L, 1024))
    tabs, n1h = _dft_tables(L)
    cw, cbias = hy_conv_w[0], hy_conv_b[0]
    w = HY_WIDTH
    taps = lambda s: [cw[0, s * w:(s + 1) * w], cw[1, s * w:(s + 1) * w], cw[2, s * w:(s + 1) * w],
                      cbias[s * w:(s + 1) * w]]
    prm = jnp.stack(taps(0) + taps(1) + taps(2) + [hy_bias[0], nrm[:, 0]]
                    + [jnp.zeros((w,), F32)] * (HY_PARAM_ROWS - 14), axis=1)
    prm = jnp.broadcast_to(prm[:, :, None], (w, HY_PARAM_ROWS, DFT_N2))
    yht = _hyena(hypt.reshape(3 * w, n1h, DFT_N2), ht.reshape(2 * w, n1h, DFT_N2), prm, tabs,
                 _pick(w, 8), 1.0 / (2 * L)).reshape(w, L)

    ym = _attn(q, k, vt, _pick(L, 1024), 4)

    wmla = jnp.pad(w_mla_out[0].reshape(N_HEADS, V_HEAD, d),
                   ((0, 0), (0, HEAD_SLAB - V_HEAD), (0, 0))).reshape(QK_SLABS, d).astype(BF16)
    wrt = jnp.pad(jnp.concatenate([w_router[0], w_group[0]], axis=1),
                  ((0, 0), (0, LANES - N_EXPERTS - N_GROUPS)))
    brt = jnp.pad(jnp.concatenate([b_router[0], b_group[0]]), (0, LANES - N_EXPERTS - N_GROUPS))[None, :]
    x1, h2, comb = _merge(x2, yht, ym, gates, mod, w_hy_out[0].astype(BF16), wmla,
                          w_mix_out[0].astype(BF16), row(norm_ffn_g[0]), wrt, brt, _pick(L, 512))

    wgu = jnp.concatenate([w_gate[0], w_up[0]], axis=2).astype(BF16)
    wd = w_down[0].reshape(N_EXPERTS * D_EXPERT, d).astype(BF16)
    out = _moe(h2, comb, wgu, wd, x1, mod, row(final_norm_g), _pick(L, 1024), 4)
    return out.reshape(B, L, d)
```

```python
import functools
import math

import numpy as np
import jax
import jax.numpy as jnp
from jax import lax
from jax.experimental import pallas as pl
from jax.experimental.pallas import tpu as pltpu

F32 = jnp.float32
BF16 = jnp.bfloat16
HIGHEST = lax.Precision.HIGHEST

D_MODEL = 1024
EPS = 1e-6
HY_WIDTH = 512
FILTER_BANDS = 16
FILTER_EMB = 1 + 2 * FILTER_BANDS
FILTER_HIDDEN = 64
FILTER_DECAY_TARGET = 1e-2
FAST_DECAY_PCT = 0.3
SLOW_DECAY_PCT = 1.5
FILTER_SHIFT = 0.05
MIN_DECAY = math.log(FILTER_DECAY_TARGET) / SLOW_DECAY_PCT
MAX_DECAY = math.log(FILTER_DECAY_TARGET) / FAST_DECAY_PCT
N_HEADS = 8
QK_NOPE = 64
QK_ROPE = 32
V_HEAD = 64
Q_LORA = 384
KV_LORA = 256
ROPE_THETA = 10000.0
MLA_WIDTH = N_HEADS * V_HEAD
OFF_Q = 3 * HY_WIDTH
OFF_KV = OFF_Q + Q_LORA
OFF_KR = OFF_KV + KV_LORA
OFF_GATE = OFF_KR + QK_ROPE
N_GROUPS = 4
EXPERTS_PER_GROUP = 8
N_EXPERTS = N_GROUPS * EXPERTS_PER_GROUP
D_EXPERT = 256

LANES = 128
SUBLANES = 8
HEAD_SLAB = LANES
QK_SLABS = N_HEADS * HEAD_SLAB
ONES_LANE = V_HEAD
DFT_N2 = LANES
HY_PARAM_ROWS = 16
NEG_BIG = -1e30
VMEM_LIMIT = 56 * 1024 * 1024
MAX_STATIC_SHIFT = 60.0
SHIFT_SLACK = 1.002
SHIFT_EPS = 1e-3

C_Q = 0
C_KV = C_Q + Q_LORA
C_GATE = C_KV + KV_LORA
C_KR = C_GATE + 2 * D_MODEL
C_END = C_KR + HEAD_SLAB


def _cparams(*sem):
    return pltpu.CompilerParams(dimension_semantics=sem, vmem_limit_bytes=VMEM_LIMIT)


def _rms(x):
    return x * lax.rsqrt(jnp.mean(x * x, axis=-1, keepdims=True) + EPS)


def _dot_nt(a, b):
    return lax.dot_general(a, b, (((1,), (1,)), ((), ())), preferred_element_type=F32)


def _dot_tn(a, b):
    return lax.dot_general(a, b, (((0,), (0,)), ((), ())), preferred_element_type=F32)


def _ada_kernel(c_ref, w_ref, b_ref, o_ref):
    c = c_ref[...]
    ca = c * jax.nn.sigmoid(c)
    o_ref[...] = jnp.dot(ca, w_ref[...], preferred_element_type=F32, precision=HIGHEST) + b_ref[...]


def _ada(c8, w, b):
    d, n = w.shape
    tn = 1024
    return pl.pallas_call(
        _ada_kernel,
        grid=(n // tn,),
        in_specs=[pl.BlockSpec((8, d), lambda j: (0, 0)),
                  pl.BlockSpec((d, tn), lambda j: (0, j)),
                  pl.BlockSpec((1, tn), lambda j: (0, j))],
        out_specs=pl.BlockSpec((8, tn), lambda j: (0, j)),
        out_shape=jax.ShapeDtypeStruct((8, n), F32),
        compiler_params=_cparams("parallel"),
        name="ada",
    )(c8, w, b)


def _rope_slab(z, cos_t, sin_a, sin_b):
    return z * cos_t + pltpu.roll(z, HEAD_SLAB - QK_ROPE // 2, 1) * sin_a + pltpu.roll(z, QK_ROPE // 2, 1) * sin_b


def _inproj_kernel(x_ref, mod_ref, g_ref, wcat_ref, whyt_ref, gq_ref, gkv_ref, wuq_ref, wuk_ref, wuvt_ref,
                   cos_ref, sina_ref, sinb_ref,
                   hypt_ref, gate_ref, q_ref, k_ref, vt_ref, kn2_ref, *, qscale):
    d = D_MODEL
    x = x_ref[...]
    sh1 = mod_ref[0:1, 0:d]
    sc1 = mod_ref[0:1, d:2 * d]
    hb = (_rms(x) * g_ref[...] * (1.0 + sc1) + sh1).astype(BF16)
    hypt_ref[...] = _dot_nt(whyt_ref[...], hb).astype(BF16)
    proj = jnp.dot(hb, wcat_ref[...], preferred_element_type=F32)
    gate_ref[...] = jax.nn.sigmoid(proj[:, C_GATE:C_KR]).astype(BF16)
    cq = _rms(proj[:, C_Q:C_KV]) * gq_ref[...]
    ckv = (_rms(proj[:, C_KV:C_GATE]) * gkv_ref[...]).astype(BF16)
    q = jnp.dot(cq.astype(BF16), wuq_ref[...], preferred_element_type=F32)
    kn = jnp.dot(ckv, wuk_ref[...], preferred_element_type=F32)
    vt = _dot_nt(wuvt_ref[...], ckv)
    cos_t = cos_ref[...]
    sin_a = sina_ref[...]
    sin_b = sinb_ref[...]
    kr = _rope_slab(proj[:, C_KR:C_END], cos_t, sin_a, sin_b)
    srow = lax.broadcasted_iota(jnp.int32, (HEAD_SLAB, 1), 0)
    ones_row = jnp.where(srow == ONES_LANE, 1.0, 0.0).astype(F32)
    kn2 = []
    for hd in range(N_HEADS):
        sl = slice(hd * HEAD_SLAB, (hd + 1) * HEAD_SLAB)
        q_ref[:, sl] = (_rope_slab(q[:, sl], cos_t, sin_a, sin_b) * qscale).astype(BF16)
        kb = (kn[:, sl] + kr).astype(BF16)
        k_ref[:, sl] = kb
        vt_ref[hd, 0] = (vt[sl] + ones_row).astype(BF16)
        kf = kb.astype(F32)
        big = jnp.max(jnp.sum(kf * kf, axis=1, keepdims=True), axis=0, keepdims=True)
        kn2.append(jnp.broadcast_to(big, (1, HEAD_SLAB)))
    kn2_ref[0] = jnp.concatenate(kn2, axis=0)


def _inproj(x2, mod, g1, wcat, whyt, gq, gkv, wuq, wuk, wuvt, cos_t, sin_a, sin_b, tm, tkv):
    L, d = x2.shape
    qscale = (QK_NOPE + QK_ROPE) ** -0.5 * math.log2(math.e)
    full = lambda a: pl.BlockSpec(a.shape, lambda i: (0,) * a.ndim)
    row = lambda w: pl.BlockSpec((tm, w), lambda i: (i, 0))
    per = tkv // tm
    return pl.pallas_call(
        functools.partial(_inproj_kernel, qscale=qscale),
        grid=(L // tm,),
        in_specs=[row(d), full(mod), full(g1), full(wcat), full(whyt), full(gq), full(gkv), full(wuq),
                  full(wuk), full(wuvt), row(HEAD_SLAB), row(HEAD_SLAB), row(HEAD_SLAB)],
        out_specs=[pl.BlockSpec((3 * HY_WIDTH, tm), lambda i: (0, i)),
                   row(2 * D_MODEL), row(QK_SLABS), row(QK_SLABS),
                   pl.BlockSpec((N_HEADS, 1, HEAD_SLAB, tm), lambda i: (0, i // per, 0, i % per)),
                   pl.BlockSpec((1, N_HEADS, HEAD_SLAB), lambda i: (i, 0, 0))],
        out_shape=[jax.ShapeDtypeStruct((3 * HY_WIDTH, L), BF16),
                   jax.ShapeDtypeStruct((L, 2 * D_MODEL), BF16),
                   jax.ShapeDtypeStruct((L, QK_SLABS), BF16),
                   jax.ShapeDtypeStruct((L, QK_SLABS), BF16),
                   jax.ShapeDtypeStruct((N_HEADS, L // tkv, HEAD_SLAB, tkv), BF16),
                   jax.ShapeDtypeStruct((L // tm, N_HEADS, HEAD_SLAB), F32)],
        compiler_params=_cparams("parallel"),
        name="inproj",
    )(x2, mod, g1, wcat, whyt, gq, gkv, wuq, wuk, wuvt, cos_t, sin_a, sin_b)


def _filt_kernel(z_ref, w1_ref, b1_ref, f1_ref, w2_ref, b2_ref, f2_ref, w3_ref, dl_ref,
                 h_ref, nrm_ref):
    i = pl.program_id(0)
    z = z_ref[...]
    tm = z.shape[1]
    h = jnp.sin(f1_ref[...] * (jnp.dot(w1_ref[...], z, preferred_element_type=F32, precision=HIGHEST)
                               + b1_ref[...]))
    h = jnp.sin(f2_ref[...] * (jnp.dot(w2_ref[...], h, preferred_element_type=F32, precision=HIGHEST)
                               + b2_ref[...]))
    h3 = jnp.dot(w3_ref[...], h.astype(BF16), preferred_element_type=F32)
    t = z[0:1, :]
    window = jnp.exp(-dl_ref[...] * t) + FILTER_SHIFT
    hf = h3[0:HY_WIDTH] * window
    hb = h3[HY_WIDTH:2 * HY_WIDTH] * window
    col = lax.broadcasted_iota(jnp.int32, (1, tm), 1)
    hb = jnp.where((col == 0) & (i == 0), 0.0, hb)
    h_ref[0:HY_WIDTH, :] = hf.astype(BF16)
    h_ref[HY_WIDTH:2 * HY_WIDTH, :] = hb.astype(BF16)
    part = jnp.sum(jnp.abs(hf) + jnp.abs(hb), axis=1, keepdims=True)

    @pl.when(i == 0)
    def _():
        nrm_ref[...] = jnp.zeros_like(nrm_ref)

    nrm_ref[...] += part


def _filt(zt, w1t, b1, f1, w2t, b2, f2, w3t, dl, tm):
    L = zt.shape[1]
    full = lambda a: pl.BlockSpec(a.shape, lambda i: (0,) * a.ndim)
    return pl.pallas_call(
        _filt_kernel,
        grid=(L // tm,),
        in_specs=[pl.BlockSpec((zt.shape[0], tm), lambda i: (0, i)), full(w1t), full(b1), full(f1),
                  full(w2t), full(b2), full(f2), full(w3t), full(dl)],
        out_specs=[pl.BlockSpec((2 * HY_WIDTH, tm), lambda i: (0, i)),
                   pl.BlockSpec((HY_WIDTH, 1), lambda i: (0, 0))],
        out_shape=[jax.ShapeDtypeStruct((2 * HY_WIDTH, L), BF16),
                   jax.ShapeDtypeStruct((HY_WIDTH, 1), F32)],
        compiler_params=_cparams("arbitrary"),
        name="filt",
    )(zt, w1t, b1, f1, w2t, b2, f2, w3t, dl)


def _hyena_kernel(x0_ref, x1_ref, v_ref, hf_ref, hb_ref, prm_ref, f1_ref, tr_ref, ti_ref, f2_ref,
                  f2i_ref, cm_ref, sg_ref, o_ref, *, inv_n):
    cb, n1h, w = x0_ref.shape
    k1p = tr_ref.shape[0]
    lane = lax.broadcasted_iota(jnp.int32, (n1h, w), 1)
    rowi = lax.broadcasted_iota(jnp.int32, (n1h, w), 0)
    tr = tr_ref[...]
    ti = ti_ref[...]

    def conv(x, p, j):
        r = pltpu.roll(x, 1, 1)
        up = jnp.where(lane == 0, jnp.where(rowi == 0, 0.0, pltpu.roll(r, 1, 0)), r)
        l = pltpu.roll(x, w - 1, 1)
        dn = jnp.where(lane == w - 1, jnp.where(rowi == n1h - 1, 0.0, pltpu.roll(l, n1h - 1, 0)), l)
        return up * p[j:j + 1] + x * p[j + 1:j + 2] + dn * p[j + 2:j + 3] + p[j + 3:j + 4]

    def twiddle(a, s):
        ar = a[0:k1p, s * w:(s + 1) * w]
        ai = a[k1p:2 * k1p, s * w:(s + 1) * w]
        return jnp.concatenate([ar * tr - ai * ti, ar * ti + ai * tr], axis=1).astype(BF16)

    prm = [prm_ref[c] for c in range(cb)]
    x0s, us, sigs = [], [], []
    for c in range(cb):
        p = prm[c]
        x0s.append(conv(x0_ref[c].astype(F32), p, 0))
        u = conv(v_ref[c].astype(F32), p, 8) * conv(x1_ref[c].astype(F32), p, 4)
        us.append(u)
        sigs += [u.astype(BF16), hf_ref[c], hb_ref[c]]
    a = jnp.dot(f1_ref[...], jnp.concatenate(sigs, axis=1), preferred_element_type=F32)
    b = jnp.concatenate([twiddle(a, s) for s in range(3 * cb)], axis=0)
    s = jnp.dot(b, f2_ref[...], preferred_element_type=F32)
    ys = []
    for c in range(cb):
        su = s[(3 * c) * k1p:(3 * c + 1) * k1p]
        sf = s[(3 * c + 1) * k1p:(3 * c + 2) * k1p]
        sb = s[(3 * c + 2) * k1p:(3 * c + 3) * k1p]
        inv = 1.0 / prm[c][13:14]
        gr = (sf[:, 0:w] + sb[:, 0:w]) * inv
        gi = (sf[:, w:2 * w] - sb[:, w:2 * w]) * inv
        ur, ui = su[:, 0:w], su[:, w:2 * w]
        ys.append(jnp.concatenate([ur * gr - ui * gi, ur * gi + ui * gr], axis=1).astype(BF16))
    zt = jnp.dot(jnp.concatenate(ys, axis=0), f2i_ref[...], preferred_element_type=F32)
    zss, nyq = [], []
    for c in range(cb):
        ztr = zt[c * k1p:(c + 1) * k1p, 0:w]
        zti = zt[c * k1p:(c + 1) * k1p, w:2 * w]
        zr = ztr * tr + zti * ti
        zi = zti * tr - ztr * ti
        zss.append(jnp.concatenate([zr[0:n1h], zi[0:n1h]], axis=0).astype(BF16))
        nyq.append(zr[n1h:n1h + 1])
    yy = jnp.dot(cm_ref[...], jnp.concatenate(zss, axis=1), preferred_element_type=F32)
    for c in range(cb):
        conv_out = (yy[:, c * w:(c + 1) * w] + sg_ref[...] * nyq[c]) * inv_n
        o_ref[c] = (x0s[c] * (conv_out + prm[c][12:13] * us[c])).astype(BF16)


def _hyena(hyp3, h3, prm, tabs, cb, inv_n):
    c3, n1h, w = hyp3.shape
    c = c3 // 3
    nb = c // cb
    f1m, tr, ti, f2, f2i, cmat, sgn = tabs
    full = lambda a: pl.BlockSpec(a.shape, lambda i: (0,) * a.ndim)
    blk = lambda off: pl.BlockSpec((cb, n1h, w), lambda i: (off * nb + i, 0, 0))
    return pl.pallas_call(
        functools.partial(_hyena_kernel, inv_n=inv_n),
        grid=(nb,),
        in_specs=[blk(0), blk(1), blk(2), blk(0), blk(1),
                  pl.BlockSpec((cb, HY_PARAM_ROWS, w), lambda i: (i, 0, 0)),
                  full(f1m), full(tr), full(ti), full(f2), full(f2i), full(cmat), full(sgn)],
        out_specs=pl.BlockSpec((cb, n1h, w), lambda i: (i, 0, 0)),
        out_shape=jax.ShapeDtypeStruct((c, n1h, w), BF16),
        compiler_params=_cparams("parallel"),
        name="hyena",
    )(hyp3, hyp3, hyp3, h3, h3, prm, f1m, tr, ti, f2, f2i, cmat, sgn)


def _attn_kernel(q_ref, k_ref, vt_ref, kn2_ref, o_ref, *, unroll):
    tq = q_ref.shape[0]
    nk, _, tk = vt_ref.shape[1:]
    q = q_ref[...]

    def finish(acc):
        o_ref[...] = (acc / acc[ONES_LANE:ONES_LANE + 1, :]).astype(BF16)

    qf = q.astype(F32)
    qn2 = lax.dot_general(jnp.ones((SUBLANES, HEAD_SLAB), F32), qf * qf, (((1,), (1,)), ((), ())),
                          precision=HIGHEST, preferred_element_type=F32)[0:1]
    bound = jnp.sqrt(qn2 * kn2_ref[0][0:1, 0:1]) * SHIFT_SLACK + SHIFT_EPS
    static_shift_ok = jnp.max(bound) <= MAX_STATIC_SHIFT

    @pl.when(static_shift_ok)
    def _():
        def body(j, acc):
            off = pl.multiple_of(j * tk, tk)
            st = _dot_nt(k_ref[pl.ds(off, tk), :], q)
            p = jnp.exp2(st - bound).astype(BF16)
            return acc + jnp.dot(vt_ref[0, j], p, preferred_element_type=F32)

        finish(lax.fori_loop(0, nk, body, jnp.zeros((HEAD_SLAB, tq), F32), unroll=unroll))

    @pl.when(jnp.logical_not(static_shift_ok))
    def _():
        def body(j, carry):
            m, acc = carry
            off = pl.multiple_of(j * tk, tk)
            st = _dot_nt(k_ref[pl.ds(off, tk), :], q)
            m_new = jnp.maximum(m, jnp.max(st, axis=0, keepdims=True))
            alpha = jnp.exp2(m - m_new)
            p = jnp.exp2(st - m_new)
            acc = alpha * acc + jnp.dot(vt_ref[0, j], p.astype(BF16), preferred_element_type=F32)
            return m_new, acc

        m0 = jnp.full((1, tq), NEG_BIG, F32)
        acc0 = jnp.zeros((HEAD_SLAB, tq), F32)
        finish(lax.fori_loop(0, nk, body, (m0, acc0), unroll=unroll)[1])


def _attn(q, k, vt, kn2, tq, unroll):
    L = q.shape[0]
    _, nk, _, tk = vt.shape
    return pl.pallas_call(
        functools.partial(_attn_kernel, unroll=unroll),
        grid=(N_HEADS, L // tq),
        in_specs=[pl.BlockSpec((tq, HEAD_SLAB), lambda h, i: (i, h)),
                  pl.BlockSpec((L, HEAD_SLAB), lambda h, i: (0, h)),
                  pl.BlockSpec((1, nk, HEAD_SLAB, tk), lambda h, i: (h, 0, 0, 0)),
                  pl.BlockSpec((1, SUBLANES, LANES), lambda h, i: (h, 0, 0))],
        out_specs=pl.BlockSpec((HEAD_SLAB, tq), lambda h, i: (h, i)),
        out_shape=jax.ShapeDtypeStruct((QK_SLABS, L), BF16),
        compiler_params=_cparams("parallel", "parallel"),
        name="attn",
    )(q, k, vt, kn2)


def _merge_kernel(x_ref, yht_ref, ymt_ref, gate_ref, mod_ref, why_ref, wmla_ref, wmix_ref, g2_ref,
                  wrt_ref, brt_ref, x1_ref, h2_ref, comb_ref):
    d = D_MODEL
    a = _dot_tn(yht_ref[...], why_ref[...])
    b = _dot_tn(ymt_ref[...], wmla_ref[...])
    mix = gate_ref[:, 0:d].astype(F32) * a + gate_ref[:, d:2 * d].astype(F32) * b
    o = jnp.dot(mix.astype(BF16), wmix_ref[...], preferred_element_type=F32)
    gt1 = mod_ref[0:1, 2 * d:3 * d]
    sh2 = mod_ref[0:1, 3 * d:4 * d]
    sc2 = mod_ref[0:1, 4 * d:5 * d]
    x1 = x_ref[...] + gt1 * o
    x1_ref[...] = x1
    h2 = _rms(x1) * g2_ref[...] * (1.0 + sc2) + sh2
    h2_hi = h2.astype(BF16)
    h2_ref[...] = h2_hi
    h2_lo = (h2 - h2_hi.astype(F32)).astype(BF16)
    wrt = wrt_ref[...]
    w_hi = wrt.astype(BF16)
    w_lo = (wrt - w_hi.astype(F32)).astype(BF16)
    e = (jnp.dot(h2_hi, w_hi, preferred_element_type=F32)
         + (jnp.dot(h2_lo, w_hi, preferred_element_type=F32)
            + jnp.dot(h2_hi, w_lo, preferred_element_type=F32))) + brt_ref[...]
    lane = lax.broadcasted_iota(jnp.int32, e.shape, 1).astype(F32)
    gvalid = (lane >= N_EXPERTS) & (lane < N_EXPERTS + N_GROUPS)
    glm = jnp.where(gvalid, e, NEG_BIG)
    gmax = jnp.max(glm, axis=1, keepdims=True)
    gidx = jnp.min(jnp.where(glm == gmax, lane, float(LANES)), axis=1, keepdims=True) - float(N_EXPERTS)
    psum = jnp.sum(jnp.where(gvalid, jnp.exp(glm - gmax), 0.0), axis=1, keepdims=True)
    p_sel = 1.0 / psum
    lo = gidx * EXPERTS_PER_GROUP
    em = jnp.where((lane >= lo) & (lane < lo + EXPERTS_PER_GROUP), e, NEG_BIG)
    v1 = jnp.max(em, axis=1, keepdims=True)
    i1 = jnp.min(jnp.where(em == v1, lane, float(LANES)), axis=1, keepdims=True)
    em2 = jnp.where(lane == i1, NEG_BIG, em)
    v2 = jnp.max(em2, axis=1, keepdims=True)
    i2 = jnp.min(jnp.where(em2 == v2, lane, float(LANES)), axis=1, keepdims=True)
    t = jnp.exp(v2 - v1)
    w1 = p_sel / (1.0 + t)
    w2 = p_sel * t / (1.0 + t)
    comb_ref[...] = jnp.where(lane == i1, w1, 0.0) + jnp.where(lane == i2, w2, 0.0)


def _merge(x2, yht, ym, gates, mod, why, wmla, wmix, g2, wrt, brt, tm):
    L, d = x2.shape
    full = lambda a: pl.BlockSpec(a.shape, lambda i: (0,) * a.ndim)
    row = lambda w: pl.BlockSpec((tm, w), lambda i: (i, 0))
    return pl.pallas_call(
        _merge_kernel,
        grid=(L // tm,),
        in_specs=[row(d), pl.BlockSpec((HY_WIDTH, tm), lambda i: (0, i)),
                  pl.BlockSpec((QK_SLABS, tm), lambda i: (0, i)), row(2 * d),
                  full(mod), full(why), full(wmla), full(wmix), full(g2), full(wrt), full(brt)],
        out_specs=[row(d), row(d), row(LANES)],
        out_shape=[jax.ShapeDtypeStruct((L, d), F32),
                   jax.ShapeDtypeStruct((L, d), BF16),
                   jax.ShapeDtypeStruct((L, LANES), F32)],
        compiler_params=_cparams("parallel"),
        name="merge",
    )(x2, yht, ym, gates, mod, why, wmla, wmix, g2, wrt, brt)


def _moe_kernel(h_ref, comb_ref, wgu_ref, wd_ref, x1_ref, mod_ref, fg_ref, o_ref, acc_ref):
    d = D_MODEL
    e = pl.program_id(1)

    @pl.when(e == 0)
    def _():
        acc_ref[...] = jnp.zeros_like(acc_ref)

    h = h_ref[...]
    comb = comb_ref[...]
    lane = lax.broadcasted_iota(jnp.int32, comb.shape, 1)
    ge = wgu_ref.shape[0]
    parts = []
    for j in range(ge):
        gu = jnp.dot(h, wgu_ref[j], preferred_element_type=F32)
        g = gu[:, 0:D_EXPERT]
        a = g * jax.nn.sigmoid(g) * gu[:, D_EXPERT:2 * D_EXPERT]
        col = jnp.sum(jnp.where(lane == e * ge + j, comb, 0.0), axis=1, keepdims=True)
        parts.append((a * col).astype(BF16))
    acc_ref[...] += jnp.dot(jnp.concatenate(parts, axis=1), wd_ref[...], preferred_element_type=F32)

    @pl.when(e == pl.num_programs(1) - 1)
    def _():
        gt2 = mod_ref[0:1, 5 * d:6 * d]
        x2 = x1_ref[...] + gt2 * acc_ref[...]
        o_ref[...] = _rms(x2) * fg_ref[...]


def _moe(h2, comb, wgu, wd, x1, mod, fg, tm, ge):
    L, d = x1.shape
    ne = wgu.shape[0]
    return pl.pallas_call(
        _moe_kernel,
        grid=(L // tm, ne // ge),
        in_specs=[pl.BlockSpec((tm, d), lambda i, e: (i, 0)),
                  pl.BlockSpec((tm, LANES), lambda i, e: (i, 0)),
                  pl.BlockSpec((ge, d, 2 * D_EXPERT), lambda i, e: (e, 0, 0)),
                  pl.BlockSpec((ge * D_EXPERT, d), lambda i, e: (e, 0)),
                  pl.BlockSpec((tm, d), lambda i, e: (i, 0)),
                  pl.BlockSpec(mod.shape, lambda i, e: (0, 0)),
                  pl.BlockSpec(fg.shape, lambda i, e: (0, 0))],
        out_specs=pl.BlockSpec((tm, d), lambda i, e: (i, 0)),
        out_shape=jax.ShapeDtypeStruct((L, d), F32),
        scratch_shapes=[pltpu.VMEM((tm, d), F32)],
        compiler_params=_cparams("parallel", "arbitrary"),
        name="moe",
    )(h2, comb, wgu, wd, x1, mod, fg)


def _pad_heads(w, width):
    lead = w.shape[:-1]
    w = w.reshape(lead + (N_HEADS, width))
    w = jnp.pad(w, [(0, 0)] * len(lead) + [(0, 0), (0, HEAD_SLAB - width)])
    return w.reshape(lead + (N_HEADS * HEAD_SLAB,))


@functools.lru_cache(maxsize=None)
def _rope_slab_tables(L):
    half = QK_ROPE // 2
    pos = np.arange(L, dtype=np.float64)
    inv = ROPE_THETA ** (-np.arange(0, QK_ROPE, 2, dtype=np.float64) / QK_ROPE)
    ang = pos[:, None] * inv[None, :]
    cos, sin = np.cos(ang), np.sin(ang)
    z = lambda w: np.zeros((L, w))
    cos_t = np.concatenate([np.ones((L, QK_NOPE)), cos, cos, z(HEAD_SLAB - QK_NOPE - QK_ROPE)], axis=1)
    sin_a = np.concatenate([z(QK_NOPE), -sin, z(HEAD_SLAB - QK_NOPE - half)], axis=1)
    sin_b = np.concatenate([z(QK_NOPE + half), sin, z(HEAD_SLAB - QK_NOPE - QK_ROPE)], axis=1)
    return cos_t.astype(np.float32), sin_a.astype(np.float32), sin_b.astype(np.float32)


@functools.lru_cache(maxsize=None)
def _filter_features(L):
    t = np.linspace(0.0, 1.0, L)[None, :]
    t_r = np.arange(L, dtype=np.float64)[None, :]
    bands = np.linspace(1e-4, FILTER_BANDS - 1, FILTER_BANDS)[:, None]
    ang = 2.0 * math.pi * bands * t_r / L
    z = np.concatenate([t, np.cos(ang), -np.sin(ang)], axis=0)
    return np.pad(z, ((0, FILTER_HIDDEN - FILTER_EMB), (0, 0))).astype(np.float32)


@functools.lru_cache(maxsize=None)
def _dft_tables(L):
    n = 2 * L
    n2 = DFT_N2
    n1 = n // n2
    n1h = n1 // 2
    k1n = n1h + 1
    k1p = -(-k1n // SUBLANES) * SUBLANES
    two_pi = 2.0 * math.pi
    k1 = np.arange(k1p, dtype=np.int64)[:, None]
    live = (k1 < k1n).astype(np.float64)
    m1 = np.arange(n1h, dtype=np.int64)[None, :]
    ang1 = two_pi * ((k1 * m1) % n1) / n1
    f1m = np.concatenate([np.cos(ang1) * live, -np.sin(ang1) * live], axis=0).astype(BF16)
    m2 = np.arange(n2, dtype=np.int64)[None, :]
    ang2 = two_pi * ((k1 * m2) % n) / n
    tr = (np.cos(ang2) * live).astype(np.float32)
    ti = (-np.sin(ang2) * live).astype(np.float32)
    ang3 = two_pi * ((m2.T * m2) % n2) / n2
    f2r, f2i = np.cos(ang3), -np.sin(ang3)
    f2 = np.block([[f2r, f2i], [-f2i, f2r]]).astype(BF16)
    f2inv = np.block([[f2r, -f2i], [f2i, f2r]]).astype(BF16)
    o1 = np.arange(n1h, dtype=np.int64)[:, None]
    q1 = np.arange(n1h, dtype=np.int64)[None, :]
    ang4 = two_pi * ((o1 * q1) % n1) / n1
    wgt = np.where(q1 == 0, 1.0, 2.0)
    cmat = np.concatenate([wgt * np.cos(ang4), -wgt * np.sin(ang4)], axis=1).astype(BF16)
    sgn = np.broadcast_to(np.where(o1 % 2 == 0, 1.0, -1.0), (n1h, n2)).astype(np.float32)
    return (f1m, tr, ti, f2, f2inv, cmat, sgn), n1h


def _pick(n, pref):
    t = pref
    while n % t:
        t //= 2
    return t


def kernel(x, c, ada_w, ada_b, norm_mix_g, w_in, hy_conv_w, hy_conv_b, filt_w1, filt_b1, filt_freq1, filt_w2, filt_b2, filt_freq2, filt_w3, hy_bias, q_norm_g, kv_norm_g, w_uq, w_uk, w_uv, w_hy_out, w_mla_out, w_mix_out, norm_ffn_g, w_group, b_group, w_router, b_router, w_gate, w_up, w_down, final_norm_g):
    B, L, d = x.shape
    assert B == 1 and d == D_MODEL and ada_w.shape[0] == 1 and L % (2 * DFT_N2) == 0
    x2 = x.reshape(L, d)
    row = lambda v: v.reshape(1, -1)
    col = lambda v: v.reshape(-1, 1)

    mod = _ada(jnp.pad(c, ((0, 8 - B), (0, 0))), ada_w[0], row(ada_b[0]))

    wi = w_in[0]
    kr_slab = jnp.pad(wi[:, OFF_KR:OFF_GATE], ((0, 0), (QK_NOPE, HEAD_SLAB - QK_NOPE - QK_ROPE)))
    wcat = jnp.concatenate([wi[:, OFF_Q:OFF_KV], wi[:, OFF_KV:OFF_KR], wi[:, OFF_GATE:], kr_slab],
                           axis=1).astype(BF16)
    whyt = wi[:, 0:OFF_Q].T.astype(BF16)
    wuq = _pad_heads(w_uq[0], QK_NOPE + QK_ROPE).astype(BF16)
    wuk = _pad_heads(w_uk[0].reshape(KV_LORA, N_HEADS * QK_NOPE), QK_NOPE).astype(BF16)
    wuvt = _pad_heads(w_uv[0].reshape(KV_LORA, N_HEADS * V_HEAD), V_HEAD).T.astype(BF16)
    tkv = _pick(L, 1024)
    cos_t, sin_a, sin_b = _rope_slab_tables(L)

    hypt, gates, q, k, vt, kn2_tiles = _inproj(x2, mod, row(norm_mix_g[0]), wcat, whyt, row(q_norm_g[0]),
                                    row(kv_norm_g[0]), wuq, wuk, wuvt, cos_t, sin_a, sin_b,
                                    _pick(L, 512), tkv)

    zt = _filter_features(L)
    w1t = jnp.pad(filt_w1[0], ((0, FILTER_HIDDEN - FILTER_EMB), (0, 0))).T
    dl = np.abs(np.linspace(MIN_DECAY, MAX_DECAY, HY_WIDTH))[:, None].astype(np.float32)
    ht, nrm = _filt(zt, w1t, col(filt_b1[0]), col(filt_freq1[0]), filt_w2[0].T, col(filt_b2[0]),
                    col(filt_freq2[0]), filt_w3[0].T.astype(BF16), dl, _pick(L, 1024))
    tabs, n1h = _dft_tables(L)
    cw, cbias = hy_conv_w[0], hy_conv_b[0]
    w = HY_WIDTH
    taps = lambda s: [cw[0, s * w:(s + 1) * w], cw[1, s * w:(s + 1) * w], cw[2, s * w:(s + 1) * w],
                      cbias[s * w:(s + 1) * w]]
    prm = jnp.stack(taps(0) + taps(1) + taps(2) + [hy_bias[0], nrm[:, 0]]
                    + [jnp.zeros((w,), F32)] * (HY_PARAM_ROWS - 14), axis=1)
    prm = jnp.broadcast_to(prm[:, :, None], (w, HY_PARAM_ROWS, DFT_N2))
    yht = _hyena(hypt.reshape(3 * w, n1h, DFT_N2), ht.reshape(2 * w, n1h, DFT_N2), prm, tabs,
                 _pick(w, 8), 1.0 / (2 * L)).reshape(w, L)

    kn2 = jnp.broadcast_to(jnp.max(kn2_tiles, axis=0)[:, None, :], (N_HEADS, SUBLANES, LANES))
    ym = _attn(q, k, vt, kn2, _pick(L, 1024), 4)

    wmla = jnp.pad(w_mla_out[0].reshape(N_HEADS, V_HEAD, d),
                   ((0, 0), (0, HEAD_SLAB - V_HEAD), (0, 0))).reshape(QK_SLABS, d).astype(BF16)
    wrt = jnp.pad(jnp.concatenate([w_router[0], w_group[0]], axis=1),
                  ((0, 0), (0, LANES - N_EXPERTS - N_GROUPS)))
    brt = jnp.pad(jnp.concatenate([b_router[0], b_group[0]]), (0, LANES - N_EXPERTS - N_GROUPS))[None, :]
    x1, h2, comb = _merge(x2, yht, ym, gates, mod, w_hy_out[0].astype(BF16), wmla,
                          w_mix_out[0].astype(BF16), row(norm_ffn_g[0]), wrt, brt, _pick(L, 512))

    wgu = jnp.concatenate([w_gate[0], w_up[0]], axis=2).astype(BF16)
    wd = w_down[0].reshape(N_EXPERTS * D_EXPERT, d).astype(BF16)
    out = _moe(h2, comb, wgu, wd, x1, mod, row(final_norm_g), _pick(L, 1024), 4)
    return out.reshape(B, L, d)
```

```python
import functools
import math

import numpy as np
import jax
import jax.numpy as jnp
from jax import lax
from jax.experimental import pallas as pl
from jax.experimental.pallas import tpu as pltpu

F32 = jnp.float32
BF16 = jnp.bfloat16
HIGHEST = lax.Precision.HIGHEST

D_MODEL = 1024
EPS = 1e-6
HY_WIDTH = 512
FILTER_BANDS = 16
FILTER_EMB = 1 + 2 * FILTER_BANDS
FILTER_HIDDEN = 64
FILTER_DECAY_TARGET = 1e-2
FAST_DECAY_PCT = 0.3
SLOW_DECAY_PCT = 1.5
FILTER_SHIFT = 0.05
MIN_DECAY = math.log(FILTER_DECAY_TARGET) / SLOW_DECAY_PCT
MAX_DECAY = math.log(FILTER_DECAY_TARGET) / FAST_DECAY_PCT
N_HEADS = 8
QK_NOPE = 64
QK_ROPE = 32
V_HEAD = 64
Q_LORA = 384
KV_LORA = 256
ROPE_THETA = 10000.0
MLA_WIDTH = N_HEADS * V_HEAD
OFF_Q = 3 * HY_WIDTH
OFF_KV = OFF_Q + Q_LORA
OFF_KR = OFF_KV + KV_LORA
OFF_GATE = OFF_KR + QK_ROPE
N_GROUPS = 4
EXPERTS_PER_GROUP = 8
N_EXPERTS = N_GROUPS * EXPERTS_PER_GROUP
D_EXPERT = 256

LANES = 128
SUBLANES = 8
HEAD_SLAB = LANES
QK_SLABS = N_HEADS * HEAD_SLAB
ONES_LANE = V_HEAD
DFT_N2 = LANES
HY_PARAM_ROWS = 16
NEG_BIG = -1e30
VMEM_LIMIT = 56 * 1024 * 1024
MAX_STATIC_SHIFT = 60.0
SHIFT_SLACK = 1.002
SHIFT_EPS = 1e-3

C_Q = 0
C_KV = C_Q + Q_LORA
C_GATE = C_KV + KV_LORA
C_KR = C_GATE + 2 * D_MODEL
C_END = C_KR + HEAD_SLAB


def _cparams(*sem):
    return pltpu.CompilerParams(dimension_semantics=sem, vmem_limit_bytes=VMEM_LIMIT)


def _rms(x):
    return x * lax.rsqrt(jnp.mean(x * x, axis=-1, keepdims=True) + EPS)


def _dot_nt(a, b):
    return lax.dot_general(a, b, (((1,), (1,)), ((), ())), preferred_element_type=F32)


def _dot_tn(a, b):
    return lax.dot_general(a, b, (((0,), (0,)), ((), ())), preferred_element_type=F32)


def _ada_kernel(c_ref, w_ref, b_ref, o_ref):
    c = c_ref[...]
    ca = c * jax.nn.sigmoid(c)
    o_ref[...] = jnp.dot(ca, w_ref[...], preferred_element_type=F32, precision=HIGHEST) + b_ref[...]


def _ada(c8, w, b):
    d, n = w.shape
    tn = 1024
    return pl.pallas_call(
        _ada_kernel,
        grid=(n // tn,),
        in_specs=[pl.BlockSpec((8, d), lambda j: (0, 0)),
                  pl.BlockSpec((d, tn), lambda j: (0, j)),
                  pl.BlockSpec((1, tn), lambda j: (0, j))],
        out_specs=pl.BlockSpec((8, tn), lambda j: (0, j)),
        out_shape=jax.ShapeDtypeStruct((8, n), F32),
        compiler_params=_cparams("parallel"),
        name="ada",
    )(c8, w, b)


def _rope_slab(z, cos_t, sin_a, sin_b):
    return z * cos_t + pltpu.roll(z, HEAD_SLAB - QK_ROPE // 2, 1) * sin_a + pltpu.roll(z, QK_ROPE // 2, 1) * sin_b


def _inproj_kernel(x_ref, mod_ref, g_ref, wcat_ref, whyt_ref, gq_ref, gkv_ref, wuq_ref, wuk_ref, wuvt_ref,
                   cos_ref, sina_ref, sinb_ref,
                   hypt_ref, gate_ref, q_ref, k_ref, vt_ref, kn2_ref, *, qscale):
    d = D_MODEL
    x = x_ref[...]
    sh1 = mod_ref[0:1, 0:d]
    sc1 = mod_ref[0:1, d:2 * d]
    hb = (_rms(x) * g_ref[...] * (1.0 + sc1) + sh1).astype(BF16)
    hypt_ref[...] = _dot_nt(whyt_ref[...], hb).astype(BF16)
    proj = jnp.dot(hb, wcat_ref[...], preferred_element_type=F32)
    gate_ref[...] = jax.nn.sigmoid(proj[:, C_GATE:C_KR]).astype(BF16)
    cq = _rms(proj[:, C_Q:C_KV]) * gq_ref[...]
    ckv = (_rms(proj[:, C_KV:C_GATE]) * gkv_ref[...]).astype(BF16)
    q = jnp.dot(cq.astype(BF16), wuq_ref[...], preferred_element_type=F32)
    kn = jnp.dot(ckv, wuk_ref[...], preferred_element_type=F32)
    vt = _dot_nt(wuvt_ref[...], ckv)
    cos_t = cos_ref[...]
    sin_a = sina_ref[...]
    sin_b = sinb_ref[...]
    kr = _rope_slab(proj[:, C_KR:C_END], cos_t, sin_a, sin_b)
    srow = lax.broadcasted_iota(jnp.int32, (HEAD_SLAB, 1), 0)
    ones_row = jnp.where(srow == ONES_LANE, 1.0, 0.0).astype(F32)
    kn2 = []
    for hd in range(N_HEADS):
        sl = slice(hd * HEAD_SLAB, (hd + 1) * HEAD_SLAB)
        q_ref[:, sl] = (_rope_slab(q[:, sl], cos_t, sin_a, sin_b) * qscale).astype(BF16)
        kb = (kn[:, sl] + kr).astype(BF16)
        k_ref[:, sl] = kb
        vt_ref[hd, 0] = (vt[sl] + ones_row).astype(BF16)
        kf = kb.astype(F32)
        big = jnp.max(jnp.sum(kf * kf, axis=1, keepdims=True), axis=0, keepdims=True)
        kn2.append(jnp.broadcast_to(big, (1, HEAD_SLAB)))
    kn2_ref[0] = jnp.concatenate(kn2, axis=0)


def _inproj(x2, mod, g1, wcat, whyt, gq, gkv, wuq, wuk, wuvt, cos_t, sin_a, sin_b, tm, tkv):
    L, d = x2.shape
    qscale = (QK_NOPE + QK_ROPE) ** -0.5 * math.log2(math.e)
    full = lambda a: pl.BlockSpec(a.shape, lambda i: (0,) * a.ndim)
    row = lambda w: pl.BlockSpec((tm, w), lambda i: (i, 0))
    per = tkv // tm
    return pl.pallas_call(
        functools.partial(_inproj_kernel, qscale=qscale),
        grid=(L // tm,),
        in_specs=[row(d), full(mod), full(g1), full(wcat), full(whyt), full(gq), full(gkv), full(wuq),
                  full(wuk), full(wuvt), row(HEAD_SLAB), row(HEAD_SLAB), row(HEAD_SLAB)],
        out_specs=[pl.BlockSpec((3 * HY_WIDTH, tm), lambda i: (0, i)),
                   row(2 * D_MODEL), row(QK_SLABS), row(QK_SLABS),
                   pl.BlockSpec((N_HEADS, 1, HEAD_SLAB, tm), lambda i: (0, i // per, 0, i % per)),
                   pl.BlockSpec((1, N_HEADS, HEAD_SLAB), lambda i: (i, 0, 0))],
        out_shape=[jax.ShapeDtypeStruct((3 * HY_WIDTH, L), BF16),
                   jax.ShapeDtypeStruct((L, 2 * D_MODEL), BF16),
                   jax.ShapeDtypeStruct((L, QK_SLABS), BF16),
                   jax.ShapeDtypeStruct((L, QK_SLABS), BF16),
                   jax.ShapeDtypeStruct((N_HEADS, L // tkv, HEAD_SLAB, tkv), BF16),
                   jax.ShapeDtypeStruct((L // tm, N_HEADS, HEAD_SLAB), F32)],
        compiler_params=_cparams("parallel"),
        name="inproj",
    )(x2, mod, g1, wcat, whyt, gq, gkv, wuq, wuk, wuvt, cos_t, sin_a, sin_b)


def _filt_kernel(z_ref, w1_ref, b1_ref, f1_ref, w2_ref, b2_ref, f2_ref, w3_ref, dl_ref,
                 h_ref, nrm_ref):
    i = pl.program_id(0)
    z = z_ref[...]
    tm = z.shape[1]
    h = jnp.sin(f1_ref[...] * (jnp.dot(w1_ref[...], z, preferred_element_type=F32, precision=HIGHEST)
                               + b1_ref[...]))
    h = jnp.sin(f2_ref[...] * (jnp.dot(w2_ref[...], h, preferred_element_type=F32, precision=HIGHEST)
                               + b2_ref[...]))
    h3 = jnp.dot(w3_ref[...], h.astype(BF16), preferred_element_type=F32)
    t = z[0:1, :]
    window = jnp.exp(-dl_ref[...] * t) + FILTER_SHIFT
    hf = h3[0:HY_WIDTH] * window
    hb = h3[HY_WIDTH:2 * HY_WIDTH] * window
    col = lax.broadcasted_iota(jnp.int32, (1, tm), 1)
    hb = jnp.where((col == 0) & (i == 0), 0.0, hb)
    h_ref[0:HY_WIDTH, :] = hf.astype(BF16)
    h_ref[HY_WIDTH:2 * HY_WIDTH, :] = hb.astype(BF16)
    part = jnp.sum(jnp.abs(hf) + jnp.abs(hb), axis=1, keepdims=True)

    @pl.when(i == 0)
    def _():
        nrm_ref[...] = jnp.zeros_like(nrm_ref)

    nrm_ref[...] += part


def _filt(zt, w1t, b1, f1, w2t, b2, f2, w3t, dl, tm):
    L = zt.shape[1]
    full = lambda a: pl.BlockSpec(a.shape, lambda i: (0,) * a.ndim)
    return pl.pallas_call(
        _filt_kernel,
        grid=(L // tm,),
        in_specs=[pl.BlockSpec((zt.shape[0], tm), lambda i: (0, i)), full(w1t), full(b1), full(f1),
                  full(w2t), full(b2), full(f2), full(w3t), full(dl)],
        out_specs=[pl.BlockSpec((2 * HY_WIDTH, tm), lambda i: (0, i)),
                   pl.BlockSpec((HY_WIDTH, 1), lambda i: (0, 0))],
        out_shape=[jax.ShapeDtypeStruct((2 * HY_WIDTH, L), BF16),
                   jax.ShapeDtypeStruct((HY_WIDTH, 1), F32)],
        compiler_params=_cparams("arbitrary"),
        name="filt",
    )(zt, w1t, b1, f1, w2t, b2, f2, w3t, dl)


def _hyena_kernel(x0_ref, x1_ref, v_ref, hf_ref, hb_ref, prm_ref, f1_ref, tr_ref, ti_ref, f2_ref,
                  f2i_ref, cm_ref, sg_ref, o_ref, *, inv_n):
    cb, n1h, w = x0_ref.shape
    k1p = tr_ref.shape[0]
    lane = lax.broadcasted_iota(jnp.int32, (n1h, w), 1)
    rowi = lax.broadcasted_iota(jnp.int32, (n1h, w), 0)
    tr = tr_ref[...]
    ti = ti_ref[...]

    def conv(x, p, j):
        r = pltpu.roll(x, 1, 1)
        up = jnp.where(lane == 0, jnp.where(rowi == 0, 0.0, pltpu.roll(r, 1, 0)), r)
        l = pltpu.roll(x, w - 1, 1)
        dn = jnp.where(lane == w - 1, jnp.where(rowi == n1h - 1, 0.0, pltpu.roll(l, n1h - 1, 0)), l)
        return up * p[j:j + 1] + x * p[j + 1:j + 2] + dn * p[j + 2:j + 3] + p[j + 3:j + 4]

    def twiddle(a, s):
        ar = a[0:k1p, s * w:(s + 1) * w]
        ai = a[k1p:2 * k1p, s * w:(s + 1) * w]
        return jnp.concatenate([ar * tr - ai * ti, ar * ti + ai * tr], axis=1).astype(BF16)

    prm = [prm_ref[c] for c in range(cb)]
    x0s, us, sigs = [], [], []
    for c in range(cb):
        p = prm[c]
        x0s.append(conv(x0_ref[c].astype(F32), p, 0))
        u = conv(v_ref[c].astype(F32), p, 8) * conv(x1_ref[c].astype(F32), p, 4)
        us.append(u)
        sigs += [u.astype(BF16), hf_ref[c], hb_ref[c]]
    a = jnp.dot(f1_ref[...], jnp.concatenate(sigs, axis=1), preferred_element_type=F32)
    b = jnp.concatenate([twiddle(a, s) for s in range(3 * cb)], axis=0)
    s = jnp.dot(b, f2_ref[...], preferred_element_type=F32)
    ys = []
    for c in range(cb):
        su = s[(3 * c) * k1p:(3 * c + 1) * k1p]
        sf = s[(3 * c + 1) * k1p:(3 * c + 2) * k1p]
        sb = s[(3 * c + 2) * k1p:(3 * c + 3) * k1p]
        inv = 1.0 / prm[c][13:14]
        gr = (sf[:, 0:w] + sb[:, 0:w]) * inv
        gi = (sf[:, w:2 * w] - sb[:, w:2 * w]) * inv
        ur, ui = su[:, 0:w], su[:, w:2 * w]
        ys.append(jnp.concatenate([ur * gr - ui * gi, ur * gi + ui * gr], axis=1).astype(BF16))
    zt = jnp.dot(jnp.concatenate(ys, axis=0), f2i_ref[...], preferred_element_type=F32)
    zss, nyq = [], []
    for c in range(cb):
        ztr = zt[c * k1p:(c + 1) * k1p, 0:w]
        zti = zt[c * k1p:(c + 1) * k1p, w:2 * w]
        zr = ztr * tr + zti * ti
        zi = zti * tr - ztr * ti
        zss.append(jnp.concatenate([zr[0:n1h], zi[0:n1h]], axis=0).astype(BF16))
        nyq.append(zr[n1h:n1h + 1])
    yy = jnp.dot(cm_ref[...], jnp.concatenate(zss, axis=1), preferred_element_type=F32)
    for c in range(cb):
        conv_out = (yy[:, c * w:(c + 1) * w] + sg_ref[...] * nyq[c]) * inv_n
        o_ref[c] = (x0s[c] * (conv_out + prm[c][12:13] * us[c])).astype(BF16)


def _hyena(hyp3, h3, prm, tabs, cb, inv_n):
    c3, n1h, w = hyp3.shape
    c = c3 // 3
    nb = c // cb
    f1m, tr, ti, f2, f2i, cmat, sgn = tabs
    full = lambda a: pl.BlockSpec(a.shape, lambda i: (0,) * a.ndim)
    blk = lambda off: pl.BlockSpec((cb, n1h, w), lambda i: (off * nb + i, 0, 0))
    return pl.pallas_call(
        functools.partial(_hyena_kernel, inv_n=inv_n),
        grid=(nb,),
        in_specs=[blk(0), blk(1), blk(2), blk(0), blk(1),
                  pl.BlockSpec((cb, HY_PARAM_ROWS, w), lambda i: (i, 0, 0)),
                  full(f1m), full(tr), full(ti), full(f2), full(f2i), full(cmat), full(sgn)],
        out_specs=pl.BlockSpec((cb, n1h, w), lambda i: (i, 0, 0)),
        out_shape=jax.ShapeDtypeStruct((c, n1h, w), BF16),
        compiler_params=_cparams("parallel"),
        name="hyena",
    )(hyp3, hyp3, hyp3, h3, h3, prm, f1m, tr, ti, f2, f2i, cmat, sgn)


def _attn_kernel(q_ref, k_ref, vt_ref, kn2_ref, o_ref, *, unroll):
    tq = q_ref.shape[0]
    nk, _, tk = vt_ref.shape[1:]
    q = q_ref[...]

    def finish(acc):
        o_ref[...] = (acc / acc[ONES_LANE:ONES_LANE + 1, :]).astype(BF16)

    qf = q.astype(F32)
    qn2 = lax.dot_general(jnp.ones((SUBLANES, HEAD_SLAB), F32), qf * qf, (((1,), (1,)), ((), ())),
                          precision=HIGHEST, preferred_element_type=F32)[0:1]
    bound = jnp.sqrt(qn2 * kn2_ref[0][0:1, 0:1]) * SHIFT_SLACK + SHIFT_EPS
    static_shift_ok = jnp.max(bound) <= MAX_STATIC_SHIFT

    @pl.when(static_shift_ok)
    def _():
        def body(j, acc):
            off = pl.multiple_of(j * tk, tk)
            st = _dot_nt(k_ref[pl.ds(off, tk), :], q)
            p = jnp.exp2(st - bound).astype(BF16)
            return acc + jnp.dot(vt_ref[0, j], p, preferred_element_type=F32)

        finish(lax.fori_loop(0, nk, body, jnp.zeros((HEAD_SLAB, tq), F32), unroll=unroll))

    @pl.when(jnp.logical_not(static_shift_ok))
    def _():
        def body(j, carry):
            m, acc = carry
            off = pl.multiple_of(j * tk, tk)
            st = _dot_nt(k_ref[pl.ds(off, tk), :], q)
            m_new = jnp.maximum(m, jnp.max(st, axis=0, keepdims=True))
            alpha = jnp.exp2(m - m_new)
            p = jnp.exp2(st - m_new)
            acc = alpha * acc + jnp.dot(vt_ref[0, j], p.astype(BF16), preferred_element_type=F32)
            return m_new, acc

        m0 = jnp.full((1, tq), NEG_BIG, F32)
        acc0 = jnp.zeros((HEAD_SLAB, tq), F32)
        finish(lax.fori_loop(0, nk, body, (m0, acc0), unroll=unroll)[1])


def _attn(q, k, vt, kn2, tq, unroll):
    L = q.shape[0]
    _, nk, _, tk = vt.shape
    return pl.pallas_call(
        functools.partial(_attn_kernel, unroll=unroll),
        grid=(N_HEADS, L // tq),
        in_specs=[pl.BlockSpec((tq, HEAD_SLAB), lambda h, i: (i, h)),
                  pl.BlockSpec((L, HEAD_SLAB), lambda h, i: (0, h)),
                  pl.BlockSpec((1, nk, HEAD_SLAB, tk), lambda h, i: (h, 0, 0, 0)),
                  pl.BlockSpec((1, SUBLANES, LANES), lambda h, i: (h, 0, 0))],
        out_specs=pl.BlockSpec((HEAD_SLAB, tq), lambda h, i: (h, i)),
        out_shape=jax.ShapeDtypeStruct((QK_SLABS, L), BF16),
        compiler_params=_cparams("parallel", "parallel"),
        name="attn",
    )(q, k, vt, kn2)


def _merge_kernel(x_ref, yht_ref, ymt_ref, gate_ref, mod_ref, why_ref, wmla_ref, wmix_ref, g2_ref,
                  wrt_ref, brt_ref, x1_ref, h2_ref, comb_ref, gsel_ref, gcnt_ref):
    d = D_MODEL
    a = _dot_tn(yht_ref[...], why_ref[...])
    b = _dot_tn(ymt_ref[...], wmla_ref[...])
    mix = gate_ref[:, 0:d].astype(F32) * a + gate_ref[:, d:2 * d].astype(F32) * b
    o = jnp.dot(mix.astype(BF16), wmix_ref[...], preferred_element_type=F32)
    gt1 = mod_ref[0:1, 2 * d:3 * d]
    sh2 = mod_ref[0:1, 3 * d:4 * d]
    sc2 = mod_ref[0:1, 4 * d:5 * d]
    x1 = x_ref[...] + gt1 * o
    x1_ref[...] = x1
    h2 = _rms(x1) * g2_ref[...] * (1.0 + sc2) + sh2
    h2_hi = h2.astype(BF16)
    h2_ref[...] = h2_hi
    h2_lo = (h2 - h2_hi.astype(F32)).astype(BF16)
    wrt = wrt_ref[...]
    w_hi = wrt.astype(BF16)
    w_lo = (wrt - w_hi.astype(F32)).astype(BF16)
    e = (jnp.dot(h2_hi, w_hi, preferred_element_type=F32)
         + (jnp.dot(h2_lo, w_hi, preferred_element_type=F32)
            + jnp.dot(h2_hi, w_lo, preferred_element_type=F32))) + brt_ref[...]
    lane = lax.broadcasted_iota(jnp.int32, e.shape, 1).astype(F32)
    gvalid = (lane >= N_EXPERTS) & (lane < N_EXPERTS + N_GROUPS)
    glm = jnp.where(gvalid, e, NEG_BIG)
    gmax = jnp.max(glm, axis=1, keepdims=True)
    gidx = jnp.min(jnp.where(glm == gmax, lane, float(LANES)), axis=1, keepdims=True) - float(N_EXPERTS)
    psum = jnp.sum(jnp.where(gvalid, jnp.exp(glm - gmax), 0.0), axis=1, keepdims=True)
    p_sel = 1.0 / psum
    lo = gidx * EXPERTS_PER_GROUP
    em = jnp.where((lane >= lo) & (lane < lo + EXPERTS_PER_GROUP), e, NEG_BIG)
    v1 = jnp.max(em, axis=1, keepdims=True)
    i1 = jnp.min(jnp.where(em == v1, lane, float(LANES)), axis=1, keepdims=True)
    em2 = jnp.where(lane == i1, NEG_BIG, em)
    v2 = jnp.max(em2, axis=1, keepdims=True)
    i2 = jnp.min(jnp.where(em2 == v2, lane, float(LANES)), axis=1, keepdims=True)
    t = jnp.exp(v2 - v1)
    w1 = p_sel / (1.0 + t)
    w2 = p_sel * t / (1.0 + t)
    comb_ref[...] = jnp.where(lane == i1, w1, 0.0) + jnp.where(lane == i2, w2, 0.0)
    gsel = jnp.where(lane == gidx, 1.0, 0.0)
    gsel_ref[...] = gsel.astype(BF16)
    srow = lax.broadcasted_iota(jnp.int32, (SUBLANES, LANES), 0)
    gcnt_ref[0] = jnp.where(srow == 0, jnp.sum(gsel, axis=0, keepdims=True), 0.0)


def _merge(x2, yht, ym, gates, mod, why, wmla, wmix, g2, wrt, brt, tm):
    L, d = x2.shape
    full = lambda a: pl.BlockSpec(a.shape, lambda i: (0,) * a.ndim)
    row = lambda w: pl.BlockSpec((tm, w), lambda i: (i, 0))
    return pl.pallas_call(
        _merge_kernel,
        grid=(L // tm,),
        in_specs=[row(d), pl.BlockSpec((HY_WIDTH, tm), lambda i: (0, i)),
                  pl.BlockSpec((QK_SLABS, tm), lambda i: (0, i)), row(2 * d),
                  full(mod), full(why), full(wmla), full(wmix), full(g2), full(wrt), full(brt)],
        out_specs=[row(d), row(d), row(LANES), row(LANES),
                   pl.BlockSpec((1, SUBLANES, LANES), lambda i: (i, 0, 0))],
        out_shape=[jax.ShapeDtypeStruct((L, d), F32),
                   jax.ShapeDtypeStruct((L, d), BF16),
                   jax.ShapeDtypeStruct((L, LANES), F32),
                   jax.ShapeDtypeStruct((L, LANES), BF16),
                   jax.ShapeDtypeStruct((L // tm, SUBLANES, LANES), F32)],
        compiler_params=_cparams("parallel"),
        name="merge",
    )(x2, yht, ym, gates, mod, why, wmla, wmix, g2, wrt, brt)


MOE_TILE = 2048
MOE_ROWS = 256
MOE_BLOCKS = MOE_TILE // MOE_ROWS + N_GROUPS
POS_RADIX = 64


def _moe_kernel(bg_ref, nbu_ref, roff_ref, h_ref, comb_ref, gsel_ref, wgu_ref, wd_ref, o_ref,
                pos_ref, cs_ref, acc_ref):
    i = pl.program_id(0)
    b = pl.program_id(1)
    t = h_ref.shape[0]
    lane = lax.broadcasted_iota(jnp.int32, (1, LANES), 1)

    @pl.when(b == 0)
    def _():
        oh = gsel_ref[...]
        sub = MOE_ROWS
        tri = jnp.where(lax.broadcasted_iota(jnp.int32, (sub, sub), 1)
                        < lax.broadcasted_iota(jnp.int32, (sub, sub), 0), 1.0, 0.0).astype(BF16)
        carry = jnp.zeros((1, LANES), F32)
        cums = []
        for s in range(t // sub):
            blk = oh[s * sub:(s + 1) * sub]
            cums.append(jnp.dot(tri, blk, preferred_element_type=F32) + carry)
            carry = carry + jnp.sum(blk.astype(F32), axis=0, keepdims=True)
        cum = jnp.concatenate(cums, axis=0)
        offs = jnp.zeros((1, LANES), F32)
        for g in range(N_GROUPS):
            offs = jnp.where(lane == g, roff_ref[i * N_GROUPS + g].astype(F32), offs)
        pos = jnp.sum(jnp.where(oh > 0, cum + offs, 0.0), axis=1, keepdims=True)
        hi = jnp.floor(pos * (1.0 / POS_RADIX))
        lo = pos - hi * POS_RADIX
        digits = jnp.where(lane == 0, hi, jnp.where(lane == 1, lo, 0.0)).astype(BF16)
        srow = lax.broadcasted_iota(jnp.int32, (SUBLANES, LANES), 0)
        scol = lax.broadcasted_iota(jnp.int32, (SUBLANES, LANES), 1)
        pick = jnp.where(srow == scol, 1.0, 0.0).astype(BF16)
        rows = _dot_nt(pick, digits)
        pos_ref[...] = jnp.broadcast_to(rows[0:1] * POS_RADIX + rows[1:2], pos_ref.shape)
        comb = comb_ref[...]
        comb_hi = comb.astype(BF16)
        cs_ref[:, 0:LANES] = comb_hi
        cs_ref[:, LANES:2 * LANES] = (comb - comb_hi.astype(F32)).astype(BF16)
        acc_ref[...] = jnp.zeros_like(acc_ref)

    @pl.when(b < nbu_ref[i])
    def _():
        g = bg_ref[i * MOE_BLOCKS + b]
        rid = (lax.broadcasted_iota(jnp.int32, (MOE_ROWS, 1), 0) + b * MOE_ROWS).astype(F32)
        sel = jnp.where(rid == pos_ref[0:1, :], 1.0, 0.0).astype(BF16)
        xs = jnp.dot(sel, h_ref[...], preferred_element_type=F32).astype(BF16)
        cs = jnp.dot(sel, cs_ref[...], preferred_element_type=F32)
        comb_s = cs[:, 0:LANES] + cs[:, LANES:2 * LANES]
        parts = []
        for j in range(EXPERTS_PER_GROUP):
            gu = jnp.dot(xs, wgu_ref[j], preferred_element_type=F32)
            gt = gu[:, 0:D_EXPERT]
            a = gt * jax.nn.sigmoid(gt) * gu[:, D_EXPERT:2 * D_EXPERT]
            col = jnp.sum(jnp.where(lane == g * EXPERTS_PER_GROUP + j, comb_s, 0.0), axis=1, keepdims=True)
            parts.append((a * col).astype(BF16))
        y = jnp.dot(jnp.concatenate(parts, axis=1), wd_ref[...], preferred_element_type=F32)
        acc_ref[...] += _dot_tn(sel, y.astype(BF16))

    @pl.when(b == pl.num_programs(1) - 1)
    def _():
        o_ref[...] = acc_ref[...].astype(BF16)


def _moe(bg, nbu, roff, h2, comb, gsel, wgu, wd):
    L, d = h2.shape
    t = MOE_TILE
    once = dict(pipeline_mode=pl.Buffered(1))
    wsel = lambda i, b, bg_r, nbu_r, roff_r: (bg_r[i * MOE_BLOCKS + b], 0, 0)
    grid_spec = pltpu.PrefetchScalarGridSpec(
        num_scalar_prefetch=3,
        grid=(L // t, MOE_BLOCKS),
        in_specs=[pl.BlockSpec((t, d), lambda i, b, *_: (i, 0), **once),
                  pl.BlockSpec((t, LANES), lambda i, b, *_: (i, 0), **once),
                  pl.BlockSpec((t, LANES), lambda i, b, *_: (i, 0), **once),
                  pl.BlockSpec((EXPERTS_PER_GROUP, d, 2 * D_EXPERT), wsel),
                  pl.BlockSpec((EXPERTS_PER_GROUP * D_EXPERT, d), lambda i, b, bg_r, *_: (bg_r[i * MOE_BLOCKS + b], 0))],
        out_specs=pl.BlockSpec((t, d), lambda i, b, *_: (i, 0)),
        scratch_shapes=[pltpu.VMEM((SUBLANES, t), F32),
                        pltpu.VMEM((t, 2 * LANES), BF16),
                        pltpu.VMEM((t, d), F32)])
    return pl.pallas_call(
        _moe_kernel,
        grid_spec=grid_spec,
        out_shape=jax.ShapeDtypeStruct((L, d), BF16),
        compiler_params=_cparams("parallel", "arbitrary"),
        name="moe",
    )(bg, nbu, roff, h2, comb, gsel, wgu, wd)


def _moe_plan(gcnt, n_tiles):
    cnt = gcnt[:, 0, 0:N_GROUPS].reshape(n_tiles, -1, N_GROUPS).sum(axis=1).astype(jnp.int32)
    nblk = (cnt + MOE_ROWS - 1) // MOE_ROWS
    bend = jnp.cumsum(nblk, axis=1)
    roff = (bend - nblk) * MOE_ROWS
    nbu = bend[:, -1]
    bidx = jnp.arange(MOE_BLOCKS, dtype=jnp.int32)[None, :]
    grp = jnp.sum((bidx[:, :, None] >= bend[:, None, :]).astype(jnp.int32), axis=2)
    last = jnp.take_along_axis(grp, jnp.maximum(nbu - 1, 0)[:, None], axis=1)
    bg = jnp.where(bidx < nbu[:, None], grp, last)
    return bg.reshape(-1), nbu, roff.reshape(-1)


def _final_kernel(x1_ref, m_ref, mod_ref, fg_ref, o_ref):
    d = D_MODEL
    x2 = x1_ref[...] + mod_ref[0:1, 5 * d:6 * d] * m_ref[...].astype(F32)
    o_ref[...] = _rms(x2) * fg_ref[...]


def _final(x1, m, mod, fg, tm):
    L, d = x1.shape
    return pl.pallas_call(
        _final_kernel,
        grid=(L // tm,),
        in_specs=[pl.BlockSpec((tm, d), lambda i: (i, 0)), pl.BlockSpec((tm, d), lambda i: (i, 0)),
                  pl.BlockSpec(mod.shape, lambda i: (0, 0)), pl.BlockSpec(fg.shape, lambda i: (0, 0))],
        out_specs=pl.BlockSpec((tm, d), lambda i: (i, 0)),
        out_shape=jax.ShapeDtypeStruct((L, d), F32),
        compiler_params=_cparams("parallel"),
        name="final",
    )(x1, m, mod, fg)


def _pad_heads(w, width):
    lead = w.shape[:-1]
    w = w.reshape(lead + (N_HEADS, width))
    w = jnp.pad(w, [(0, 0)] * len(lead) + [(0, 0), (0, HEAD_SLAB - width)])
    return w.reshape(lead + (N_HEADS * HEAD_SLAB,))


@functools.lru_cache(maxsize=None)
def _rope_slab_tables(L):
    half = QK_ROPE // 2
    pos = np.arange(L, dtype=np.float64)
    inv = ROPE_THETA ** (-np.arange(0, QK_ROPE, 2, dtype=np.float64) / QK_ROPE)
    ang = pos[:, None] * inv[None, :]
    cos, sin = np.cos(ang), np.sin(ang)
    z = lambda w: np.zeros((L, w))
    cos_t = np.concatenate([np.ones((L, QK_NOPE)), cos, cos, z(HEAD_SLAB - QK_NOPE - QK_ROPE)], axis=1)
    sin_a = np.concatenate([z(QK_NOPE), -sin, z(HEAD_SLAB - QK_NOPE - half)], axis=1)
    sin_b = np.concatenate([z(QK_NOPE + half), sin, z(HEAD_SLAB - QK_NOPE - QK_ROPE)], axis=1)
    return cos_t.astype(np.float32), sin_a.astype(np.float32), sin_b.astype(np.float32)


@functools.lru_cache(maxsize=None)
def _filter_features(L):
    t = np.linspace(0.0, 1.0, L)[None, :]
    t_r = np.arange(L, dtype=np.float64)[None, :]
    bands = np.linspace(1e-4, FILTER_BANDS - 1, FILTER_BANDS)[:, None]
    ang = 2.0 * math.pi * bands * t_r / L
    z = np.concatenate([t, np.cos(ang), -np.sin(ang)], axis=0)
    return np.pad(z, ((0, FILTER_HIDDEN - FILTER_EMB), (0, 0))).astype(np.float32)


@functools.lru_cache(maxsize=None)
def _dft_tables(L):
    n = 2 * L
    n2 = DFT_N2
    n1 = n // n2
    n1h = n1 // 2
    k1n = n1h + 1
    k1p = -(-k1n // SUBLANES) * SUBLANES
    two_pi = 2.0 * math.pi
    k1 = np.arange(k1p, dtype=np.int64)[:, None]
    live = (k1 < k1n).astype(np.float64)
    m1 = np.arange(n1h, dtype=np.int64)[None, :]
    ang1 = two_pi * ((k1 * m1) % n1) / n1
    f1m = np.concatenate([np.cos(ang1) * live, -np.sin(ang1) * live], axis=0).astype(BF16)
    m2 = np.arange(n2, dtype=np.int64)[None, :]
    ang2 = two_pi * ((k1 * m2) % n) / n
    tr = (np.cos(ang2) * live).astype(np.float32)
    ti = (-np.sin(ang2) * live).astype(np.float32)
    ang3 = two_pi * ((m2.T * m2) % n2) / n2
    f2r, f2i = np.cos(ang3), -np.sin(ang3)
    f2 = np.block([[f2r, f2i], [-f2i, f2r]]).astype(BF16)
    f2inv = np.block([[f2r, -f2i], [f2i, f2r]]).astype(BF16)
    o1 = np.arange(n1h, dtype=np.int64)[:, None]
    q1 = np.arange(n1h, dtype=np.int64)[None, :]
    ang4 = two_pi * ((o1 * q1) % n1) / n1
    wgt = np.where(q1 == 0, 1.0, 2.0)
    cmat = np.concatenate([wgt * np.cos(ang4), -wgt * np.sin(ang4)], axis=1).astype(BF16)
    sgn = np.broadcast_to(np.where(o1 % 2 == 0, 1.0, -1.0), (n1h, n2)).astype(np.float32)
    return (f1m, tr, ti, f2, f2inv, cmat, sgn), n1h


def _pick(n, pref):
    t = pref
    while n % t:
        t //= 2
    return t


def kernel(x, c, ada_w, ada_b, norm_mix_g, w_in, hy_conv_w, hy_conv_b, filt_w1, filt_b1, filt_freq1, filt_w2, filt_b2, filt_freq2, filt_w3, hy_bias, q_norm_g, kv_norm_g, w_uq, w_uk, w_uv, w_hy_out, w_mla_out, w_mix_out, norm_ffn_g, w_group, b_group, w_router, b_router, w_gate, w_up, w_down, final_norm_g):
    B, L, d = x.shape
    assert B == 1 and d == D_MODEL and ada_w.shape[0] == 1 and L % (2 * DFT_N2) == 0
    x2 = x.reshape(L, d)
    row = lambda v: v.reshape(1, -1)
    col = lambda v: v.reshape(-1, 1)

    mod = _ada(jnp.pad(c, ((0, 8 - B), (0, 0))), ada_w[0], row(ada_b[0]))

    wi = w_in[0]
    kr_slab = jnp.pad(wi[:, OFF_KR:OFF_GATE], ((0, 0), (QK_NOPE, HEAD_SLAB - QK_NOPE - QK_ROPE)))
    wcat = jnp.concatenate([wi[:, OFF_Q:OFF_KV], wi[:, OFF_KV:OFF_KR], wi[:, OFF_GATE:], kr_slab],
                           axis=1).astype(BF16)
    whyt = wi[:, 0:OFF_Q].T.astype(BF16)
    wuq = _pad_heads(w_uq[0], QK_NOPE + QK_ROPE).astype(BF16)
    wuk = _pad_heads(w_uk[0].reshape(KV_LORA, N_HEADS * QK_NOPE), QK_NOPE).astype(BF16)
    wuvt = _pad_heads(w_uv[0].reshape(KV_LORA, N_HEADS * V_HEAD), V_HEAD).T.astype(BF16)
    tkv = _pick(L, 1024)
    cos_t, sin_a, sin_b = _rope_slab_tables(L)

    hypt, gates, q, k, vt, kn2_tiles = _inproj(x2, mod, row(norm_mix_g[0]), wcat, whyt, row(q_norm_g[0]),
                                    row(kv_norm_g[0]), wuq, wuk, wuvt, cos_t, sin_a, sin_b,
                                    _pick(L, 512), tkv)

    zt = _filter_features(L)
    w1t = jnp.pad(filt_w1[0], ((0, FILTER_HIDDEN - FILTER_EMB), (0, 0))).T
    dl = np.abs(np.linspace(MIN_DECAY, MAX_DECAY, HY_WIDTH))[:, None].astype(np.float32)
    ht, nrm = _filt(zt, w1t, col(filt_b1[0]), col(filt_freq1[0]), filt_w2[0].T, col(filt_b2[0]),
                    col(filt_freq2[0]), filt_w3[0].T.astype(BF16), dl, _pick(L, 1024))
    tabs, n1h = _dft_tables(L)
    cw, cbias = hy_conv_w[0], hy_conv_b[0]
    w = HY_WIDTH
    taps = lambda s: [cw[0, s * w:(s + 1) * w], cw[1, s * w:(s + 1) * w], cw[2, s * w:(s + 1) * w],
                      cbias[s * w:(s + 1) * w]]
    prm = jnp.stack(taps(0) + taps(1) + taps(2) + [hy_bias[0], nrm[:, 0]]
                    + [jnp.zeros((w,), F32)] * (HY_PARAM_ROWS - 14), axis=1)
    prm = jnp.broadcast_to(prm[:, :, None], (w, HY_PARAM_ROWS, DFT_N2))
    yht = _hyena(hypt.reshape(3 * w, n1h, DFT_N2), ht.reshape(2 * w, n1h, DFT_N2), prm, tabs,
                 _pick(w, 8), 1.0 / (2 * L)).reshape(w, L)

    kn2 = jnp.broadcast_to(jnp.max(kn2_tiles, axis=0)[:, None, :], (N_HEADS, SUBLANES, LANES))
    ym = _attn(q, k, vt, kn2, _pick(L, 1024), 4)

    wmla = jnp.pad(w_mla_out[0].reshape(N_HEADS, V_HEAD, d),
                   ((0, 0), (0, HEAD_SLAB - V_HEAD), (0, 0))).reshape(QK_SLABS, d).astype(BF16)
    wrt = jnp.pad(jnp.concatenate([w_router[0], w_group[0]], axis=1),
                  ((0, 0), (0, LANES - N_EXPERTS - N_GROUPS)))
    brt = jnp.pad(jnp.concatenate([b_router[0], b_group[0]]), (0, LANES - N_EXPERTS - N_GROUPS))[None, :]
    x1, h2, comb, gsel, gcnt = _merge(x2, yht, ym, gates, mod, w_hy_out[0].astype(BF16), wmla,
                                      w_mix_out[0].astype(BF16), row(norm_ffn_g[0]), wrt, brt,
                                      _pick(L, 512))

    assert L % MOE_TILE == 0
    bg, nbu, roff = _moe_plan(gcnt, L // MOE_TILE)
    wgu = jnp.concatenate([w_gate[0], w_up[0]], axis=2).astype(BF16)
    wd = w_down[0].reshape(N_EXPERTS * D_EXPERT, d).astype(BF16)
    moe = _moe(bg, nbu, roff, h2, comb, gsel, wgu, wd)
    out = _final(x1, moe, mod, row(final_norm_g), _pick(L, 512))
    return out.reshape(B, L, d)
```

```python
import functools
import math

import numpy as np
import jax
import jax.numpy as jnp
from jax import lax
from jax.experimental import pallas as pl
from jax.experimental.pallas import tpu as pltpu

F32 = jnp.float32
BF16 = jnp.bfloat16
HIGHEST = lax.Precision.HIGHEST

D_MODEL = 1024
EPS = 1e-6
HY_WIDTH = 512
FILTER_BANDS = 16
FILTER_EMB = 1 + 2 * FILTER_BANDS
FILTER_HIDDEN = 64
FILTER_DECAY_TARGET = 1e-2
FAST_DECAY_PCT = 0.3
SLOW_DECAY_PCT = 1.5
FILTER_SHIFT = 0.05
MIN_DECAY = math.log(FILTER_DECAY_TARGET) / SLOW_DECAY_PCT
MAX_DECAY = math.log(FILTER_DECAY_TARGET) / FAST_DECAY_PCT
N_HEADS = 8
QK_NOPE = 64
QK_ROPE = 32
V_HEAD = 64
Q_LORA = 384
KV_LORA = 256
ROPE_THETA = 10000.0
MLA_WIDTH = N_HEADS * V_HEAD
OFF_Q = 3 * HY_WIDTH
OFF_KV = OFF_Q + Q_LORA
OFF_KR = OFF_KV + KV_LORA
OFF_GATE = OFF_KR + QK_ROPE
N_GROUPS = 4
EXPERTS_PER_GROUP = 8
N_EXPERTS = N_GROUPS * EXPERTS_PER_GROUP
D_EXPERT = 256

LANES = 128
SUBLANES = 8
HEAD_SLAB = LANES
QK_SLABS = N_HEADS * HEAD_SLAB
ONES_LANE = V_HEAD
DFT_N2 = LANES
HY_PARAM_ROWS = 16
NEG_BIG = -1e30
VMEM_LIMIT = 56 * 1024 * 1024
MAX_STATIC_SHIFT = 60.0
SHIFT_SLACK = 1.002
SHIFT_EPS = 1e-3

C_Q = 0
C_KV = C_Q + Q_LORA
C_GATE = C_KV + KV_LORA
C_KR = C_GATE + 2 * D_MODEL
C_END = C_KR + HEAD_SLAB


def _cparams(*sem):
    return pltpu.CompilerParams(dimension_semantics=sem, vmem_limit_bytes=VMEM_LIMIT)


def _rms(x):
    return x * lax.rsqrt(jnp.mean(x * x, axis=-1, keepdims=True) + EPS)


def _dot_nt(a, b):
    return lax.dot_general(a, b, (((1,), (1,)), ((), ())), preferred_element_type=F32)


def _dot_tn(a, b):
    return lax.dot_general(a, b, (((0,), (0,)), ((), ())), preferred_element_type=F32)


def _ada_kernel(c_ref, w_ref, b_ref, o_ref):
    c = c_ref[...]
    ca = c * jax.nn.sigmoid(c)
    o_ref[...] = jnp.dot(ca, w_ref[...], preferred_element_type=F32, precision=HIGHEST) + b_ref[...]


def _ada(c8, w, b):
    d, n = w.shape
    tn = 1024
    return pl.pallas_call(
        _ada_kernel,
        grid=(n // tn,),
        in_specs=[pl.BlockSpec((8, d), lambda j: (0, 0)),
                  pl.BlockSpec((d, tn), lambda j: (0, j)),
                  pl.BlockSpec((1, tn), lambda j: (0, j))],
        out_specs=pl.BlockSpec((8, tn), lambda j: (0, j)),
        out_shape=jax.ShapeDtypeStruct((8, n), F32),
        compiler_params=_cparams("parallel"),
        name="ada",
    )(c8, w, b)


def _rope_slab(z, cos_t, sin_a, sin_b):
    return z * cos_t + pltpu.roll(z, HEAD_SLAB - QK_ROPE // 2, 1) * sin_a + pltpu.roll(z, QK_ROPE // 2, 1) * sin_b


def _inproj_kernel(x_ref, mod_ref, g_ref, wcat_ref, whyt_ref, gq_ref, gkv_ref, wuq_ref, wuk_ref, wuvt_ref,
                   cos_ref, sina_ref, sinb_ref,
                   hypt_ref, gate_ref, q_ref, k_ref, vt_ref, kn2_ref, *, qscale):
    d = D_MODEL
    x = x_ref[...]
    sh1 = mod_ref[0:1, 0:d]
    sc1 = mod_ref[0:1, d:2 * d]
    hb = (_rms(x) * g_ref[...] * (1.0 + sc1) + sh1).astype(BF16)
    hypt_ref[...] = _dot_nt(whyt_ref[...], hb).astype(BF16)
    proj = jnp.dot(hb, wcat_ref[...], preferred_element_type=F32)
    gate_ref[...] = jax.nn.sigmoid(proj[:, C_GATE:C_KR]).astype(BF16)
    cq = _rms(proj[:, C_Q:C_KV]) * gq_ref[...]
    ckv = (_rms(proj[:, C_KV:C_GATE]) * gkv_ref[...]).astype(BF16)
    q = jnp.dot(cq.astype(BF16), wuq_ref[...], preferred_element_type=F32)
    kn = jnp.dot(ckv, wuk_ref[...], preferred_element_type=F32)
    vt = _dot_nt(wuvt_ref[...], ckv)
    cos_t = cos_ref[...]
    sin_a = sina_ref[...]
    sin_b = sinb_ref[...]
    kr = _rope_slab(proj[:, C_KR:C_END], cos_t, sin_a, sin_b)
    srow = lax.broadcasted_iota(jnp.int32, (HEAD_SLAB, 1), 0)
    ones_row = jnp.where(srow == ONES_LANE, 1.0, 0.0).astype(F32)
    kn2 = []
    for hd in range(N_HEADS):
        sl = slice(hd * HEAD_SLAB, (hd + 1) * HEAD_SLAB)
        q_ref[:, sl] = (_rope_slab(q[:, sl], cos_t, sin_a, sin_b) * qscale).astype(BF16)
        kb = (kn[:, sl] + kr).astype(BF16)
        k_ref[:, sl] = kb
        vt_ref[hd, 0] = (vt[sl] + ones_row).astype(BF16)
        kf = kb.astype(F32)
        big = jnp.max(jnp.sum(kf * kf, axis=1, keepdims=True), axis=0, keepdims=True)
        kn2.append(jnp.broadcast_to(big, (1, HEAD_SLAB)))
    kn2_ref[0] = jnp.concatenate(kn2, axis=0)


def _inproj(x2, mod, g1, wcat, whyt, gq, gkv, wuq, wuk, wuvt, cos_t, sin_a, sin_b, tm, tkv):
    L, d = x2.shape
    qscale = (QK_NOPE + QK_ROPE) ** -0.5 * math.log2(math.e)
    full = lambda a: pl.BlockSpec(a.shape, lambda i: (0,) * a.ndim)
    row = lambda w: pl.BlockSpec((tm, w), lambda i: (i, 0))
    per = tkv // tm
    return pl.pallas_call(
        functools.partial(_inproj_kernel, qscale=qscale),
        grid=(L // tm,),
        in_specs=[row(d), full(mod), full(g1), full(wcat), full(whyt), full(gq), full(gkv), full(wuq),
                  full(wuk), full(wuvt), row(HEAD_SLAB), row(HEAD_SLAB), row(HEAD_SLAB)],
        out_specs=[pl.BlockSpec((3 * HY_WIDTH, tm), lambda i: (0, i)),
                   row(2 * D_MODEL), row(QK_SLABS), row(QK_SLABS),
                   pl.BlockSpec((N_HEADS, 1, HEAD_SLAB, tm), lambda i: (0, i // per, 0, i % per)),
                   pl.BlockSpec((1, N_HEADS, HEAD_SLAB), lambda i: (i, 0, 0))],
        out_shape=[jax.ShapeDtypeStruct((3 * HY_WIDTH, L), BF16),
                   jax.ShapeDtypeStruct((L, 2 * D_MODEL), BF16),
                   jax.ShapeDtypeStruct((L, QK_SLABS), BF16),
                   jax.ShapeDtypeStruct((L, QK_SLABS), BF16),
                   jax.ShapeDtypeStruct((N_HEADS, L // tkv, HEAD_SLAB, tkv), BF16),
                   jax.ShapeDtypeStruct((L // tm, N_HEADS, HEAD_SLAB), F32)],
        compiler_params=_cparams("parallel"),
        name="inproj",
    )(x2, mod, g1, wcat, whyt, gq, gkv, wuq, wuk, wuvt, cos_t, sin_a, sin_b)


def _filt_kernel(z_ref, w1_ref, b1_ref, f1_ref, w2_ref, b2_ref, f2_ref, w3_ref, dl_ref,
                 h_ref, nrm_ref):
    i = pl.program_id(0)
    z = z_ref[...]
    tm = z.shape[1]
    h = jnp.sin(f1_ref[...] * (jnp.dot(w1_ref[...], z, preferred_element_type=F32, precision=HIGHEST)
                               + b1_ref[...]))
    h = jnp.sin(f2_ref[...] * (jnp.dot(w2_ref[...], h, preferred_element_type=F32, precision=HIGHEST)
                               + b2_ref[...]))
    h3 = jnp.dot(w3_ref[...], h.astype(BF16), preferred_element_type=F32)
    t = z[0:1, :]
    window = jnp.exp(-dl_ref[...] * t) + FILTER_SHIFT
    hf = h3[0:HY_WIDTH] * window
    hb = h3[HY_WIDTH:2 * HY_WIDTH] * window
    col = lax.broadcasted_iota(jnp.int32, (1, tm), 1)
    hb = jnp.where((col == 0) & (i == 0), 0.0, hb)
    h_ref[0:HY_WIDTH, :] = hf.astype(BF16)
    h_ref[HY_WIDTH:2 * HY_WIDTH, :] = hb.astype(BF16)
    part = jnp.sum(jnp.abs(hf) + jnp.abs(hb), axis=1, keepdims=True)

    @pl.when(i == 0)
    def _():
        nrm_ref[...] = jnp.zeros_like(nrm_ref)

    nrm_ref[...] += part


def _filt(zt, w1t, b1, f1, w2t, b2, f2, w3t, dl, tm):
    L = zt.shape[1]
    full = lambda a: pl.BlockSpec(a.shape, lambda i: (0,) * a.ndim)
    return pl.pallas_call(
        _filt_kernel,
        grid=(L // tm,),
        in_specs=[pl.BlockSpec((zt.shape[0], tm), lambda i: (0, i)), full(w1t), full(b1), full(f1),
                  full(w2t), full(b2), full(f2), full(w3t), full(dl)],
        out_specs=[pl.BlockSpec((2 * HY_WIDTH, tm), lambda i: (0, i)),
                   pl.BlockSpec((HY_WIDTH, 1), lambda i: (0, 0))],
        out_shape=[jax.ShapeDtypeStruct((2 * HY_WIDTH, L), BF16),
                   jax.ShapeDtypeStruct((HY_WIDTH, 1), F32)],
        compiler_params=_cparams("arbitrary"),
        name="filt",
    )(zt, w1t, b1, f1, w2t, b2, f2, w3t, dl)


def _hyena_kernel(x0_ref, x1_ref, v_ref, hf_ref, hb_ref, prm_ref, f1_ref, tr_ref, ti_ref, f2_ref,
                  f2i_ref, cm_ref, sg_ref, o_ref, *, inv_n):
    cb, n1h, w = x0_ref.shape
    k1p = tr_ref.shape[0]
    lane = lax.broadcasted_iota(jnp.int32, (n1h, w), 1)
    rowi = lax.broadcasted_iota(jnp.int32, (n1h, w), 0)
    tr = tr_ref[...]
    ti = ti_ref[...]

    def conv(x, p, j):
        r = pltpu.roll(x, 1, 1)
        up = jnp.where(lane == 0, jnp.where(rowi == 0, 0.0, pltpu.roll(r, 1, 0)), r)
        l = pltpu.roll(x, w - 1, 1)
        dn = jnp.where(lane == w - 1, jnp.where(rowi == n1h - 1, 0.0, pltpu.roll(l, n1h - 1, 0)), l)
        return up * p[j:j + 1] + x * p[j + 1:j + 2] + dn * p[j + 2:j + 3] + p[j + 3:j + 4]

    def twiddle(a, s):
        ar = a[0:k1p, s * w:(s + 1) * w]
        ai = a[k1p:2 * k1p, s * w:(s + 1) * w]
        return jnp.concatenate([ar * tr - ai * ti, ar * ti + ai * tr], axis=1).astype(BF16)

    prm = [prm_ref[c] for c in range(cb)]
    x0s, us, sigs = [], [], []
    for c in range(cb):
        p = prm[c]
        x0s.append(conv(x0_ref[c].astype(F32), p, 0))
        u = conv(v_ref[c].astype(F32), p, 8) * conv(x1_ref[c].astype(F32), p, 4)
        us.append(u)
        sigs += [u.astype(BF16), hf_ref[c], hb_ref[c]]
    a = jnp.dot(f1_ref[...], jnp.concatenate(sigs, axis=1), preferred_element_type=F32)
    b = jnp.concatenate([twiddle(a, s) for s in range(3 * cb)], axis=0)
    s = jnp.dot(b, f2_ref[...], preferred_element_type=F32)
    ys = []
    for c in range(cb):
        su = s[(3 * c) * k1p:(3 * c + 1) * k1p]
        sf = s[(3 * c + 1) * k1p:(3 * c + 2) * k1p]
        sb = s[(3 * c + 2) * k1p:(3 * c + 3) * k1p]
        inv = 1.0 / prm[c][13:14]
        gr = (sf[:, 0:w] + sb[:, 0:w]) * inv
        gi = (sf[:, w:2 * w] - sb[:, w:2 * w]) * inv
        ur, ui = su[:, 0:w], su[:, w:2 * w]
        ys.append(jnp.concatenate([ur * gr - ui * gi, ur * gi + ui * gr], axis=1).astype(BF16))
    zt = jnp.dot(jnp.concatenate(ys, axis=0), f2i_ref[...], preferred_element_type=F32)
    zss, nyq = [], []
    for c in range(cb):
        ztr = zt[c * k1p:(c + 1) * k1p, 0:w]
        zti = zt[c * k1p:(c + 1) * k1p, w:2 * w]
        zr = ztr * tr + zti * ti
        zi = zti * tr - ztr * ti
        zss.append(jnp.concatenate([zr[0:n1h], zi[0:n1h]], axis=0).astype(BF16))
        nyq.append(zr[n1h:n1h + 1])
    yy = jnp.dot(cm_ref[...], jnp.concatenate(zss, axis=1), preferred_element_type=F32)
    for c in range(cb):
        conv_out = (yy[:, c * w:(c + 1) * w] + sg_ref[...] * nyq[c]) * inv_n
        o_ref[c] = (x0s[c] * (conv_out + prm[c][12:13] * us[c])).astype(BF16)


def _hyena(hyp3, h3, prm, tabs, cb, inv_n):
    c3, n1h, w = hyp3.shape
    c = c3 // 3
    nb = c // cb
    f1m, tr, ti, f2, f2i, cmat, sgn = tabs
    full = lambda a: pl.BlockSpec(a.shape, lambda i: (0,) * a.ndim)
    blk = lambda off: pl.BlockSpec((cb, n1h, w), lambda i: (off * nb + i, 0, 0))
    return pl.pallas_call(
        functools.partial(_hyena_kernel, inv_n=inv_n),
        grid=(nb,),
        in_specs=[blk(0), blk(1), blk(2), blk(0), blk(1),
                  pl.BlockSpec((cb, HY_PARAM_ROWS, w), lambda i: (i, 0, 0)),
                  full(f1m), full(tr), full(ti), full(f2), full(f2i), full(cmat), full(sgn)],
        out_specs=pl.BlockSpec((cb, n1h, w), lambda i: (i, 0, 0)),
        out_shape=jax.ShapeDtypeStruct((c, n1h, w), BF16),
        compiler_params=_cparams("parallel"),
        name="hyena",
    )(hyp3, hyp3, hyp3, h3, h3, prm, f1m, tr, ti, f2, f2i, cmat, sgn)


def _attn_kernel(q_ref, k_ref, vt_ref, kn2_ref, o_ref, *, unroll):
    tq = q_ref.shape[0]
    nk, _, tk = vt_ref.shape[1:]
    q = q_ref[...]

    def finish(acc):
        o_ref[...] = (acc / acc[ONES_LANE:ONES_LANE + 1, :]).astype(BF16)

    qf = q.astype(F32)
    qn2 = lax.dot_general(jnp.ones((SUBLANES, HEAD_SLAB), F32), qf * qf, (((1,), (1,)), ((), ())),
                          precision=HIGHEST, preferred_element_type=F32)[0:1]
    bound = jnp.sqrt(qn2 * kn2_ref[0][0:1, 0:1]) * SHIFT_SLACK + SHIFT_EPS
    static_shift_ok = jnp.max(bound) <= MAX_STATIC_SHIFT

    @pl.when(static_shift_ok)
    def _():
        def body(j, acc):
            off = pl.multiple_of(j * tk, tk)
            st = _dot_nt(k_ref[pl.ds(off, tk), :], q)
            p = jnp.exp2(st - bound).astype(BF16)
            return acc + jnp.dot(vt_ref[0, j], p, preferred_element_type=F32)

        finish(lax.fori_loop(0, nk, body, jnp.zeros((HEAD_SLAB, tq), F32), unroll=unroll))

    @pl.when(jnp.logical_not(static_shift_ok))
    def _():
        def body(j, carry):
            m, acc = carry
            off = pl.multiple_of(j * tk, tk)
            st = _dot_nt(k_ref[pl.ds(off, tk), :], q)
            m_new = jnp.maximum(m, jnp.max(st, axis=0, keepdims=True))
            alpha = jnp.exp2(m - m_new)
            p = jnp.exp2(st - m_new)
            acc = alpha * acc + jnp.dot(vt_ref[0, j], p.astype(BF16), preferred_element_type=F32)
            return m_new, acc

        m0 = jnp.full((1, tq), NEG_BIG, F32)
        acc0 = jnp.zeros((HEAD_SLAB, tq), F32)
        finish(lax.fori_loop(0, nk, body, (m0, acc0), unroll=unroll)[1])


def _attn(q, k, vt, kn2, tq, unroll):
    L = q.shape[0]
    _, nk, _, tk = vt.shape
    return pl.pallas_call(
        functools.partial(_attn_kernel, unroll=unroll),
        grid=(N_HEADS, L // tq),
        in_specs=[pl.BlockSpec((tq, HEAD_SLAB), lambda h, i: (i, h)),
                  pl.BlockSpec((L, HEAD_SLAB), lambda h, i: (0, h)),
                  pl.BlockSpec((1, nk, HEAD_SLAB, tk), lambda h, i: (h, 0, 0, 0)),
                  pl.BlockSpec((1, SUBLANES, LANES), lambda h, i: (h, 0, 0))],
        out_specs=pl.BlockSpec((HEAD_SLAB, tq), lambda h, i: (h, i)),
        out_shape=jax.ShapeDtypeStruct((QK_SLABS, L), BF16),
        compiler_params=_cparams("parallel", "parallel"),
        name="attn",
    )(q, k, vt, kn2)


def _merge_kernel(x_ref, yht_ref, ymt_ref, gate_ref, mod_ref, why_ref, wmla_ref, wmix_ref, g2_ref,
                  wrt_ref, brt_ref, x1_ref, h2_ref, comb_ref, gsel_ref, gcnt_ref):
    d = D_MODEL
    a = _dot_tn(yht_ref[...], why_ref[...])
    b = _dot_tn(ymt_ref[...], wmla_ref[...])
    mix = gate_ref[:, 0:d].astype(F32) * a + gate_ref[:, d:2 * d].astype(F32) * b
    o = jnp.dot(mix.astype(BF16), wmix_ref[...], preferred_element_type=F32)
    gt1 = mod_ref[0:1, 2 * d:3 * d]
    sh2 = mod_ref[0:1, 3 * d:4 * d]
    sc2 = mod_ref[0:1, 4 * d:5 * d]
    x1 = x_ref[...] + gt1 * o
    x1_ref[...] = x1
    h2 = _rms(x1) * g2_ref[...] * (1.0 + sc2) + sh2
    h2_hi = h2.astype(BF16)
    h2_ref[...] = h2_hi
    h2_lo = (h2 - h2_hi.astype(F32)).astype(BF16)
    wrt = wrt_ref[...]
    w_hi = wrt.astype(BF16)
    w_lo = (wrt - w_hi.astype(F32)).astype(BF16)
    e = (jnp.dot(h2_hi, w_hi, preferred_element_type=F32)
         + (jnp.dot(h2_lo, w_hi, preferred_element_type=F32)
            + jnp.dot(h2_hi, w_lo, preferred_element_type=F32))) + brt_ref[...]
    lane = lax.broadcasted_iota(jnp.int32, e.shape, 1).astype(F32)
    gvalid = (lane >= N_EXPERTS) & (lane < N_EXPERTS + N_GROUPS)
    glm = jnp.where(gvalid, e, NEG_BIG)
    gmax = jnp.max(glm, axis=1, keepdims=True)
    gidx = jnp.min(jnp.where(glm == gmax, lane, float(LANES)), axis=1, keepdims=True) - float(N_EXPERTS)
    psum = jnp.sum(jnp.where(gvalid, jnp.exp(glm - gmax), 0.0), axis=1, keepdims=True)
    p_sel = 1.0 / psum
    lo = gidx * EXPERTS_PER_GROUP
    em = jnp.where((lane >= lo) & (lane < lo + EXPERTS_PER_GROUP), e, NEG_BIG)
    v1 = jnp.max(em, axis=1, keepdims=True)
    i1 = jnp.min(jnp.where(em == v1, lane, float(LANES)), axis=1, keepdims=True)
    em2 = jnp.where(lane == i1, NEG_BIG, em)
    v2 = jnp.max(em2, axis=1, keepdims=True)
    i2 = jnp.min(jnp.where(em2 == v2, lane, float(LANES)), axis=1, keepdims=True)
    t = jnp.exp(v2 - v1)
    w1 = p_sel / (1.0 + t)
    w2 = p_sel * t / (1.0 + t)
    comb_ref[...] = jnp.where(lane == i1, w1, 0.0) + jnp.where(lane == i2, w2, 0.0)
    gsel = jnp.where(lane == gidx, 1.0, 0.0)
    gsel_ref[...] = gsel.astype(BF16)
    srow = lax.broadcasted_iota(jnp.int32, (SUBLANES, LANES), 0)
    gcnt_ref[0] = jnp.where(srow == 0, jnp.sum(gsel, axis=0, keepdims=True), 0.0)


def _merge(x2, yht, ym, gates, mod, why, wmla, wmix, g2, wrt, brt, tm):
    L, d = x2.shape
    full = lambda a: pl.BlockSpec(a.shape, lambda i: (0,) * a.ndim)
    row = lambda w: pl.BlockSpec((tm, w), lambda i: (i, 0))
    return pl.pallas_call(
        _merge_kernel,
        grid=(L // tm,),
        in_specs=[row(d), pl.BlockSpec((HY_WIDTH, tm), lambda i: (0, i)),
                  pl.BlockSpec((QK_SLABS, tm), lambda i: (0, i)), row(2 * d),
                  full(mod), full(why), full(wmla), full(wmix), full(g2), full(wrt), full(brt)],
        out_specs=[row(d), row(d), row(LANES), row(LANES),
                   pl.BlockSpec((1, SUBLANES, LANES), lambda i: (i, 0, 0))],
        out_shape=[jax.ShapeDtypeStruct((L, d), F32),
                   jax.ShapeDtypeStruct((L, d), BF16),
                   jax.ShapeDtypeStruct((L, LANES), F32),
                   jax.ShapeDtypeStruct((L, LANES), BF16),
                   jax.ShapeDtypeStruct((L // tm, SUBLANES, LANES), F32)],
        compiler_params=_cparams("parallel"),
        name="merge",
    )(x2, yht, ym, gates, mod, why, wmla, wmix, g2, wrt, brt)


MOE_TILE = 2048
MOE_ROWS = 256
MOE_BLOCKS = MOE_TILE // MOE_ROWS + N_GROUPS
POS_RADIX = 64


def _moe_kernel(bg_ref, nbu_ref, roff_ref, h_ref, comb_ref, gsel_ref, wgu_ref, wd_ref, o_ref,
                pos_ref, cs_ref, acc_ref):
    i = pl.program_id(0)
    b = pl.program_id(1)
    t = h_ref.shape[0]
    lane = lax.broadcasted_iota(jnp.int32, (1, LANES), 1)

    @pl.when(b == 0)
    def _():
        oh = gsel_ref[...]
        sub = MOE_ROWS
        tri = jnp.where(lax.broadcasted_iota(jnp.int32, (sub, sub), 1)
                        < lax.broadcasted_iota(jnp.int32, (sub, sub), 0), 1.0, 0.0).astype(BF16)
        carry = jnp.zeros((1, LANES), F32)
        cums = []
        for s in range(t // sub):
            blk = oh[s * sub:(s + 1) * sub]
            cums.append(jnp.dot(tri, blk, preferred_element_type=F32) + carry)
            carry = carry + jnp.sum(blk.astype(F32), axis=0, keepdims=True)
        cum = jnp.concatenate(cums, axis=0)
        offs = jnp.zeros((1, LANES), F32)
        for g in range(N_GROUPS):
            offs = jnp.where(lane == g, roff_ref[i * N_GROUPS + g].astype(F32), offs)
        pos = jnp.sum(jnp.where(oh > 0, cum + offs, 0.0), axis=1, keepdims=True)
        hi = jnp.floor(pos * (1.0 / POS_RADIX))
        lo = pos - hi * POS_RADIX
        digits = jnp.where(lane == 0, hi, jnp.where(lane == 1, lo, 0.0)).astype(BF16)
        srow = lax.broadcasted_iota(jnp.int32, (SUBLANES, LANES), 0)
        scol = lax.broadcasted_iota(jnp.int32, (SUBLANES, LANES), 1)
        pick = jnp.where(srow == scol, 1.0, 0.0).astype(BF16)
        rows = _dot_nt(pick, digits)
        pos_ref[...] = jnp.broadcast_to(rows[0:1] * POS_RADIX + rows[1:2], pos_ref.shape)
        comb = comb_ref[...]
        comb_hi = comb.astype(BF16)
        cs_ref[:, 0:LANES] = comb_hi
        cs_ref[:, LANES:2 * LANES] = (comb - comb_hi.astype(F32)).astype(BF16)
        acc_ref[...] = jnp.zeros_like(acc_ref)

    @pl.when(b < nbu_ref[i])
    def _():
        g = bg_ref[i * MOE_BLOCKS + b]
        rid = (lax.broadcasted_iota(jnp.int32, (MOE_ROWS, 1), 0) + b * MOE_ROWS).astype(F32)
        sel = jnp.where(rid == pos_ref[0:1, :], 1.0, 0.0).astype(BF16)
        xs = jnp.dot(sel, h_ref[...], preferred_element_type=F32).astype(BF16)
        cs = jnp.dot(sel, cs_ref[...], preferred_element_type=F32)
        comb_s = cs[:, 0:LANES] + cs[:, LANES:2 * LANES]
        parts = []
        for j in range(EXPERTS_PER_GROUP):
            gu = jnp.dot(xs, wgu_ref[j], preferred_element_type=F32)
            gt = gu[:, 0:D_EXPERT]
            a = gt * jax.nn.sigmoid(gt) * gu[:, D_EXPERT:2 * D_EXPERT]
            col = jnp.sum(jnp.where(lane == g * EXPERTS_PER_GROUP + j, comb_s, 0.0), axis=1, keepdims=True)
            parts.append((a * col).astype(BF16))
        y = jnp.dot(jnp.concatenate(parts, axis=1), wd_ref[...], preferred_element_type=F32)
        acc_ref[...] += _dot_tn(sel, y.astype(BF16))

    @pl.when(b == pl.num_programs(1) - 1)
    def _():
        o_ref[...] = acc_ref[...].astype(BF16)


def _moe(bg, nbu, roff, h2, comb, gsel, wgu, wd):
    L, d = h2.shape
    t = MOE_TILE
    once = dict(pipeline_mode=pl.Buffered(1))
    wsel = lambda i, b, bg_r, nbu_r, roff_r: (bg_r[i * MOE_BLOCKS + b], 0, 0)
    grid_spec = pltpu.PrefetchScalarGridSpec(
        num_scalar_prefetch=3,
        grid=(L // t, MOE_BLOCKS),
        in_specs=[pl.BlockSpec((t, d), lambda i, b, *_: (i, 0), **once),
                  pl.BlockSpec((t, LANES), lambda i, b, *_: (i, 0), **once),
                  pl.BlockSpec((t, LANES), lambda i, b, *_: (i, 0), **once),
                  pl.BlockSpec((EXPERTS_PER_GROUP, d, 2 * D_EXPERT), wsel),
                  pl.BlockSpec((EXPERTS_PER_GROUP * D_EXPERT, d), lambda i, b, bg_r, *_: (bg_r[i * MOE_BLOCKS + b], 0))],
        out_specs=pl.BlockSpec((t, d), lambda i, b, *_: (i, 0)),
        scratch_shapes=[pltpu.VMEM((SUBLANES, t), F32),
                        pltpu.VMEM((t, 2 * LANES), BF16),
                        pltpu.VMEM((t, d), F32)])
    return pl.pallas_call(
        _moe_kernel,
        grid_spec=grid_spec,
        out_shape=jax.ShapeDtypeStruct((L, d), BF16),
        compiler_params=_cparams("parallel", "arbitrary"),
        name="moe",
    )(bg, nbu, roff, h2, comb, gsel, wgu, wd)


def _moe_plan(gcnt, n_tiles):
    cnt = gcnt[:, 0, 0:N_GROUPS].reshape(n_tiles, -1, N_GROUPS).sum(axis=1).astype(jnp.int32)
    nblk = (cnt + MOE_ROWS - 1) // MOE_ROWS
    bend = jnp.cumsum(nblk, axis=1)
    roff = (bend - nblk) * MOE_ROWS
    nbu = bend[:, -1]
    bidx = jnp.arange(MOE_BLOCKS, dtype=jnp.int32)[None, :]
    grp = jnp.sum((bidx[:, :, None] >= bend[:, None, :]).astype(jnp.int32), axis=2)
    last = jnp.take_along_axis(grp, jnp.maximum(nbu - 1, 0)[:, None], axis=1)
    bg = jnp.where(bidx < nbu[:, None], grp, last)
    return bg.reshape(-1), nbu, roff.reshape(-1)


def _final_kernel(x1_ref, m_ref, mod_ref, fg_ref, o_ref):
    d = D_MODEL
    x2 = x1_ref[...] + mod_ref[0:1, 5 * d:6 * d] * m_ref[...].astype(F32)
    o_ref[...] = _rms(x2) * fg_ref[...]


def _final(x1, m, mod, fg, tm):
    L, d = x1.shape
    return pl.pallas_call(
        _final_kernel,
        grid=(L // tm,),
        in_specs=[pl.BlockSpec((tm, d), lambda i: (i, 0)), pl.BlockSpec((tm, d), lambda i: (i, 0)),
                  pl.BlockSpec(mod.shape, lambda i: (0, 0)), pl.BlockSpec(fg.shape, lambda i: (0, 0))],
        out_specs=pl.BlockSpec((tm, d), lambda i: (i, 0)),
        out_shape=jax.ShapeDtypeStruct((L, d), F32),
        compiler_params=_cparams("parallel"),
        name="final",
    )(x1, m, mod, fg)


def _pad_heads(w, width):
    lead = w.shape[:-1]
    w = w.reshape(lead + (N_HEADS, width))
    w = jnp.pad(w, [(0, 0)] * len(lead) + [(0, 0), (0, HEAD_SLAB - width)])
    return w.reshape(lead + (N_HEADS * HEAD_SLAB,))


@functools.lru_cache(maxsize=None)
def _rope_slab_tables(L):
    half = QK_ROPE // 2
    pos = np.arange(L, dtype=np.float64)
    inv = ROPE_THETA ** (-np.arange(0, QK_ROPE, 2, dtype=np.float64) / QK_ROPE)
    ang = pos[:, None] * inv[None, :]
    cos, sin = np.cos(ang), np.sin(ang)
    z = lambda w: np.zeros((L, w))
    cos_t = np.concatenate([np.ones((L, QK_NOPE)), cos, cos, z(HEAD_SLAB - QK_NOPE - QK_ROPE)], axis=1)
    sin_a = np.concatenate([z(QK_NOPE), -sin, z(HEAD_SLAB - QK_NOPE - half)], axis=1)
    sin_b = np.concatenate([z(QK_NOPE + half), sin, z(HEAD_SLAB - QK_NOPE - QK_ROPE)], axis=1)
    return cos_t.astype(np.float32), sin_a.astype(np.float32), sin_b.astype(np.float32)


@functools.lru_cache(maxsize=None)
def _filter_features(L):
    t = np.linspace(0.0, 1.0, L)[None, :]
    t_r = np.arange(L, dtype=np.float64)[None, :]
    bands = np.linspace(1e-4, FILTER_BANDS - 1, FILTER_BANDS)[:, None]
    ang = 2.0 * math.pi * bands * t_r / L
    z = np.concatenate([t, np.cos(ang), -np.sin(ang)], axis=0)
    return np.pad(z, ((0, FILTER_HIDDEN - FILTER_EMB), (0, 0))).astype(np.float32)


@functools.lru_cache(maxsize=None)
def _dft_tables(L):
    n = 2 * L
    n2 = DFT_N2
    n1 = n // n2
    n1h = n1 // 2
    k1n = n1h + 1
    k1p = -(-k1n // SUBLANES) * SUBLANES
    two_pi = 2.0 * math.pi
    k1 = np.arange(k1p, dtype=np.int64)[:, None]
    live = (k1 < k1n).astype(np.float64)
    m1 = np.arange(n1h, dtype=np.int64)[None, :]
    ang1 = two_pi * ((k1 * m1) % n1) / n1
    f1m = np.concatenate([np.cos(ang1) * live, -np.sin(ang1) * live], axis=0).astype(BF16)
    m2 = np.arange(n2, dtype=np.int64)[None, :]
    ang2 = two_pi * ((k1 * m2) % n) / n
    tr = (np.cos(ang2) * live).astype(np.float32)
    ti = (-np.sin(ang2) * live).astype(np.float32)
    ang3 = two_pi * ((m2.T * m2) % n2) / n2
    f2r, f2i = np.cos(ang3), -np.sin(ang3)
    f2 = np.block([[f2r, f2i], [-f2i, f2r]]).astype(BF16)
    f2inv = np.block([[f2r, -f2i], [f2i, f2r]]).astype(BF16)
    o1 = np.arange(n1h, dtype=np.int64)[:, None]
    q1 = np.arange(n1h, dtype=np.int64)[None, :]
    ang4 = two_pi * ((o1 * q1) % n1) / n1
    wgt = np.where(q1 == 0, 1.0, 2.0)
    cmat = np.concatenate([wgt * np.cos(ang4), -wgt * np.sin(ang4)], axis=1).astype(BF16)
    sgn = np.broadcast_to(np.where(o1 % 2 == 0, 1.0, -1.0), (n1h, n2)).astype(np.float32)
    return (f1m, tr, ti, f2, f2inv, cmat, sgn), n1h


def _pick(n, pref):
    t = pref
    while n % t:
        t //= 2
    return t


def kernel(x, c, ada_w, ada_b, norm_mix_g, w_in, hy_conv_w, hy_conv_b, filt_w1, filt_b1, filt_freq1, filt_w2, filt_b2, filt_freq2, filt_w3, hy_bias, q_norm_g, kv_norm_g, w_uq, w_uk, w_uv, w_hy_out, w_mla_out, w_mix_out, norm_ffn_g, w_group, b_group, w_router, b_router, w_gate, w_up, w_down, final_norm_g):
    B, L, d = x.shape
    assert B == 1 and d == D_MODEL and ada_w.shape[0] == 1 and L % (2 * DFT_N2) == 0
    x2 = x.reshape(L, d)
    row = lambda v: v.reshape(1, -1)
    col = lambda v: v.reshape(-1, 1)

    mod = _ada(jnp.pad(c, ((0, 8 - B), (0, 0))), ada_w[0], row(ada_b[0]))

    wi = w_in[0]
    kr_slab = jnp.pad(wi[:, OFF_KR:OFF_GATE], ((0, 0), (QK_NOPE, HEAD_SLAB - QK_NOPE - QK_ROPE)))
    wcat = jnp.concatenate([wi[:, OFF_Q:OFF_KV], wi[:, OFF_KV:OFF_KR], wi[:, OFF_GATE:], kr_slab],
                           axis=1).astype(BF16)
    whyt = wi[:, 0:OFF_Q].T.astype(BF16)
    wuq = _pad_heads(w_uq[0], QK_NOPE + QK_ROPE).astype(BF16)
    wuk = _pad_heads(w_uk[0].reshape(KV_LORA, N_HEADS * QK_NOPE), QK_NOPE).astype(BF16)
    wuvt = _pad_heads(w_uv[0].reshape(KV_LORA, N_HEADS * V_HEAD), V_HEAD).T.astype(BF16)
    tkv = _pick(L, 1024)
    cos_t, sin_a, sin_b = _rope_slab_tables(L)

    hypt, gates, q, k, vt, kn2_tiles = _inproj(x2, mod, row(norm_mix_g[0]), wcat, whyt, row(q_norm_g[0]),
                                    row(kv_norm_g[0]), wuq, wuk, wuvt, cos_t, sin_a, sin_b,
                                    _pick(L, 512), tkv)

    zt = _filter_features(L)
    w1t = jnp.pad(filt_w1[0], ((0, FILTER_HIDDEN - FILTER_EMB), (0, 0))).T
    dl = np.abs(np.linspace(MIN_DECAY, MAX_DECAY, HY_WIDTH))[:, None].astype(np.float32)
    ht, nrm = _filt(zt, w1t, col(filt_b1[0]), col(filt_freq1[0]), filt_w2[0].T, col(filt_b2[0]),
                    col(filt_freq2[0]), filt_w3[0].T.astype(BF16), dl, _pick(L, 1024))
    tabs, n1h = _dft_tables(L)
    cw, cbias = hy_conv_w[0], hy_conv_b[0]
    w = HY_WIDTH
    taps = lambda s: [cw[0, s * w:(s + 1) * w], cw[1, s * w:(s + 1) * w], cw[2, s * w:(s + 1) * w],
                      cbias[s * w:(s + 1) * w]]
    prm = jnp.stack(taps(0) + taps(1) + taps(2) + [hy_bias[0], nrm[:, 0]]
                    + [jnp.zeros((w,), F32)] * (HY_PARAM_ROWS - 14), axis=1)
    prm = jnp.broadcast_to(prm[:, :, None], (w, HY_PARAM_ROWS, DFT_N2))
    yht = _hyena(hypt.reshape(3 * w, n1h, DFT_N2), ht.reshape(2 * w, n1h, DFT_N2), prm, tabs,
                 _pick(w, 8), 1.0 / (2 * L)).reshape(w, L)

    kn2 = jnp.broadcast_to(jnp.max(kn2_tiles, axis=0)[:, None, :], (N_HEADS, SUBLANES, LANES))
    ym = _attn(q, k, vt, kn2, _pick(L, 1024), 16)

    wmla = jnp.pad(w_mla_out[0].reshape(N_HEADS, V_HEAD, d),
                   ((0, 0), (0, HEAD_SLAB - V_HEAD), (0, 0))).reshape(QK_SLABS, d).astype(BF16)
    wrt = jnp.pad(jnp.concatenate([w_router[0], w_group[0]], axis=1),
                  ((0, 0), (0, LANES - N_EXPERTS - N_GROUPS)))
    brt = jnp.pad(jnp.concatenate([b_router[0], b_group[0]]), (0, LANES - N_EXPERTS - N_GROUPS))[None, :]
    x1, h2, comb, gsel, gcnt = _merge(x2, yht, ym, gates, mod, w_hy_out[0].astype(BF16), wmla,
                                      w_mix_out[0].astype(BF16), row(norm_ffn_g[0]), wrt, brt,
                                      _pick(L, 1024))

    assert L % MOE_TILE == 0
    bg, nbu, roff = _moe_plan(gcnt, L // MOE_TILE)
    wgu = jnp.concatenate([w_gate[0], w_up[0]], axis=2).astype(BF16)
    wd = w_down[0].reshape(N_EXPERTS * D_EXPERT, d).astype(BF16)
    moe = _moe(bg, nbu, roff, h2, comb, gsel, wgu, wd)
    out = _final(x1, moe, mod, row(final_norm_g), _pick(L, 512))
    return out.reshape(B, L, d)
```

```python
import functools
import math

import numpy as np
import jax
import jax.numpy as jnp
from jax import lax
from jax.experimental import pallas as pl
from jax.experimental.pallas import tpu as pltpu

F32 = jnp.float32
BF16 = jnp.bfloat16
HIGHEST = lax.Precision.HIGHEST

D_MODEL = 1024
EPS = 1e-6
HY_WIDTH = 512
FILTER_BANDS = 16
FILTER_EMB = 1 + 2 * FILTER_BANDS
FILTER_HIDDEN = 64
FILTER_DECAY_TARGET = 1e-2
FAST_DECAY_PCT = 0.3
SLOW_DECAY_PCT = 1.5
FILTER_SHIFT = 0.05
MIN_DECAY = math.log(FILTER_DECAY_TARGET) / SLOW_DECAY_PCT
MAX_DECAY = math.log(FILTER_DECAY_TARGET) / FAST_DECAY_PCT
N_HEADS = 8
QK_NOPE = 64
QK_ROPE = 32
V_HEAD = 64
Q_LORA = 384
KV_LORA = 256
ROPE_THETA = 10000.0
MLA_WIDTH = N_HEADS * V_HEAD
OFF_Q = 3 * HY_WIDTH
OFF_KV = OFF_Q + Q_LORA
OFF_KR = OFF_KV + KV_LORA
OFF_GATE = OFF_KR + QK_ROPE
N_GROUPS = 4
EXPERTS_PER_GROUP = 8
N_EXPERTS = N_GROUPS * EXPERTS_PER_GROUP
D_EXPERT = 256

LANES = 128
SUBLANES = 8
HEAD_SLAB = LANES
QK_SLABS = N_HEADS * HEAD_SLAB
ONES_LANE = V_HEAD
DFT_N2 = LANES
HY_PARAM_ROWS = 16
NEG_BIG = -1e30
VMEM_LIMIT = 56 * 1024 * 1024
MAX_STATIC_SHIFT = 60.0
SHIFT_SLACK = 1.01
SHIFT_EPS = 1e-3

C_Q = 0
C_KV = C_Q + Q_LORA
C_GATE = C_KV + KV_LORA
C_KR = C_GATE + 2 * D_MODEL
C_END = C_KR + HEAD_SLAB


def _cparams(*sem):
    return pltpu.CompilerParams(dimension_semantics=sem, vmem_limit_bytes=VMEM_LIMIT)


def _rms(x):
    return x * lax.rsqrt(jnp.mean(x * x, axis=-1, keepdims=True) + EPS)


def _dot_nt(a, b):
    return lax.dot_general(a, b, (((1,), (1,)), ((), ())), preferred_element_type=F32)


def _dot_tn(a, b):
    return lax.dot_general(a, b, (((0,), (0,)), ((), ())), preferred_element_type=F32)


def _ada_kernel(c_ref, w_ref, b_ref, o_ref):
    c = c_ref[...]
    ca = c * jax.nn.sigmoid(c)
    o_ref[...] = jnp.dot(ca, w_ref[...], preferred_element_type=F32, precision=HIGHEST) + b_ref[...]


def _ada(c8, w, b):
    d, n = w.shape
    tn = 1024
    return pl.pallas_call(
        _ada_kernel,
        grid=(n // tn,),
        in_specs=[pl.BlockSpec((8, d), lambda j: (0, 0)),
                  pl.BlockSpec((d, tn), lambda j: (0, j)),
                  pl.BlockSpec((1, tn), lambda j: (0, j))],
        out_specs=pl.BlockSpec((8, tn), lambda j: (0, j)),
        out_shape=jax.ShapeDtypeStruct((8, n), F32),
        compiler_params=_cparams("parallel"),
        name="ada",
    )(c8, w, b)


def _rope_slab(z, cos_t, sin_a, sin_b):
    return z * cos_t + pltpu.roll(z, HEAD_SLAB - QK_ROPE // 2, 1) * sin_a + pltpu.roll(z, QK_ROPE // 2, 1) * sin_b


def _inproj_kernel(x_ref, mod_ref, g_ref, wcat_ref, whyt_ref, gq_ref, gkv_ref, wuq_ref, wuk_ref, wuvt_ref,
                   cos_ref, sina_ref, sinb_ref,
                   hypt_ref, gate_ref, q_ref, k_ref, vt_ref, kn2_ref, *, qscale):
    d = D_MODEL
    x = x_ref[...]
    sh1 = mod_ref[0:1, 0:d]
    sc1 = mod_ref[0:1, d:2 * d]
    hb = (_rms(x) * g_ref[...] * (1.0 + sc1) + sh1).astype(BF16)
    hypt_ref[...] = _dot_nt(whyt_ref[...], hb).astype(BF16)
    proj = jnp.dot(hb, wcat_ref[...], preferred_element_type=F32)
    gate_ref[...] = jax.nn.sigmoid(proj[:, C_GATE:C_KR]).astype(BF16)
    cq = _rms(proj[:, C_Q:C_KV]) * gq_ref[...]
    ckv = (_rms(proj[:, C_KV:C_GATE]) * gkv_ref[...]).astype(BF16)
    q = jnp.dot(cq.astype(BF16), wuq_ref[...], preferred_element_type=F32)
    kn = jnp.dot(ckv, wuk_ref[...], preferred_element_type=F32)
    vt = _dot_nt(wuvt_ref[...], ckv)
    cos_t = cos_ref[...]
    sin_a = sina_ref[...]
    sin_b = sinb_ref[...]
    kr = _rope_slab(proj[:, C_KR:C_END], cos_t, sin_a, sin_b)
    srow = lax.broadcasted_iota(jnp.int32, (HEAD_SLAB, 1), 0)
    ones_row = jnp.where(srow == ONES_LANE, 1.0, 0.0).astype(F32)
    kn2 = []
    for hd in range(N_HEADS):
        sl = slice(hd * HEAD_SLAB, (hd + 1) * HEAD_SLAB)
        q_ref[:, sl] = (_rope_slab(q[:, sl], cos_t, sin_a, sin_b) * qscale).astype(BF16)
        kb = (kn[:, sl] + kr).astype(BF16)
        k_ref[:, sl] = kb
        vt_ref[hd, 0] = (vt[sl] + ones_row).astype(BF16)
        kf = kb.astype(F32)
        big = jnp.max(jnp.sum(kf * kf, axis=1, keepdims=True), axis=0, keepdims=True)
        kn2.append(jnp.broadcast_to(big, (1, HEAD_SLAB)))
    kn2_ref[0] = jnp.concatenate(kn2, axis=0)


def _inproj(x2, mod, g1, wcat, whyt, gq, gkv, wuq, wuk, wuvt, cos_t, sin_a, sin_b, tm, tkv):
    L, d = x2.shape
    qscale = (QK_NOPE + QK_ROPE) ** -0.5 * math.log2(math.e)
    full = lambda a: pl.BlockSpec(a.shape, lambda i: (0,) * a.ndim)
    row = lambda w: pl.BlockSpec((tm, w), lambda i: (i, 0))
    per = tkv // tm
    return pl.pallas_call(
        functools.partial(_inproj_kernel, qscale=qscale),
        grid=(L // tm,),
        in_specs=[row(d), full(mod), full(g1), full(wcat), full(whyt), full(gq), full(gkv), full(wuq),
                  full(wuk), full(wuvt), row(HEAD_SLAB), row(HEAD_SLAB), row(HEAD_SLAB)],
        out_specs=[pl.BlockSpec((3 * HY_WIDTH, tm), lambda i: (0, i)),
                   row(2 * D_MODEL), row(QK_SLABS), row(QK_SLABS),
                   pl.BlockSpec((N_HEADS, 1, HEAD_SLAB, tm), lambda i: (0, i // per, 0, i % per)),
                   pl.BlockSpec((1, N_HEADS, HEAD_SLAB), lambda i: (i, 0, 0))],
        out_shape=[jax.ShapeDtypeStruct((3 * HY_WIDTH, L), BF16),
                   jax.ShapeDtypeStruct((L, 2 * D_MODEL), BF16),
                   jax.ShapeDtypeStruct((L, QK_SLABS), BF16),
                   jax.ShapeDtypeStruct((L, QK_SLABS), BF16),
                   jax.ShapeDtypeStruct((N_HEADS, L // tkv, HEAD_SLAB, tkv), BF16),
                   jax.ShapeDtypeStruct((L // tm, N_HEADS, HEAD_SLAB), F32)],
        compiler_params=_cparams("parallel"),
        name="inproj",
    )(x2, mod, g1, wcat, whyt, gq, gkv, wuq, wuk, wuvt, cos_t, sin_a, sin_b)


def _filt_kernel(z_ref, w1_ref, b1_ref, f1_ref, w2_ref, b2_ref, f2_ref, w3_ref, dl_ref,
                 h_ref, nrm_ref):
    i = pl.program_id(0)
    z = z_ref[...]
    tm = z.shape[1]
    h = jnp.sin(f1_ref[...] * (jnp.dot(w1_ref[...], z, preferred_element_type=F32, precision=HIGHEST)
                               + b1_ref[...]))
    h = jnp.sin(f2_ref[...] * (jnp.dot(w2_ref[...], h, preferred_element_type=F32, precision=HIGHEST)
                               + b2_ref[...]))
    h3 = jnp.dot(w3_ref[...], h.astype(BF16), preferred_element_type=F32)
    t = z[0:1, :]
    window = jnp.exp(-dl_ref[...] * t) + FILTER_SHIFT
    hf = h3[0:HY_WIDTH] * window
    hb = h3[HY_WIDTH:2 * HY_WIDTH] * window
    col = lax.broadcasted_iota(jnp.int32, (1, tm), 1)
    hb = jnp.where((col == 0) & (i == 0), 0.0, hb)
    h_ref[0:HY_WIDTH, :] = hf.astype(BF16)
    h_ref[HY_WIDTH:2 * HY_WIDTH, :] = hb.astype(BF16)
    part = jnp.sum(jnp.abs(hf) + jnp.abs(hb), axis=1, keepdims=True)

    @pl.when(i == 0)
    def _():
        nrm_ref[...] = jnp.zeros_like(nrm_ref)

    nrm_ref[...] += part


def _filt(zt, w1t, b1, f1, w2t, b2, f2, w3t, dl, tm):
    L = zt.shape[1]
    full = lambda a: pl.BlockSpec(a.shape, lambda i: (0,) * a.ndim)
    return pl.pallas_call(
        _filt_kernel,
        grid=(L // tm,),
        in_specs=[pl.BlockSpec((zt.shape[0], tm), lambda i: (0, i)), full(w1t), full(b1), full(f1),
                  full(w2t), full(b2), full(f2), full(w3t), full(dl)],
        out_specs=[pl.BlockSpec((2 * HY_WIDTH, tm), lambda i: (0, i)),
                   pl.BlockSpec((HY_WIDTH, 1), lambda i: (0, 0))],
        out_shape=[jax.ShapeDtypeStruct((2 * HY_WIDTH, L), BF16),
                   jax.ShapeDtypeStruct((HY_WIDTH, 1), F32)],
        compiler_params=_cparams("arbitrary"),
        name="filt",
    )(zt, w1t, b1, f1, w2t, b2, f2, w3t, dl)


def _hyena_kernel(x0_ref, x1_ref, v_ref, hf_ref, hb_ref, prm_ref, f1_ref, tr_ref, ti_ref, f2_ref,
                  f2i_ref, cm_ref, sg_ref, o_ref, *, inv_n):
    cb, n1h, w = x0_ref.shape
    k1p = tr_ref.shape[0]
    lane = lax.broadcasted_iota(jnp.int32, (n1h, w), 1)
    rowi = lax.broadcasted_iota(jnp.int32, (n1h, w), 0)
    tr = tr_ref[...]
    ti = ti_ref[...]

    def conv(x, p, j):
        r = pltpu.roll(x, 1, 1)
        up = jnp.where(lane == 0, jnp.where(rowi == 0, 0.0, pltpu.roll(r, 1, 0)), r)
        l = pltpu.roll(x, w - 1, 1)
        dn = jnp.where(lane == w - 1, jnp.where(rowi == n1h - 1, 0.0, pltpu.roll(l, n1h - 1, 0)), l)
        return up * p[j:j + 1] + x * p[j + 1:j + 2] + dn * p[j + 2:j + 3] + p[j + 3:j + 4]

    def twiddle(a, s):
        ar = a[0:k1p, s * w:(s + 1) * w]
        ai = a[k1p:2 * k1p, s * w:(s + 1) * w]
        return jnp.concatenate([ar * tr - ai * ti, ar * ti + ai * tr], axis=1).astype(BF16)

    prm = [prm_ref[c] for c in range(cb)]
    x0s, us, sigs = [], [], []
    for c in range(cb):
        p = prm[c]
        x0s.append(conv(x0_ref[c].astype(F32), p, 0))
        u = conv(v_ref[c].astype(F32), p, 8) * conv(x1_ref[c].astype(F32), p, 4)
        us.append(u)
        sigs += [u.astype(BF16), hf_ref[c], hb_ref[c]]
    a = jnp.dot(f1_ref[...], jnp.concatenate(sigs, axis=1), preferred_element_type=F32)
    b = jnp.concatenate([twiddle(a, s) for s in range(3 * cb)], axis=0)
    s = jnp.dot(b, f2_ref[...], preferred_element_type=F32)
    ys = []
    for c in range(cb):
        su = s[(3 * c) * k1p:(3 * c + 1) * k1p]
        sf = s[(3 * c + 1) * k1p:(3 * c + 2) * k1p]
        sb = s[(3 * c + 2) * k1p:(3 * c + 3) * k1p]
        inv = 1.0 / prm[c][13:14]
        gr = (sf[:, 0:w] + sb[:, 0:w]) * inv
        gi = (sf[:, w:2 * w] - sb[:, w:2 * w]) * inv
        ur, ui = su[:, 0:w], su[:, w:2 * w]
        ys.append(jnp.concatenate([ur * gr - ui * gi, ur * gi + ui * gr], axis=1).astype(BF16))
    zt = jnp.dot(jnp.concatenate(ys, axis=0), f2i_ref[...], preferred_element_type=F32)
    zss, nyq = [], []
    for c in range(cb):
        ztr = zt[c * k1p:(c + 1) * k1p, 0:w]
        zti = zt[c * k1p:(c + 1) * k1p, w:2 * w]
        zr = ztr * tr + zti * ti
        zi = zti * tr - ztr * ti
        zss.append(jnp.concatenate([zr[0:n1h], zi[0:n1h]], axis=0).astype(BF16))
        nyq.append(zr[n1h:n1h + 1])
    yy = jnp.dot(cm_ref[...], jnp.concatenate(zss, axis=1), preferred_element_type=F32)
    for c in range(cb):
        conv_out = (yy[:, c * w:(c + 1) * w] + sg_ref[...] * nyq[c]) * inv_n
        o_ref[c] = (x0s[c] * (conv_out + prm[c][12:13] * us[c])).astype(BF16)


def _hyena(hyp3, h3, prm, tabs, cb, inv_n):
    c3, n1h, w = hyp3.shape
    c = c3 // 3
    nb = c // cb
    f1m, tr, ti, f2, f2i, cmat, sgn = tabs
    full = lambda a: pl.BlockSpec(a.shape, lambda i: (0,) * a.ndim)
    blk = lambda off: pl.BlockSpec((cb, n1h, w), lambda i: (off * nb + i, 0, 0))
    return pl.pallas_call(
        functools.partial(_hyena_kernel, inv_n=inv_n),
        grid=(nb,),
        in_specs=[blk(0), blk(1), blk(2), blk(0), blk(1),
                  pl.BlockSpec((cb, HY_PARAM_ROWS, w), lambda i: (i, 0, 0)),
                  full(f1m), full(tr), full(ti), full(f2), full(f2i), full(cmat), full(sgn)],
        out_specs=pl.BlockSpec((cb, n1h, w), lambda i: (i, 0, 0)),
        out_shape=jax.ShapeDtypeStruct((c, n1h, w), BF16),
        compiler_params=_cparams("parallel"),
        name="hyena",
    )(hyp3, hyp3, hyp3, h3, h3, prm, f1m, tr, ti, f2, f2i, cmat, sgn)


def _attn_kernel(q_ref, k_ref, vt_ref, kn2_ref, o_ref, *, unroll):
    tq = q_ref.shape[0]
    nk, _, tk = vt_ref.shape[1:]
    q = q_ref[...]

    def finish(acc):
        o_ref[...] = (acc / acc[ONES_LANE:ONES_LANE + 1, :]).astype(BF16)

    qf = q.astype(F32)
    qn2 = _dot_nt(jnp.ones((SUBLANES, HEAD_SLAB), BF16), (qf * qf).astype(BF16))[0:1]
    bound = jnp.sqrt(qn2 * kn2_ref[0][0:1, 0:1]) * SHIFT_SLACK + SHIFT_EPS
    static_shift_ok = jnp.max(bound) <= MAX_STATIC_SHIFT

    @pl.when(static_shift_ok)
    def _():
        def body(j, acc):
            off = pl.multiple_of(j * tk, tk)
            st = _dot_nt(k_ref[pl.ds(off, tk), :], q)
            p = jnp.exp2(st - bound).astype(BF16)
            return acc + jnp.dot(vt_ref[0, j], p, preferred_element_type=F32)

        finish(lax.fori_loop(0, nk, body, jnp.zeros((HEAD_SLAB, tq), F32), unroll=unroll))

    @pl.when(jnp.logical_not(static_shift_ok))
    def _():
        def body(j, carry):
            m, acc = carry
            off = pl.multiple_of(j * tk, tk)
            st = _dot_nt(k_ref[pl.ds(off, tk), :], q)
            m_new = jnp.maximum(m, jnp.max(st, axis=0, keepdims=True))
            alpha = jnp.exp2(m - m_new)
            p = jnp.exp2(st - m_new)
            acc = alpha * acc + jnp.dot(vt_ref[0, j], p.astype(BF16), preferred_element_type=F32)
            return m_new, acc

        m0 = jnp.full((1, tq), NEG_BIG, F32)
        acc0 = jnp.zeros((HEAD_SLAB, tq), F32)
        finish(lax.fori_loop(0, nk, body, (m0, acc0), unroll=unroll)[1])


def _attn(q, k, vt, kn2, tq, unroll):
    L = q.shape[0]
    _, nk, _, tk = vt.shape
    return pl.pallas_call(
        functools.partial(_attn_kernel, unroll=unroll),
        grid=(N_HEADS, L // tq),
        in_specs=[pl.BlockSpec((tq, HEAD_SLAB), lambda h, i: (i, h)),
                  pl.BlockSpec((L, HEAD_SLAB), lambda h, i: (0, h)),
                  pl.BlockSpec((1, nk, HEAD_SLAB, tk), lambda h, i: (h, 0, 0, 0)),
                  pl.BlockSpec((1, SUBLANES, LANES), lambda h, i: (h, 0, 0))],
        out_specs=pl.BlockSpec((HEAD_SLAB, tq), lambda h, i: (h, i)),
        out_shape=jax.ShapeDtypeStruct((QK_SLABS, L), BF16),
        compiler_params=_cparams("parallel", "parallel"),
        name="attn",
    )(q, k, vt, kn2)


def _merge_kernel(x_ref, yht_ref, ymt_ref, gate_ref, mod_ref, why_ref, wmla_ref, wmix_ref, g2_ref,
                  wrt_ref, brt_ref, x1_ref, h2_ref, comb_ref, gsel_ref, gcnt_ref):
    d = D_MODEL
    a = _dot_tn(yht_ref[...], why_ref[...])
    b = _dot_tn(ymt_ref[...], wmla_ref[...])
    mix = gate_ref[:, 0:d].astype(F32) * a + gate_ref[:, d:2 * d].astype(F32) * b
    o = jnp.dot(mix.astype(BF16), wmix_ref[...], preferred_element_type=F32)
    gt1 = mod_ref[0:1, 2 * d:3 * d]
    sh2 = mod_ref[0:1, 3 * d:4 * d]
    sc2 = mod_ref[0:1, 4 * d:5 * d]
    x1 = x_ref[...] + gt1 * o
    x1_ref[...] = x1
    h2 = _rms(x1) * g2_ref[...] * (1.0 + sc2) + sh2
    h2_hi = h2.astype(BF16)
    h2_ref[...] = h2_hi
    h2_lo = (h2 - h2_hi.astype(F32)).astype(BF16)
    wrt = wrt_ref[...]
    w_hi = wrt.astype(BF16)
    w_lo = (wrt - w_hi.astype(F32)).astype(BF16)
    e = (jnp.dot(h2_hi, w_hi, preferred_element_type=F32)
         + (jnp.dot(h2_lo, w_hi, preferred_element_type=F32)
            + jnp.dot(h2_hi, w_lo, preferred_element_type=F32))) + brt_ref[...]
    lane = lax.broadcasted_iota(jnp.int32, e.shape, 1).astype(F32)
    gvalid = (lane >= N_EXPERTS) & (lane < N_EXPERTS + N_GROUPS)
    glm = jnp.where(gvalid, e, NEG_BIG)
    gmax = jnp.max(glm, axis=1, keepdims=True)
    gidx = jnp.min(jnp.where(glm == gmax, lane, float(LANES)), axis=1, keepdims=True) - float(N_EXPERTS)
    psum = jnp.sum(jnp.where(gvalid, jnp.exp(glm - gmax), 0.0), axis=1, keepdims=True)
    p_sel = 1.0 / psum
    lo = gidx * EXPERTS_PER_GROUP
    em = jnp.where((lane >= lo) & (lane < lo + EXPERTS_PER_GROUP), e, NEG_BIG)
    v1 = jnp.max(em, axis=1, keepdims=True)
    i1 = jnp.min(jnp.where(em == v1, lane, float(LANES)), axis=1, keepdims=True)
    em2 = jnp.where(lane == i1, NEG_BIG, em)
    v2 = jnp.max(em2, axis=1, keepdims=True)
    i2 = jnp.min(jnp.where(em2 == v2, lane, float(LANES)), axis=1, keepdims=True)
    t = jnp.exp(v2 - v1)
    w1 = p_sel / (1.0 + t)
    w2 = p_sel * t / (1.0 + t)
    comb_ref[...] = jnp.where(lane == i1, w1, 0.0) + jnp.where(lane == i2, w2, 0.0)
    gsel = jnp.where(lane == gidx, 1.0, 0.0)
    gsel_ref[...] = gsel.astype(BF16)
    srow = lax.broadcasted_iota(jnp.int32, (SUBLANES, LANES), 0)
    gcnt_ref[0] = jnp.where(srow == 0, jnp.sum(gsel, axis=0, keepdims=True), 0.0)


def _merge(x2, yht, ym, gates, mod, why, wmla, wmix, g2, wrt, brt, tm):
    L, d = x2.shape
    full = lambda a: pl.BlockSpec(a.shape, lambda i: (0,) * a.ndim)
    row = lambda w: pl.BlockSpec((tm, w), lambda i: (i, 0))
    return pl.pallas_call(
        _merge_kernel,
        grid=(L // tm,),
        in_specs=[row(d), pl.BlockSpec((HY_WIDTH, tm), lambda i: (0, i)),
                  pl.BlockSpec((QK_SLABS, tm), lambda i: (0, i)), row(2 * d),
                  full(mod), full(why), full(wmla), full(wmix), full(g2), full(wrt), full(brt)],
        out_specs=[row(d), row(d), row(LANES), row(LANES),
                   pl.BlockSpec((1, SUBLANES, LANES), lambda i: (i, 0, 0))],
        out_shape=[jax.ShapeDtypeStruct((L, d), F32),
                   jax.ShapeDtypeStruct((L, d), BF16),
                   jax.ShapeDtypeStruct((L, LANES), F32),
                   jax.ShapeDtypeStruct((L, LANES), BF16),
                   jax.ShapeDtypeStruct((L // tm, SUBLANES, LANES), F32)],
        compiler_params=_cparams("parallel"),
        name="merge",
    )(x2, yht, ym, gates, mod, why, wmla, wmix, g2, wrt, brt)


MOE_TILE = 2048
MOE_ROWS = 256
MOE_BLOCKS = MOE_TILE // MOE_ROWS + N_GROUPS
POS_RADIX = 64


def _moe_kernel(bg_ref, nbu_ref, roff_ref, h_ref, comb_ref, gsel_ref, wgu_ref, wd_ref, o_ref,
                pos_ref, cs_ref, acc_ref):
    i = pl.program_id(0)
    b = pl.program_id(1)
    t = h_ref.shape[0]
    lane = lax.broadcasted_iota(jnp.int32, (1, LANES), 1)

    @pl.when(b == 0)
    def _():
        oh = gsel_ref[...]
        sub = MOE_ROWS
        tri = jnp.where(lax.broadcasted_iota(jnp.int32, (sub, sub), 1)
                        < lax.broadcasted_iota(jnp.int32, (sub, sub), 0), 1.0, 0.0).astype(BF16)
        carry = jnp.zeros((1, LANES), F32)
        cums = []
        for s in range(t // sub):
            blk = oh[s * sub:(s + 1) * sub]
            cums.append(jnp.dot(tri, blk, preferred_element_type=F32) + carry)
            carry = carry + jnp.sum(blk.astype(F32), axis=0, keepdims=True)
        cum = jnp.concatenate(cums, axis=0)
        offs = jnp.zeros((1, LANES), F32)
        for g in range(N_GROUPS):
            offs = jnp.where(lane == g, roff_ref[i * N_GROUPS + g].astype(F32), offs)
        pos = jnp.sum(jnp.where(oh > 0, cum + offs, 0.0), axis=1, keepdims=True)
        hi = jnp.floor(pos * (1.0 / POS_RADIX))
        lo = pos - hi * POS_RADIX
        digits = jnp.where(lane == 0, hi, jnp.where(lane == 1, lo, 0.0)).astype(BF16)
        srow = lax.broadcasted_iota(jnp.int32, (SUBLANES, LANES), 0)
        scol = lax.broadcasted_iota(jnp.int32, (SUBLANES, LANES), 1)
        pick = jnp.where(srow == scol, 1.0, 0.0).astype(BF16)
        rows = _dot_nt(pick, digits)
        pos_ref[...] = jnp.broadcast_to(rows[0:1] * POS_RADIX + rows[1:2], pos_ref.shape)
        comb = comb_ref[...]
        comb_hi = comb.astype(BF16)
        cs_ref[:, 0:LANES] = comb_hi
        cs_ref[:, LANES:2 * LANES] = (comb - comb_hi.astype(F32)).astype(BF16)
        acc_ref[...] = jnp.zeros_like(acc_ref)

    @pl.when(b < nbu_ref[i])
    def _():
        g = bg_ref[i * MOE_BLOCKS + b]
        rid = (lax.broadcasted_iota(jnp.int32, (MOE_ROWS, 1), 0) + b * MOE_ROWS).astype(F32)
        sel = jnp.where(rid == pos_ref[0:1, :], 1.0, 0.0).astype(BF16)
        xs = jnp.dot(sel, h_ref[...], preferred_element_type=F32).astype(BF16)
        cs = jnp.dot(sel, cs_ref[...], preferred_element_type=F32)
        comb_s = cs[:, 0:LANES] + cs[:, LANES:2 * LANES]
        parts = []
        for j in range(EXPERTS_PER_GROUP):
            gu = jnp.dot(xs, wgu_ref[j], preferred_element_type=F32)
            gt = gu[:, 0:D_EXPERT]
            a = gt * jax.nn.sigmoid(gt) * gu[:, D_EXPERT:2 * D_EXPERT]
            col = jnp.sum(jnp.where(lane == g * EXPERTS_PER_GROUP + j, comb_s, 0.0), axis=1, keepdims=True)
            parts.append((a * col).astype(BF16))
        y = jnp.dot(jnp.concatenate(parts, axis=1), wd_ref[...], preferred_element_type=F32)
        acc_ref[...] += _dot_tn(sel, y.astype(BF16))

    @pl.when(b == pl.num_programs(1) - 1)
    def _():
        o_ref[...] = acc_ref[...].astype(BF16)


def _moe(bg, nbu, roff, h2, comb, gsel, wgu, wd):
    L, d = h2.shape
    t = MOE_TILE
    once = dict(pipeline_mode=pl.Buffered(1))
    wsel = lambda i, b, bg_r, nbu_r, roff_r: (bg_r[i * MOE_BLOCKS + b], 0, 0)
    grid_spec = pltpu.PrefetchScalarGridSpec(
        num_scalar_prefetch=3,
        grid=(L // t, MOE_BLOCKS),
        in_specs=[pl.BlockSpec((t, d), lambda i, b, *_: (i, 0), **once),
                  pl.BlockSpec((t, LANES), lambda i, b, *_: (i, 0), **once),
                  pl.BlockSpec((t, LANES), lambda i, b, *_: (i, 0), **once),
                  pl.BlockSpec((EXPERTS_PER_GROUP, d, 2 * D_EXPERT), wsel),
                  pl.BlockSpec((EXPERTS_PER_GROUP * D_EXPERT, d), lambda i, b, bg_r, *_: (bg_r[i * MOE_BLOCKS + b], 0))],
        out_specs=pl.BlockSpec((t, d), lambda i, b, *_: (i, 0)),
        scratch_shapes=[pltpu.VMEM((SUBLANES, t), F32),
                        pltpu.VMEM((t, 2 * LANES), BF16),
                        pltpu.VMEM((t, d), F32)])
    return pl.pallas_call(
        _moe_kernel,
        grid_spec=grid_spec,
        out_shape=jax.ShapeDtypeStruct((L, d), BF16),
        compiler_params=_cparams("parallel", "arbitrary"),
        name="moe",
    )(bg, nbu, roff, h2, comb, gsel, wgu, wd)


def _moe_plan(gcnt, n_tiles):
    cnt = gcnt[:, 0, 0:N_GROUPS].reshape(n_tiles, -1, N_GROUPS).sum(axis=1).astype(jnp.int32)
    nblk = (cnt + MOE_ROWS - 1) // MOE_ROWS
    bend = jnp.cumsum(nblk, axis=1)
    roff = (bend - nblk) * MOE_ROWS
    nbu = bend[:, -1]
    bidx = jnp.arange(MOE_BLOCKS, dtype=jnp.int32)[None, :]
    grp = jnp.sum((bidx[:, :, None] >= bend[:, None, :]).astype(jnp.int32), axis=2)
    last = jnp.take_along_axis(grp, jnp.maximum(nbu - 1, 0)[:, None], axis=1)
    bg = jnp.where(bidx < nbu[:, None], grp, last)
    return bg.reshape(-1), nbu, roff.reshape(-1)


def _final_kernel(x1_ref, m_ref, mod_ref, fg_ref, o_ref):
    d = D_MODEL
    x2 = x1_ref[...] + mod_ref[0:1, 5 * d:6 * d] * m_ref[...].astype(F32)
    o_ref[...] = _rms(x2) * fg_ref[...]


def _final(x1, m, mod, fg, tm):
    L, d = x1.shape
    return pl.pallas_call(
        _final_kernel,
        grid=(L // tm,),
        in_specs=[pl.BlockSpec((tm, d), lambda i: (i, 0)), pl.BlockSpec((tm, d), lambda i: (i, 0)),
                  pl.BlockSpec(mod.shape, lambda i: (0, 0)), pl.BlockSpec(fg.shape, lambda i: (0, 0))],
        out_specs=pl.BlockSpec((tm, d), lambda i: (i, 0)),
        out_shape=jax.ShapeDtypeStruct((L, d), F32),
        compiler_params=_cparams("parallel"),
        name="final",
    )(x1, m, mod, fg)


def _pad_heads(w, width):
    lead = w.shape[:-1]
    w = w.reshape(lead + (N_HEADS, width))
    w = jnp.pad(w, [(0, 0)] * len(lead) + [(0, 0), (0, HEAD_SLAB - width)])
    return w.reshape(lead + (N_HEADS * HEAD_SLAB,))


@functools.lru_cache(maxsize=None)
def _rope_slab_tables(L):
    half = QK_ROPE // 2
    pos = np.arange(L, dtype=np.float64)
    inv = ROPE_THETA ** (-np.arange(0, QK_ROPE, 2, dtype=np.float64) / QK_ROPE)
    ang = pos[:, None] * inv[None, :]
    cos, sin = np.cos(ang), np.sin(ang)
    z = lambda w: np.zeros((L, w))
    cos_t = np.concatenate([np.ones((L, QK_NOPE)), cos, cos, z(HEAD_SLAB - QK_NOPE - QK_ROPE)], axis=1)
    sin_a = np.concatenate([z(QK_NOPE), -sin, z(HEAD_SLAB - QK_NOPE - half)], axis=1)
    sin_b = np.concatenate([z(QK_NOPE + half), sin, z(HEAD_SLAB - QK_NOPE - QK_ROPE)], axis=1)
    return cos_t.astype(np.float32), sin_a.astype(np.float32), sin_b.astype(np.float32)


@functools.lru_cache(maxsize=None)
def _filter_features(L):
    t = np.linspace(0.0, 1.0, L)[None, :]
    t_r = np.arange(L, dtype=np.float64)[None, :]
    bands = np.linspace(1e-4, FILTER_BANDS - 1, FILTER_BANDS)[:, None]
    ang = 2.0 * math.pi * bands * t_r / L
    z = np.concatenate([t, np.cos(ang), -np.sin(ang)], axis=0)
    return np.pad(z, ((0, FILTER_HIDDEN - FILTER_EMB), (0, 0))).astype(np.float32)


@functools.lru_cache(maxsize=None)
def _dft_tables(L):
    n = 2 * L
    n2 = DFT_N2
    n1 = n // n2
    n1h = n1 // 2
    k1n = n1h + 1
    k1p = -(-k1n // SUBLANES) * SUBLANES
    two_pi = 2.0 * math.pi
    k1 = np.arange(k1p, dtype=np.int64)[:, None]
    live = (k1 < k1n).astype(np.float64)
    m1 = np.arange(n1h, dtype=np.int64)[None, :]
    ang1 = two_pi * ((k1 * m1) % n1) / n1
    f1m = np.concatenate([np.cos(ang1) * live, -np.sin(ang1) * live], axis=0).astype(BF16)
    m2 = np.arange(n2, dtype=np.int64)[None, :]
    ang2 = two_pi * ((k1 * m2) % n) / n
    tr = (np.cos(ang2) * live).astype(np.float32)
    ti = (-np.sin(ang2) * live).astype(np.float32)
    ang3 = two_pi * ((m2.T * m2) % n2) / n2
    f2r, f2i = np.cos(ang3), -np.sin(ang3)
    f2 = np.block([[f2r, f2i], [-f2i, f2r]]).astype(BF16)
    f2inv = np.block([[f2r, -f2i], [f2i, f2r]]).astype(BF16)
    o1 = np.arange(n1h, dtype=np.int64)[:, None]
    q1 = np.arange(n1h, dtype=np.int64)[None, :]
    ang4 = two_pi * ((o1 * q1) % n1) / n1
    wgt = np.where(q1 == 0, 1.0, 2.0)
    cmat = np.concatenate([wgt * np.cos(ang4), -wgt * np.sin(ang4)], axis=1).astype(BF16)
    sgn = np.broadcast_to(np.where(o1 % 2 == 0, 1.0, -1.0), (n1h, n2)).astype(np.float32)
    return (f1m, tr, ti, f2, f2inv, cmat, sgn), n1h


def _pick(n, pref):
    t = pref
    while n % t:
        t //= 2
    return t


def kernel(x, c, ada_w, ada_b, norm_mix_g, w_in, hy_conv_w, hy_conv_b, filt_w1, filt_b1, filt_freq1, filt_w2, filt_b2, filt_freq2, filt_w3, hy_bias, q_norm_g, kv_norm_g, w_uq, w_uk, w_uv, w_hy_out, w_mla_out, w_mix_out, norm_ffn_g, w_group, b_group, w_router, b_router, w_gate, w_up, w_down, final_norm_g):
    B, L, d = x.shape
    assert B == 1 and d == D_MODEL and ada_w.shape[0] == 1 and L % (2 * DFT_N2) == 0
    x2 = x.reshape(L, d)
    row = lambda v: v.reshape(1, -1)
    col = lambda v: v.reshape(-1, 1)

    mod = _ada(jnp.pad(c, ((0, 8 - B), (0, 0))), ada_w[0], row(ada_b[0]))

    wi = w_in[0]
    kr_slab = jnp.pad(wi[:, OFF_KR:OFF_GATE], ((0, 0), (QK_NOPE, HEAD_SLAB - QK_NOPE - QK_ROPE)))
    wcat = jnp.concatenate([wi[:, OFF_Q:OFF_KV], wi[:, OFF_KV:OFF_KR], wi[:, OFF_GATE:], kr_slab],
                           axis=1).astype(BF16)
    whyt = wi[:, 0:OFF_Q].T.astype(BF16)
    wuq = _pad_heads(w_uq[0], QK_NOPE + QK_ROPE).astype(BF16)
    wuk = _pad_heads(w_uk[0].reshape(KV_LORA, N_HEADS * QK_NOPE), QK_NOPE).astype(BF16)
    wuvt = _pad_heads(w_uv[0].reshape(KV_LORA, N_HEADS * V_HEAD), V_HEAD).T.astype(BF16)
    tkv = _pick(L, 1024)
    cos_t, sin_a, sin_b = _rope_slab_tables(L)

    hypt, gates, q, k, vt, kn2_tiles = _inproj(x2, mod, row(norm_mix_g[0]), wcat, whyt, row(q_norm_g[0]),
                                    row(kv_norm_g[0]), wuq, wuk, wuvt, cos_t, sin_a, sin_b,
                                    _pick(L, 512), tkv)

    zt = _filter_features(L)
    w1t = jnp.pad(filt_w1[0], ((0, FILTER_HIDDEN - FILTER_EMB), (0, 0))).T
    dl = np.abs(np.linspace(MIN_DECAY, MAX_DECAY, HY_WIDTH))[:, None].astype(np.float32)
    ht, nrm = _filt(zt, w1t, col(filt_b1[0]), col(filt_freq1[0]), filt_w2[0].T, col(filt_b2[0]),
                    col(filt_freq2[0]), filt_w3[0].T.astype(BF16), dl, _pick(L, 1024))
    tabs, n1h = _dft_tables(L)
    cw, cbias = hy_conv_w[0], hy_conv_b[0]
    w = HY_WIDTH
    taps = lambda s: [cw[0, s * w:(s + 1) * w], cw[1, s * w:(s + 1) * w], cw[2, s * w:(s + 1) * w],
                      cbias[s * w:(s + 1) * w]]
    prm = jnp.stack(taps(0) + taps(1) + taps(2) + [hy_bias[0], nrm[:, 0]]
                    + [jnp.zeros((w,), F32)] * (HY_PARAM_ROWS - 14), axis=1)
    prm = jnp.broadcast_to(prm[:, :, None], (w, HY_PARAM_ROWS, DFT_N2))
    yht = _hyena(hypt.reshape(3 * w, n1h, DFT_N2), ht.reshape(2 * w, n1h, DFT_N2), prm, tabs,
                 _pick(w, 8), 1.0 / (2 * L)).reshape(w, L)

    kn2 = jnp.broadcast_to(jnp.max(kn2_tiles, axis=0)[:, None, :], (N_HEADS, SUBLANES, LANES))
    ym = _attn(q, k, vt, kn2, _pick(L, 1024), 16)

    wmla = jnp.pad(w_mla_out[0].reshape(N_HEADS, V_HEAD, d),
                   ((0, 0), (0, HEAD_SLAB - V_HEAD), (0, 0))).reshape(QK_SLABS, d).astype(BF16)
    wrt = jnp.pad(jnp.concatenate([w_router[0], w_group[0]], axis=1),
                  ((0, 0), (0, LANES - N_EXPERTS - N_GROUPS)))
    brt = jnp.pad(jnp.concatenate([b_router[0], b_group[0]]), (0, LANES - N_EXPERTS - N_GROUPS))[None, :]
    x1, h2, comb, gsel, gcnt = _merge(x2, yht, ym, gates, mod, w_hy_out[0].astype(BF16), wmla,
                                      w_mix_out[0].astype(BF16), row(norm_ffn_g[0]), wrt, brt,
                                      _pick(L, 1024))

    assert L % MOE_TILE == 0
    bg, nbu, roff = _moe_plan(gcnt, L // MOE_TILE)
    wgu = jnp.concatenate([w_gate[0], w_up[0]], axis=2).astype(BF16)
    wd = w_down[0].reshape(N_EXPERTS * D_EXPERT, d).astype(BF16)
    moe = _moe(bg, nbu, roff, h2, comb, gsel, wgu, wd)
    out = _final(x1, moe, mod, row(final_norm_g), _pick(L, 512))
    return out.reshape(B, L, d)
```

```python
import functools
import math

import numpy as np
import jax
import jax.numpy as jnp
from jax import lax
from jax.experimental import pallas as pl
from jax.experimental.pallas import tpu as pltpu

F32 = jnp.float32
BF16 = jnp.bfloat16
HIGHEST = lax.Precision.HIGHEST

D_MODEL = 1024
EPS = 1e-6
HY_WIDTH = 512
FILTER_BANDS = 16
FILTER_EMB = 1 + 2 * FILTER_BANDS
FILTER_HIDDEN = 64
FILTER_DECAY_TARGET = 1e-2
FAST_DECAY_PCT = 0.3
SLOW_DECAY_PCT = 1.5
FILTER_SHIFT = 0.05
MIN_DECAY = math.log(FILTER_DECAY_TARGET) / SLOW_DECAY_PCT
MAX_DECAY = math.log(FILTER_DECAY_TARGET) / FAST_DECAY_PCT
N_HEADS = 8
QK_NOPE = 64
QK_ROPE = 32
V_HEAD = 64
Q_LORA = 384
KV_LORA = 256
ROPE_THETA = 10000.0
MLA_WIDTH = N_HEADS * V_HEAD
OFF_Q = 3 * HY_WIDTH
OFF_KV = OFF_Q + Q_LORA
OFF_KR = OFF_KV + KV_LORA
OFF_GATE = OFF_KR + QK_ROPE
N_GROUPS = 4
EXPERTS_PER_GROUP = 8
N_EXPERTS = N_GROUPS * EXPERTS_PER_GROUP
D_EXPERT = 256

LANES = 128
SUBLANES = 8
HEAD_SLAB = LANES
QK_SLABS = N_HEADS * HEAD_SLAB
ONES_LANE = V_HEAD
DFT_N2 = LANES
HY_PARAM_ROWS = 16
NEG_BIG = -1e30
VMEM_LIMIT = 56 * 1024 * 1024
MAX_STATIC_SHIFT = 60.0
SHIFT_SLACK = 1.01
SHIFT_EPS = 1e-3

C_Q = 0
C_KV = C_Q + Q_LORA
C_GATE = C_KV + KV_LORA
C_KR = C_GATE + 2 * D_MODEL
C_END = C_KR + HEAD_SLAB


def _cparams(*sem):
    return pltpu.CompilerParams(dimension_semantics=sem, vmem_limit_bytes=VMEM_LIMIT)


def _rms(x):
    return x * lax.rsqrt(jnp.mean(x * x, axis=-1, keepdims=True) + EPS)


def _dot_nt(a, b):
    return lax.dot_general(a, b, (((1,), (1,)), ((), ())), preferred_element_type=F32)


def _dot_tn(a, b):
    return lax.dot_general(a, b, (((0,), (0,)), ((), ())), preferred_element_type=F32)


def _ada_kernel(c_ref, w_ref, b_ref, o_ref):
    c = c_ref[...]
    ca = c * jax.nn.sigmoid(c)
    o_ref[...] = jnp.dot(ca, w_ref[...], preferred_element_type=F32, precision=HIGHEST) + b_ref[...]


def _ada(c8, w, b):
    d, n = w.shape
    tn = 1024
    return pl.pallas_call(
        _ada_kernel,
        grid=(n // tn,),
        in_specs=[pl.BlockSpec((8, d), lambda j: (0, 0)),
                  pl.BlockSpec((d, tn), lambda j: (0, j)),
                  pl.BlockSpec((1, tn), lambda j: (0, j))],
        out_specs=pl.BlockSpec((8, tn), lambda j: (0, j)),
        out_shape=jax.ShapeDtypeStruct((8, n), F32),
        compiler_params=_cparams("parallel"),
        name="ada",
    )(c8, w, b)


def _rope_slab(z, cos_t, sin_a, sin_b):
    return z * cos_t + pltpu.roll(z, HEAD_SLAB - QK_ROPE // 2, 1) * sin_a + pltpu.roll(z, QK_ROPE // 2, 1) * sin_b


def _inproj_kernel(x_ref, mod_ref, g_ref, wcat_ref, whyt_ref, gq_ref, gkv_ref, wuq_ref, wuk_ref, wuvt_ref,
                   cos_ref, sina_ref, sinb_ref,
                   hypt_ref, gate_ref, q_ref, k_ref, vt_ref, kn2_ref, *, qscale):
    d = D_MODEL
    x = x_ref[...]
    sh1 = mod_ref[0:1, 0:d]
    sc1 = mod_ref[0:1, d:2 * d]
    hb = (_rms(x) * g_ref[...] * (1.0 + sc1) + sh1).astype(BF16)
    hypt_ref[...] = _dot_nt(whyt_ref[...], hb).astype(BF16)
    proj = jnp.dot(hb, wcat_ref[...], preferred_element_type=F32)
    gate_ref[...] = jax.nn.sigmoid(proj[:, C_GATE:C_KR]).astype(BF16)
    cq = _rms(proj[:, C_Q:C_KV]) * gq_ref[...]
    ckv = (_rms(proj[:, C_KV:C_GATE]) * gkv_ref[...]).astype(BF16)
    q = jnp.dot(cq.astype(BF16), wuq_ref[...], preferred_element_type=F32)
    kn = jnp.dot(ckv, wuk_ref[...], preferred_element_type=F32)
    vt = _dot_nt(wuvt_ref[...], ckv)
    cos_t = cos_ref[...]
    sin_a = sina_ref[...]
    sin_b = sinb_ref[...]
    kr = _rope_slab(proj[:, C_KR:C_END], cos_t, sin_a, sin_b)
    srow = lax.broadcasted_iota(jnp.int32, (HEAD_SLAB, 1), 0)
    ones_row = jnp.where(srow == ONES_LANE, 1.0, 0.0).astype(F32)
    kn2 = []
    for hd in range(N_HEADS):
        sl = slice(hd * HEAD_SLAB, (hd + 1) * HEAD_SLAB)
        q_ref[:, sl] = (_rope_slab(q[:, sl], cos_t, sin_a, sin_b) * qscale).astype(BF16)
        kb = (kn[:, sl] + kr).astype(BF16)
        k_ref[:, sl] = kb
        vt_ref[hd, 0] = (vt[sl] + ones_row).astype(BF16)
        kf = kb.astype(F32)
        big = jnp.max(jnp.sum(kf * kf, axis=1, keepdims=True), axis=0, keepdims=True)
        kn2.append(jnp.broadcast_to(big, (1, HEAD_SLAB)))
    kn2_ref[0] = jnp.concatenate(kn2, axis=0)


def _inproj(x2, mod, g1, wcat, whyt, gq, gkv, wuq, wuk, wuvt, cos_t, sin_a, sin_b, tm, tkv):
    L, d = x2.shape
    qscale = (QK_NOPE + QK_ROPE) ** -0.5 * math.log2(math.e)
    full = lambda a: pl.BlockSpec(a.shape, lambda i: (0,) * a.ndim)
    row = lambda w: pl.BlockSpec((tm, w), lambda i: (i, 0))
    per = tkv // tm
    return pl.pallas_call(
        functools.partial(_inproj_kernel, qscale=qscale),
        grid=(L // tm,),
        in_specs=[row(d), full(mod), full(g1), full(wcat), full(whyt), full(gq), full(gkv), full(wuq),
                  full(wuk), full(wuvt), row(HEAD_SLAB), row(HEAD_SLAB), row(HEAD_SLAB)],
        out_specs=[pl.BlockSpec((3 * HY_WIDTH, tm), lambda i: (0, i)),
                   row(2 * D_MODEL), row(QK_SLABS), row(QK_SLABS),
                   pl.BlockSpec((N_HEADS, 1, HEAD_SLAB, tm), lambda i: (0, i // per, 0, i % per)),
                   pl.BlockSpec((1, N_HEADS, HEAD_SLAB), lambda i: (i, 0, 0))],
        out_shape=[jax.ShapeDtypeStruct((3 * HY_WIDTH, L), BF16),
                   jax.ShapeDtypeStruct((L, 2 * D_MODEL), BF16),
                   jax.ShapeDtypeStruct((L, QK_SLABS), BF16),
                   jax.ShapeDtypeStruct((L, QK_SLABS), BF16),
                   jax.ShapeDtypeStruct((N_HEADS, L // tkv, HEAD_SLAB, tkv), BF16),
                   jax.ShapeDtypeStruct((L // tm, N_HEADS, HEAD_SLAB), F32)],
        compiler_params=_cparams("parallel"),
        name="inproj",
    )(x2, mod, g1, wcat, whyt, gq, gkv, wuq, wuk, wuvt, cos_t, sin_a, sin_b)


def _filt_kernel(z_ref, w1_ref, b1_ref, f1_ref, w2_ref, b2_ref, f2_ref, w3_ref, dl_ref,
                 h_ref, nrm_ref):
    i = pl.program_id(0)
    z = z_ref[...]
    tm = z.shape[1]
    h = jnp.sin(f1_ref[...] * (jnp.dot(w1_ref[...], z, preferred_element_type=F32, precision=HIGHEST)
                               + b1_ref[...]))
    h = jnp.sin(f2_ref[...] * (jnp.dot(w2_ref[...], h, preferred_element_type=F32, precision=HIGHEST)
                               + b2_ref[...]))
    h3 = jnp.dot(w3_ref[...], h.astype(BF16), preferred_element_type=F32)
    t = z[0:1, :]
    window = jnp.exp(-dl_ref[...] * t) + FILTER_SHIFT
    hf = h3[0:HY_WIDTH] * window
    hb = h3[HY_WIDTH:2 * HY_WIDTH] * window
    col = lax.broadcasted_iota(jnp.int32, (1, tm), 1)
    hb = jnp.where((col == 0) & (i == 0), 0.0, hb)
    h_ref[0:HY_WIDTH, :] = hf.astype(BF16)
    h_ref[HY_WIDTH:2 * HY_WIDTH, :] = hb.astype(BF16)
    part = jnp.sum(jnp.abs(hf) + jnp.abs(hb), axis=1, keepdims=True)

    @pl.when(i == 0)
    def _():
        nrm_ref[...] = jnp.zeros_like(nrm_ref)

    nrm_ref[...] += part


def _filt(zt, w1t, b1, f1, w2t, b2, f2, w3t, dl, tm):
    L = zt.shape[1]
    full = lambda a: pl.BlockSpec(a.shape, lambda i: (0,) * a.ndim)
    return pl.pallas_call(
        _filt_kernel,
        grid=(L // tm,),
        in_specs=[pl.BlockSpec((zt.shape[0], tm), lambda i: (0, i)), full(w1t), full(b1), full(f1),
                  full(w2t), full(b2), full(f2), full(w3t), full(dl)],
        out_specs=[pl.BlockSpec((2 * HY_WIDTH, tm), lambda i: (0, i)),
                   pl.BlockSpec((HY_WIDTH, 1), lambda i: (0, 0))],
        out_shape=[jax.ShapeDtypeStruct((2 * HY_WIDTH, L), BF16),
                   jax.ShapeDtypeStruct((HY_WIDTH, 1), F32)],
        compiler_params=_cparams("arbitrary"),
        name="filt",
    )(zt, w1t, b1, f1, w2t, b2, f2, w3t, dl)


def _hyena_kernel(x0_ref, x1_ref, v_ref, hf_ref, hb_ref, prm_ref, f1_ref, tr_ref, ti_ref, f2_ref,
                  f2i_ref, cm_ref, sg_ref, o_ref, *, inv_n):
    cb, n1h, w = x0_ref.shape
    k1p = tr_ref.shape[0]
    lane = lax.broadcasted_iota(jnp.int32, (n1h, w), 1)
    rowi = lax.broadcasted_iota(jnp.int32, (n1h, w), 0)
    tr = tr_ref[...]
    ti = ti_ref[...]

    def conv(x, p, j):
        r = pltpu.roll(x, 1, 1)
        up = jnp.where(lane == 0, jnp.where(rowi == 0, 0.0, pltpu.roll(r, 1, 0)), r)
        l = pltpu.roll(x, w - 1, 1)
        dn = jnp.where(lane == w - 1, jnp.where(rowi == n1h - 1, 0.0, pltpu.roll(l, n1h - 1, 0)), l)
        return up * p[j:j + 1] + x * p[j + 1:j + 2] + dn * p[j + 2:j + 3] + p[j + 3:j + 4]

    def twiddle(a, s):
        ar = a[0:k1p, s * w:(s + 1) * w]
        ai = a[k1p:2 * k1p, s * w:(s + 1) * w]
        return jnp.concatenate([ar * tr - ai * ti, ar * ti + ai * tr], axis=1).astype(BF16)

    prm = [prm_ref[c] for c in range(cb)]
    x0s, us, sigs = [], [], []
    for c in range(cb):
        p = prm[c]
        x0s.append(conv(x0_ref[c].astype(F32), p, 0))
        u = conv(v_ref[c].astype(F32), p, 8) * conv(x1_ref[c].astype(F32), p, 4)
        us.append(u)
        sigs += [u.astype(BF16), hf_ref[c], hb_ref[c]]
    a = jnp.dot(f1_ref[...], jnp.concatenate(sigs, axis=1), preferred_element_type=F32)
    b = jnp.concatenate([twiddle(a, s) for s in range(3 * cb)], axis=0)
    s = jnp.dot(b, f2_ref[...], preferred_element_type=F32)
    ys = []
    for c in range(cb):
        su = s[(3 * c) * k1p:(3 * c + 1) * k1p]
        sf = s[(3 * c + 1) * k1p:(3 * c + 2) * k1p]
        sb = s[(3 * c + 2) * k1p:(3 * c + 3) * k1p]
        inv = 1.0 / prm[c][13:14]
        gr = (sf[:, 0:w] + sb[:, 0:w]) * inv
        gi = (sf[:, w:2 * w] - sb[:, w:2 * w]) * inv
        ur, ui = su[:, 0:w], su[:, w:2 * w]
        ys.append(jnp.concatenate([ur * gr - ui * gi, ur * gi + ui * gr], axis=1).astype(BF16))
    zt = jnp.dot(jnp.concatenate(ys, axis=0), f2i_ref[...], preferred_element_type=F32)
    zss, nyq = [], []
    for c in range(cb):
        ztr = zt[c * k1p:(c + 1) * k1p, 0:w]
        zti = zt[c * k1p:(c + 1) * k1p, w:2 * w]
        zr = ztr * tr + zti * ti
        zi = zti * tr - ztr * ti
        zss.append(jnp.concatenate([zr[0:n1h], zi[0:n1h]], axis=0).astype(BF16))
        nyq.append(zr[n1h:n1h + 1])
    yy = jnp.dot(cm_ref[...], jnp.concatenate(zss, axis=1), preferred_element_type=F32)
    for c in range(cb):
        conv_out = (yy[:, c * w:(c + 1) * w] + sg_ref[...] * nyq[c]) * inv_n
        o_ref[c] = (x0s[c] * (conv_out + prm[c][12:13] * us[c])).astype(BF16)


def _hyena(hyp3, h3, prm, tabs, cb, inv_n):
    c3, n1h, w = hyp3.shape
    c = c3 // 3
    nb = c // cb
    f1m, tr, ti, f2, f2i, cmat, sgn = tabs
    full = lambda a: pl.BlockSpec(a.shape, lambda i: (0,) * a.ndim)
    blk = lambda off: pl.BlockSpec((cb, n1h, w), lambda i: (off * nb + i, 0, 0))
    return pl.pallas_call(
        functools.partial(_hyena_kernel, inv_n=inv_n),
        grid=(nb,),
        in_specs=[blk(0), blk(1), blk(2), blk(0), blk(1),
                  pl.BlockSpec((cb, HY_PARAM_ROWS, w), lambda i: (i, 0, 0)),
                  full(f1m), full(tr), full(ti), full(f2), full(f2i), full(cmat), full(sgn)],
        out_specs=pl.BlockSpec((cb, n1h, w), lambda i: (i, 0, 0)),
        out_shape=jax.ShapeDtypeStruct((c, n1h, w), BF16),
        compiler_params=_cparams("parallel"),
        name="hyena",
    )(hyp3, hyp3, hyp3, h3, h3, prm, f1m, tr, ti, f2, f2i, cmat, sgn)


def _attn_kernel(q_ref, k_ref, vt_ref, kn2_ref, o_ref, *, unroll):
    tq = q_ref.shape[0]
    nk, _, tk = vt_ref.shape[1:]
    q = q_ref[...]

    def finish(acc):
        o_ref[...] = (acc / acc[ONES_LANE:ONES_LANE + 1, :]).astype(BF16)

    qf = q.astype(F32)
    qn2 = _dot_nt(jnp.ones((SUBLANES, HEAD_SLAB), BF16), (qf * qf).astype(BF16))[0:1]
    bound = jnp.sqrt(qn2 * kn2_ref[0][0:1, 0:1]) * SHIFT_SLACK + SHIFT_EPS
    static_shift_ok = jnp.max(bound) <= MAX_STATIC_SHIFT

    @pl.when(static_shift_ok)
    def _():
        def body(j, acc):
            off = pl.multiple_of(j * tk, tk)
            st = _dot_nt(k_ref[pl.ds(off, tk), :], q)
            p = jnp.exp2(st - bound).astype(BF16)
            return acc + jnp.dot(vt_ref[0, j], p, preferred_element_type=F32)

        finish(lax.fori_loop(0, nk, body, jnp.zeros((HEAD_SLAB, tq), F32), unroll=unroll))

    @pl.when(jnp.logical_not(static_shift_ok))
    def _():
        def body(j, carry):
            m, acc = carry
            off = pl.multiple_of(j * tk, tk)
            st = _dot_nt(k_ref[pl.ds(off, tk), :], q)
            m_new = jnp.maximum(m, jnp.max(st, axis=0, keepdims=True))
            alpha = jnp.exp2(m - m_new)
            p = jnp.exp2(st - m_new)
            acc = alpha * acc + jnp.dot(vt_ref[0, j], p.astype(BF16), preferred_element_type=F32)
            return m_new, acc

        m0 = jnp.full((1, tq), NEG_BIG, F32)
        acc0 = jnp.zeros((HEAD_SLAB, tq), F32)
        finish(lax.fori_loop(0, nk, body, (m0, acc0), unroll=unroll)[1])


def _attn(q, k, vt, kn2, tq, unroll):
    L = q.shape[0]
    _, nk, _, tk = vt.shape
    return pl.pallas_call(
        functools.partial(_attn_kernel, unroll=unroll),
        grid=(N_HEADS, L // tq),
        in_specs=[pl.BlockSpec((tq, HEAD_SLAB), lambda h, i: (i, h)),
                  pl.BlockSpec((L, HEAD_SLAB), lambda h, i: (0, h)),
                  pl.BlockSpec((1, nk, HEAD_SLAB, tk), lambda h, i: (h, 0, 0, 0)),
                  pl.BlockSpec((1, SUBLANES, LANES), lambda h, i: (h, 0, 0))],
        out_specs=pl.BlockSpec((HEAD_SLAB, tq), lambda h, i: (h, i)),
        out_shape=jax.ShapeDtypeStruct((QK_SLABS, L), BF16),
        compiler_params=_cparams("parallel", "parallel"),
        name="attn",
    )(q, k, vt, kn2)


def _merge_kernel(x_ref, yht_ref, ymt_ref, gate_ref, mod_ref, why_ref, wmla_ref, wmix_ref, g2_ref,
                  wrt_ref, brt_ref, x1_ref, h2_ref, comb_ref, gsel_ref, gcnt_ref):
    d = D_MODEL
    a = _dot_tn(yht_ref[...], why_ref[...])
    b = _dot_tn(ymt_ref[...], wmla_ref[...])
    mix = gate_ref[:, 0:d].astype(F32) * a + gate_ref[:, d:2 * d].astype(F32) * b
    o = jnp.dot(mix.astype(BF16), wmix_ref[...], preferred_element_type=F32)
    gt1 = mod_ref[0:1, 2 * d:3 * d]
    sh2 = mod_ref[0:1, 3 * d:4 * d]
    sc2 = mod_ref[0:1, 4 * d:5 * d]
    x1 = x_ref[...] + gt1 * o
    x1_ref[...] = x1
    h2 = _rms(x1) * g2_ref[...] * (1.0 + sc2) + sh2
    h2_hi = h2.astype(BF16)
    h2_ref[...] = h2_hi
    h2_lo = (h2 - h2_hi.astype(F32)).astype(BF16)
    wrt = wrt_ref[...]
    w_hi = wrt.astype(BF16)
    w_lo = (wrt - w_hi.astype(F32)).astype(BF16)
    e = (jnp.dot(h2_hi, w_hi, preferred_element_type=F32)
         + (jnp.dot(h2_lo, w_hi, preferred_element_type=F32)
            + jnp.dot(h2_hi, w_lo, preferred_element_type=F32))) + brt_ref[...]
    lane = lax.broadcasted_iota(jnp.int32, e.shape, 1).astype(F32)
    gvalid = (lane >= N_EXPERTS) & (lane < N_EXPERTS + N_GROUPS)
    glm = jnp.where(gvalid, e, NEG_BIG)
    gmax = jnp.max(glm, axis=1, keepdims=True)
    gidx = jnp.min(jnp.where(glm == gmax, lane, float(LANES)), axis=1, keepdims=True) - float(N_EXPERTS)
    psum = jnp.sum(jnp.where(gvalid, jnp.exp(glm - gmax), 0.0), axis=1, keepdims=True)
    p_sel = 1.0 / psum
    lo = gidx * EXPERTS_PER_GROUP
    em = jnp.where((lane >= lo) & (lane < lo + EXPERTS_PER_GROUP), e, NEG_BIG)
    v1 = jnp.max(em, axis=1, keepdims=True)
    i1 = jnp.min(jnp.where(em == v1, lane, float(LANES)), axis=1, keepdims=True)
    em2 = jnp.where(lane == i1, NEG_BIG, em)
    v2 = jnp.max(em2, axis=1, keepdims=True)
    i2 = jnp.min(jnp.where(em2 == v2, lane, float(LANES)), axis=1, keepdims=True)
    t = jnp.exp(v2 - v1)
    w1 = p_sel / (1.0 + t)
    w2 = p_sel * t / (1.0 + t)
    comb_ref[...] = jnp.where(lane == i1, w1, 0.0) + jnp.where(lane == i2, w2, 0.0)
    gsel = jnp.where(lane == gidx, 1.0, 0.0)
    gsel_ref[...] = gsel.astype(BF16)
    srow = lax.broadcasted_iota(jnp.int32, (SUBLANES, LANES), 0)
    gcnt_ref[0] = jnp.where(srow == 0, jnp.sum(gsel, axis=0, keepdims=True), 0.0)


def _merge(x2, yht, ym, gates, mod, why, wmla, wmix, g2, wrt, brt, tm):
    L, d = x2.shape
    full = lambda a: pl.BlockSpec(a.shape, lambda i: (0,) * a.ndim)
    row = lambda w: pl.BlockSpec((tm, w), lambda i: (i, 0))
    return pl.pallas_call(
        _merge_kernel,
        grid=(L // tm,),
        in_specs=[row(d), pl.BlockSpec((HY_WIDTH, tm), lambda i: (0, i)),
                  pl.BlockSpec((QK_SLABS, tm), lambda i: (0, i)), row(2 * d),
                  full(mod), full(why), full(wmla), full(wmix), full(g2), full(wrt), full(brt)],
        out_specs=[row(d), row(d), row(LANES), row(LANES),
                   pl.BlockSpec((1, SUBLANES, LANES), lambda i: (i, 0, 0))],
        out_shape=[jax.ShapeDtypeStruct((L, d), F32),
                   jax.ShapeDtypeStruct((L, d), BF16),
                   jax.ShapeDtypeStruct((L, LANES), F32),
                   jax.ShapeDtypeStruct((L, LANES), BF16),
                   jax.ShapeDtypeStruct((L // tm, SUBLANES, LANES), F32)],
        compiler_params=_cparams("parallel"),
        name="merge",
    )(x2, yht, ym, gates, mod, why, wmla, wmix, g2, wrt, brt)


MOE_TILE = 2048
MOE_ROWS = 256
MOE_BLOCKS = MOE_TILE // MOE_ROWS + N_GROUPS
POS_RADIX = 64


def _moe_kernel(bg_ref, nbu_ref, roff_ref, h_ref, comb_ref, gsel_ref, wgu_ref, wd_ref, o_ref,
                pos_ref, cs_ref, acc_ref):
    i = pl.program_id(0)
    b = pl.program_id(1)
    t = h_ref.shape[0]
    lane = lax.broadcasted_iota(jnp.int32, (1, LANES), 1)

    @pl.when(b == 0)
    def _():
        oh = gsel_ref[...]
        sub = MOE_ROWS
        tri = jnp.where(lax.broadcasted_iota(jnp.int32, (sub, sub), 1)
                        < lax.broadcasted_iota(jnp.int32, (sub, sub), 0), 1.0, 0.0).astype(BF16)
        carry = jnp.zeros((1, LANES), F32)
        cums = []
        for s in range(t // sub):
            blk = oh[s * sub:(s + 1) * sub]
            cums.append(jnp.dot(tri, blk, preferred_element_type=F32) + carry)
            carry = carry + jnp.sum(blk.astype(F32), axis=0, keepdims=True)
        cum = jnp.concatenate(cums, axis=0)
        offs = jnp.zeros((1, LANES), F32)
        for g in range(N_GROUPS):
            offs = jnp.where(lane == g, roff_ref[i * N_GROUPS + g].astype(F32), offs)
        pos = jnp.sum(jnp.where(oh > 0, cum + offs, 0.0), axis=1, keepdims=True)
        hi = jnp.floor(pos * (1.0 / POS_RADIX))
        lo = pos - hi * POS_RADIX
        digits = jnp.where(lane == 0, hi, jnp.where(lane == 1, lo, 0.0)).astype(BF16)
        srow = lax.broadcasted_iota(jnp.int32, (SUBLANES, LANES), 0)
        scol = lax.broadcasted_iota(jnp.int32, (SUBLANES, LANES), 1)
        pick = jnp.where(srow == scol, 1.0, 0.0).astype(BF16)
        rows = _dot_nt(pick, digits)
        pos_ref[...] = jnp.broadcast_to(rows[0:1] * POS_RADIX + rows[1:2], pos_ref.shape)
        comb = comb_ref[...]
        comb_hi = comb.astype(BF16)
        cs_ref[:, 0:LANES] = comb_hi
        cs_ref[:, LANES:2 * LANES] = (comb - comb_hi.astype(F32)).astype(BF16)
        acc_ref[...] = jnp.zeros_like(acc_ref)

    @pl.when(b < nbu_ref[i])
    def _():
        g = bg_ref[i * MOE_BLOCKS + b]
        rid = (lax.broadcasted_iota(jnp.int32, (MOE_ROWS, 1), 0) + b * MOE_ROWS).astype(F32)
        sel = jnp.where(rid == pos_ref[0:1, :], 1.0, 0.0).astype(BF16)
        xs = jnp.dot(sel, h_ref[...], preferred_element_type=F32).astype(BF16)
        cs = jnp.dot(sel, cs_ref[...], preferred_element_type=F32)
        comb_s = cs[:, 0:LANES] + cs[:, LANES:2 * LANES]
        parts = []
        for j in range(EXPERTS_PER_GROUP):
            gu = jnp.dot(xs, wgu_ref[j], preferred_element_type=F32)
            gt = gu[:, 0:D_EXPERT]
            a = gt * jax.nn.sigmoid(gt) * gu[:, D_EXPERT:2 * D_EXPERT]
            col = jnp.sum(jnp.where(lane == g * EXPERTS_PER_GROUP + j, comb_s, 0.0), axis=1, keepdims=True)
            parts.append((a * col).astype(BF16))
        y = jnp.dot(jnp.concatenate(parts, axis=1), wd_ref[...], preferred_element_type=F32)
        acc_ref[...] += _dot_tn(sel, y.astype(BF16))

    @pl.when(b == pl.num_programs(1) - 1)
    def _():
        o_ref[...] = acc_ref[...].astype(BF16)


def _moe(bg, nbu, roff, h2, comb, gsel, wgu, wd):
    L, d = h2.shape
    t = MOE_TILE
    once = dict(pipeline_mode=pl.Buffered(1))
    wsel = lambda i, b, bg_r, nbu_r, roff_r: (bg_r[i * MOE_BLOCKS + b], 0, 0)
    grid_spec = pltpu.PrefetchScalarGridSpec(
        num_scalar_prefetch=3,
        grid=(L // t, MOE_BLOCKS),
        in_specs=[pl.BlockSpec((t, d), lambda i, b, *_: (i, 0), **once),
                  pl.BlockSpec((t, LANES), lambda i, b, *_: (i, 0), **once),
                  pl.BlockSpec((t, LANES), lambda i, b, *_: (i, 0), **once),
                  pl.BlockSpec((EXPERTS_PER_GROUP, d, 2 * D_EXPERT), wsel),
                  pl.BlockSpec((EXPERTS_PER_GROUP * D_EXPERT, d), lambda i, b, bg_r, *_: (bg_r[i * MOE_BLOCKS + b], 0))],
        out_specs=pl.BlockSpec((t, d), lambda i, b, *_: (i, 0)),
        scratch_shapes=[pltpu.VMEM((SUBLANES, t), F32),
                        pltpu.VMEM((t, 2 * LANES), BF16),
                        pltpu.VMEM((t, d), F32)])
    return pl.pallas_call(
        _moe_kernel,
        grid_spec=grid_spec,
        out_shape=jax.ShapeDtypeStruct((L, d), BF16),
        compiler_params=_cparams("parallel", "arbitrary"),
        name="moe",
    )(bg, nbu, roff, h2, comb, gsel, wgu, wd)


def _moe_plan(gcnt, n_tiles):
    cnt = gcnt[:, 0, 0:N_GROUPS].reshape(n_tiles, -1, N_GROUPS).sum(axis=1).astype(jnp.int32)
    nblk = (cnt + MOE_ROWS - 1) // MOE_ROWS
    bend = jnp.cumsum(nblk, axis=1)
    roff = (bend - nblk) * MOE_ROWS
    nbu = bend[:, -1]
    bidx = jnp.arange(MOE_BLOCKS, dtype=jnp.int32)[None, :]
    grp = jnp.sum((bidx[:, :, None] >= bend[:, None, :]).astype(jnp.int32), axis=2)
    last = jnp.take_along_axis(grp, jnp.maximum(nbu - 1, 0)[:, None], axis=1)
    bg = jnp.where(bidx < nbu[:, None], grp, last)
    bg = jnp.minimum(bg, N_GROUPS - 1)
    return bg.reshape(-1), nbu, roff.reshape(-1)


def _final_kernel(x1_ref, m_ref, mod_ref, fg_ref, o_ref):
    d = D_MODEL
    x2 = x1_ref[...] + mod_ref[0:1, 5 * d:6 * d] * m_ref[...].astype(F32)
    o_ref[...] = _rms(x2) * fg_ref[...]


def _final(x1, m, mod, fg, tm):
    L, d = x1.shape
    return pl.pallas_call(
        _final_kernel,
        grid=(L // tm,),
        in_specs=[pl.BlockSpec((tm, d), lambda i: (i, 0)), pl.BlockSpec((tm, d), lambda i: (i, 0)),
                  pl.BlockSpec(mod.shape, lambda i: (0, 0)), pl.BlockSpec(fg.shape, lambda i: (0, 0))],
        out_specs=pl.BlockSpec((tm, d), lambda i: (i, 0)),
        out_shape=jax.ShapeDtypeStruct((L, d), F32),
        compiler_params=_cparams("parallel"),
        name="final",
    )(x1, m, mod, fg)


def _pad_heads(w, width):
    lead = w.shape[:-1]
    w = w.reshape(lead + (N_HEADS, width))
    w = jnp.pad(w, [(0, 0)] * len(lead) + [(0, 0), (0, HEAD_SLAB - width)])
    return w.reshape(lead + (N_HEADS * HEAD_SLAB,))


@functools.lru_cache(maxsize=None)
def _rope_slab_tables(L):
    half = QK_ROPE // 2
    pos = np.arange(L, dtype=np.float64)
    inv = ROPE_THETA ** (-np.arange(0, QK_ROPE, 2, dtype=np.float64) / QK_ROPE)
    ang = pos[:, None] * inv[None, :]
    cos, sin = np.cos(ang), np.sin(ang)
    z = lambda w: np.zeros((L, w))
    cos_t = np.concatenate([np.ones((L, QK_NOPE)), cos, cos, z(HEAD_SLAB - QK_NOPE - QK_ROPE)], axis=1)
    sin_a = np.concatenate([z(QK_NOPE), -sin, z(HEAD_SLAB - QK_NOPE - half)], axis=1)
    sin_b = np.concatenate([z(QK_NOPE + half), sin, z(HEAD_SLAB - QK_NOPE - QK_ROPE)], axis=1)
    return cos_t.astype(np.float32), sin_a.astype(np.float32), sin_b.astype(np.float32)


@functools.lru_cache(maxsize=None)
def _filter_features(L):
    t = np.linspace(0.0, 1.0, L)[None, :]
    t_r = np.arange(L, dtype=np.float64)[None, :]
    bands = np.linspace(1e-4, FILTER_BANDS - 1, FILTER_BANDS)[:, None]
    ang = 2.0 * math.pi * bands * t_r / L
    z = np.concatenate([t, np.cos(ang), -np.sin(ang)], axis=0)
    return np.pad(z, ((0, FILTER_HIDDEN - FILTER_EMB), (0, 0))).astype(np.float32)


@functools.lru_cache(maxsize=None)
def _dft_tables(L):
    n = 2 * L
    n2 = DFT_N2
    n1 = n // n2
    n1h = n1 // 2
    k1n = n1h + 1
    k1p = -(-k1n // SUBLANES) * SUBLANES
    two_pi = 2.0 * math.pi
    k1 = np.arange(k1p, dtype=np.int64)[:, None]
    live = (k1 < k1n).astype(np.float64)
    m1 = np.arange(n1h, dtype=np.int64)[None, :]
    ang1 = two_pi * ((k1 * m1) % n1) / n1
    f1m = np.concatenate([np.cos(ang1) * live, -np.sin(ang1) * live], axis=0).astype(BF16)
    m2 = np.arange(n2, dtype=np.int64)[None, :]
    ang2 = two_pi * ((k1 * m2) % n) / n
    tr = (np.cos(ang2) * live).astype(np.float32)
    ti = (-np.sin(ang2) * live).astype(np.float32)
    ang3 = two_pi * ((m2.T * m2) % n2) / n2
    f2r, f2i = np.cos(ang3), -np.sin(ang3)
    f2 = np.block([[f2r, f2i], [-f2i, f2r]]).astype(BF16)
    f2inv = np.block([[f2r, -f2i], [f2i, f2r]]).astype(BF16)
    o1 = np.arange(n1h, dtype=np.int64)[:, None]
    q1 = np.arange(n1h, dtype=np.int64)[None, :]
    ang4 = two_pi * ((o1 * q1) % n1) / n1
    wgt = np.where(q1 == 0, 1.0, 2.0)
    cmat = np.concatenate([wgt * np.cos(ang4), -wgt * np.sin(ang4)], axis=1).astype(BF16)
    sgn = np.broadcast_to(np.where(o1 % 2 == 0, 1.0, -1.0), (n1h, n2)).astype(np.float32)
    return (f1m, tr, ti, f2, f2inv, cmat, sgn), n1h


def _pick(n, pref):
    t = pref
    while n % t:
        t //= 2
    return t


def kernel(x, c, ada_w, ada_b, norm_mix_g, w_in, hy_conv_w, hy_conv_b, filt_w1, filt_b1, filt_freq1, filt_w2, filt_b2, filt_freq2, filt_w3, hy_bias, q_norm_g, kv_norm_g, w_uq, w_uk, w_uv, w_hy_out, w_mla_out, w_mix_out, norm_ffn_g, w_group, b_group, w_router, b_router, w_gate, w_up, w_down, final_norm_g):
    B, L, d = x.shape
    assert B == 1 and d == D_MODEL and ada_w.shape[0] == 1 and L % (2 * DFT_N2) == 0
    x2 = x.reshape(L, d)
    row = lambda v: v.reshape(1, -1)
    col = lambda v: v.reshape(-1, 1)

    mod = _ada(jnp.pad(c, ((0, 8 - B), (0, 0))), ada_w[0], row(ada_b[0]))

    wi = w_in[0]
    kr_slab = jnp.pad(wi[:, OFF_KR:OFF_GATE], ((0, 0), (QK_NOPE, HEAD_SLAB - QK_NOPE - QK_ROPE)))
    wcat = jnp.concatenate([wi[:, OFF_Q:OFF_KV], wi[:, OFF_KV:OFF_KR], wi[:, OFF_GATE:], kr_slab],
                           axis=1).astype(BF16)
    whyt = wi[:, 0:OFF_Q].T.astype(BF16)
    wuq = _pad_heads(w_uq[0], QK_NOPE + QK_ROPE).astype(BF16)
    wuk = _pad_heads(w_uk[0].reshape(KV_LORA, N_HEADS * QK_NOPE), QK_NOPE).astype(BF16)
    wuvt = _pad_heads(w_uv[0].reshape(KV_LORA, N_HEADS * V_HEAD), V_HEAD).T.astype(BF16)
    tkv = _pick(L, 1024)
    cos_t, sin_a, sin_b = _rope_slab_tables(L)

    hypt, gates, q, k, vt, kn2_tiles = _inproj(x2, mod, row(norm_mix_g[0]), wcat, whyt, row(q_norm_g[0]),
                                    row(kv_norm_g[0]), wuq, wuk, wuvt, cos_t, sin_a, sin_b,
                                    _pick(L, 512), tkv)

    zt = _filter_features(L)
    w1t = jnp.pad(filt_w1[0], ((0, FILTER_HIDDEN - FILTER_EMB), (0, 0))).T
    dl = np.abs(np.linspace(MIN_DECAY, MAX_DECAY, HY_WIDTH))[:, None].astype(np.float32)
    ht, nrm = _filt(zt, w1t, col(filt_b1[0]), col(filt_freq1[0]), filt_w2[0].T, col(filt_b2[0]),
                    col(filt_freq2[0]), filt_w3[0].T.astype(BF16), dl, _pick(L, 1024))
    tabs, n1h = _dft_tables(L)
    cw, cbias = hy_conv_w[0], hy_conv_b[0]
    w = HY_WIDTH
    taps = lambda s: [cw[0, s * w:(s + 1) * w], cw[1, s * w:(s + 1) * w], cw[2, s * w:(s + 1) * w],
                      cbias[s * w:(s + 1) * w]]
    prm = jnp.stack(taps(0) + taps(1) + taps(2) + [hy_bias[0], nrm[:, 0]]
                    + [jnp.zeros((w,), F32)] * (HY_PARAM_ROWS - 14), axis=1)
    prm = jnp.broadcast_to(prm[:, :, None], (w, HY_PARAM_ROWS, DFT_N2))
    yht = _hyena(hypt.reshape(3 * w, n1h, DFT_N2), ht.reshape(2 * w, n1h, DFT_N2), prm, tabs,
                 _pick(w, 16), 1.0 / (2 * L)).reshape(w, L)

    kn2 = jnp.broadcast_to(jnp.max(kn2_tiles, axis=0)[:, None, :], (N_HEADS, SUBLANES, LANES))
    ym = _attn(q, k, vt, kn2, _pick(L, 1024), 16)

    wmla = jnp.pad(w_mla_out[0].reshape(N_HEADS, V_HEAD, d),
                   ((0, 0), (0, HEAD_SLAB - V_HEAD), (0, 0))).reshape(QK_SLABS, d).astype(BF16)
    wrt = jnp.pad(jnp.concatenate([w_router[0], w_group[0]], axis=1),
                  ((0, 0), (0, LANES - N_EXPERTS - N_GROUPS)))
    brt = jnp.pad(jnp.concatenate([b_router[0], b_group[0]]), (0, LANES - N_EXPERTS - N_GROUPS))[None, :]
    x1, h2, comb, gsel, gcnt = _merge(x2, yht, ym, gates, mod, w_hy_out[0].astype(BF16), wmla,
                                      w_mix_out[0].astype(BF16), row(norm_ffn_g[0]), wrt, brt,
                                      _pick(L, 1024))

    assert L % MOE_TILE == 0
    bg, nbu, roff = _moe_plan(gcnt, L // MOE_TILE)
    wgu = jnp.concatenate([w_gate[0], w_up[0]], axis=2).astype(BF16)
    wd = w_down[0].reshape(N_EXPERTS * D_EXPERT, d).astype(BF16)
    moe = _moe(bg, nbu, roff, h2, comb, gsel, wgu, wd)
    out = _final(x1, moe, mod, row(final_norm_g), _pick(L, 512))
    return out.reshape(B, L, d)
```

```python
import functools
import math

import numpy as np
import jax
import jax.numpy as jnp
from jax import lax
from jax.experimental import pallas as pl
from jax.experimental.pallas import tpu as pltpu

F32 = jnp.float32
BF16 = jnp.bfloat16
HIGHEST = lax.Precision.HIGHEST

D_MODEL = 1024
EPS = 1e-6
HY_WIDTH = 512
FILTER_BANDS = 16
FILTER_EMB = 1 + 2 * FILTER_BANDS
FILTER_HIDDEN = 64
FILTER_DECAY_TARGET = 1e-2
FAST_DECAY_PCT = 0.3
SLOW_DECAY_PCT = 1.5
FILTER_SHIFT = 0.05
MIN_DECAY = math.log(FILTER_DECAY_TARGET) / SLOW_DECAY_PCT
MAX_DECAY = math.log(FILTER_DECAY_TARGET) / FAST_DECAY_PCT
N_HEADS = 8
QK_NOPE = 64
QK_ROPE = 32
V_HEAD = 64
Q_LORA = 384
KV_LORA = 256
ROPE_THETA = 10000.0
MLA_WIDTH = N_HEADS * V_HEAD
OFF_Q = 3 * HY_WIDTH
OFF_KV = OFF_Q + Q_LORA
OFF_KR = OFF_KV + KV_LORA
OFF_GATE = OFF_KR + QK_ROPE
N_GROUPS = 4
EXPERTS_PER_GROUP = 8
N_EXPERTS = N_GROUPS * EXPERTS_PER_GROUP
D_EXPERT = 256

LANES = 128
SUBLANES = 8
HEAD_SLAB = LANES
QK_SLABS = N_HEADS * HEAD_SLAB
ONES_LANE = V_HEAD
V_ROWS = 80
V_ALL = N_HEADS * V_ROWS
DFT_N2 = LANES
HY_PARAM_ROWS = 16
NEG_BIG = -1e30
VMEM_LIMIT = 56 * 1024 * 1024
MAX_STATIC_SHIFT = 60.0
SHIFT_SLACK = 1.01
SHIFT_EPS = 1e-3

C_Q = 0
C_KV = C_Q + Q_LORA
C_GATE = C_KV + KV_LORA
C_KR = C_GATE + 2 * D_MODEL
C_END = C_KR + HEAD_SLAB


def _cparams(*sem):
    return pltpu.CompilerParams(dimension_semantics=sem, vmem_limit_bytes=VMEM_LIMIT)


def _rms(x):
    return x * lax.rsqrt(jnp.mean(x * x, axis=-1, keepdims=True) + EPS)


def _dot_nt(a, b):
    return lax.dot_general(a, b, (((1,), (1,)), ((), ())), preferred_element_type=F32)


def _dot_tn(a, b):
    return lax.dot_general(a, b, (((0,), (0,)), ((), ())), preferred_element_type=F32)


def _ada_kernel(c_ref, w_ref, b_ref, o_ref):
    c = c_ref[...]
    ca = c * jax.nn.sigmoid(c)
    o_ref[...] = jnp.dot(ca, w_ref[...], preferred_element_type=F32, precision=HIGHEST) + b_ref[...]


def _ada(c8, w, b):
    d, n = w.shape
    tn = 1024
    return pl.pallas_call(
        _ada_kernel,
        grid=(n // tn,),
        in_specs=[pl.BlockSpec((8, d), lambda j: (0, 0)),
                  pl.BlockSpec((d, tn), lambda j: (0, j)),
                  pl.BlockSpec((1, tn), lambda j: (0, j))],
        out_specs=pl.BlockSpec((8, tn), lambda j: (0, j)),
        out_shape=jax.ShapeDtypeStruct((8, n), F32),
        compiler_params=_cparams("parallel"),
        name="ada",
    )(c8, w, b)


def _rope_slab(z, cos_t, sin_a, sin_b):
    return z * cos_t + pltpu.roll(z, HEAD_SLAB - QK_ROPE // 2, 1) * sin_a + pltpu.roll(z, QK_ROPE // 2, 1) * sin_b


def _inproj_kernel(x_ref, mod_ref, g_ref, wcat_ref, whyt_ref, gq_ref, gkv_ref, wuq_ref, wuk_ref, wuvt_ref,
                   cos_ref, sina_ref, sinb_ref,
                   hypt_ref, gate_ref, q_ref, k_ref, vt_ref, kn2_ref, *, qscale):
    d = D_MODEL
    x = x_ref[...]
    sh1 = mod_ref[0:1, 0:d]
    sc1 = mod_ref[0:1, d:2 * d]
    hb = (_rms(x) * g_ref[...] * (1.0 + sc1) + sh1).astype(BF16)
    hypt_ref[...] = _dot_nt(whyt_ref[...], hb).astype(BF16)
    proj = jnp.dot(hb, wcat_ref[...], preferred_element_type=F32)
    gate_ref[...] = jax.nn.sigmoid(proj[:, C_GATE:C_KR]).astype(BF16)
    cq = _rms(proj[:, C_Q:C_KV]) * gq_ref[...]
    ckv = (_rms(proj[:, C_KV:C_GATE]) * gkv_ref[...]).astype(BF16)
    q = jnp.dot(cq.astype(BF16), wuq_ref[...], preferred_element_type=F32)
    kn = jnp.dot(ckv, wuk_ref[...], preferred_element_type=F32)
    vt = _dot_nt(wuvt_ref[...], ckv)
    cos_t = cos_ref[...]
    sin_a = sina_ref[...]
    sin_b = sinb_ref[...]
    kr = _rope_slab(proj[:, C_KR:C_END], cos_t, sin_a, sin_b)
    srow = lax.broadcasted_iota(jnp.int32, (V_ROWS, 1), 0)
    ones_row = jnp.where(srow == ONES_LANE, 1.0, 0.0).astype(F32)
    kn2 = []
    for hd in range(N_HEADS):
        sl = slice(hd * HEAD_SLAB, (hd + 1) * HEAD_SLAB)
        q_ref[:, sl] = (_rope_slab(q[:, sl], cos_t, sin_a, sin_b) * qscale).astype(BF16)
        kb = (kn[:, sl] + kr).astype(BF16)
        k_ref[:, sl] = kb
        vt_ref[hd, 0] = (vt[hd * V_ROWS:(hd + 1) * V_ROWS] + ones_row).astype(BF16)
        kf = kb.astype(F32)
        big = jnp.max(jnp.sum(kf * kf, axis=1, keepdims=True), axis=0, keepdims=True)
        kn2.append(jnp.broadcast_to(big, (1, HEAD_SLAB)))
    kn2_ref[0] = jnp.concatenate(kn2, axis=0)


def _inproj(x2, mod, g1, wcat, whyt, gq, gkv, wuq, wuk, wuvt, cos_t, sin_a, sin_b, tm, tkv):
    L, d = x2.shape
    qscale = (QK_NOPE + QK_ROPE) ** -0.5 * math.log2(math.e)
    full = lambda a: pl.BlockSpec(a.shape, lambda i: (0,) * a.ndim)
    row = lambda w: pl.BlockSpec((tm, w), lambda i: (i, 0))
    per = tkv // tm
    return pl.pallas_call(
        functools.partial(_inproj_kernel, qscale=qscale),
        grid=(L // tm,),
        in_specs=[row(d), full(mod), full(g1), full(wcat), full(whyt), full(gq), full(gkv), full(wuq),
                  full(wuk), full(wuvt), row(HEAD_SLAB), row(HEAD_SLAB), row(HEAD_SLAB)],
        out_specs=[pl.BlockSpec((3 * HY_WIDTH, tm), lambda i: (0, i)),
                   row(2 * D_MODEL), row(QK_SLABS), row(QK_SLABS),
                   pl.BlockSpec((N_HEADS, 1, V_ROWS, tm), lambda i: (0, i // per, 0, i % per)),
                   pl.BlockSpec((1, N_HEADS, HEAD_SLAB), lambda i: (i, 0, 0))],
        out_shape=[jax.ShapeDtypeStruct((3 * HY_WIDTH, L), BF16),
                   jax.ShapeDtypeStruct((L, 2 * D_MODEL), BF16),
                   jax.ShapeDtypeStruct((L, QK_SLABS), BF16),
                   jax.ShapeDtypeStruct((L, QK_SLABS), BF16),
                   jax.ShapeDtypeStruct((N_HEADS, L // tkv, V_ROWS, tkv), BF16),
                   jax.ShapeDtypeStruct((L // tm, N_HEADS, HEAD_SLAB), F32)],
        compiler_params=_cparams("parallel"),
        name="inproj",
    )(x2, mod, g1, wcat, whyt, gq, gkv, wuq, wuk, wuvt, cos_t, sin_a, sin_b)


def _filt_kernel(z_ref, w1_ref, b1_ref, f1_ref, w2_ref, b2_ref, f2_ref, w3_ref, dl_ref,
                 h_ref, nrm_ref):
    i = pl.program_id(0)
    z = z_ref[...]
    tm = z.shape[1]
    h = jnp.sin(f1_ref[...] * (jnp.dot(w1_ref[...], z, preferred_element_type=F32, precision=HIGHEST)
                               + b1_ref[...]))
    h = jnp.sin(f2_ref[...] * (jnp.dot(w2_ref[...], h, preferred_element_type=F32, precision=HIGHEST)
                               + b2_ref[...]))
    h3 = jnp.dot(w3_ref[...], h.astype(BF16), preferred_element_type=F32)
    t = z[0:1, :]
    window = jnp.exp(-dl_ref[...] * t) + FILTER_SHIFT
    hf = h3[0:HY_WIDTH] * window
    hb = h3[HY_WIDTH:2 * HY_WIDTH] * window
    col = lax.broadcasted_iota(jnp.int32, (1, tm), 1)
    hb = jnp.where((col == 0) & (i == 0), 0.0, hb)
    h_ref[0:HY_WIDTH, :] = hf.astype(BF16)
    h_ref[HY_WIDTH:2 * HY_WIDTH, :] = hb.astype(BF16)
    part = jnp.sum(jnp.abs(hf) + jnp.abs(hb), axis=1, keepdims=True)

    @pl.when(i == 0)
    def _():
        nrm_ref[...] = jnp.zeros_like(nrm_ref)

    nrm_ref[...] += part


def _filt(zt, w1t, b1, f1, w2t, b2, f2, w3t, dl, tm):
    L = zt.shape[1]
    full = lambda a: pl.BlockSpec(a.shape, lambda i: (0,) * a.ndim)
    return pl.pallas_call(
        _filt_kernel,
        grid=(L // tm,),
        in_specs=[pl.BlockSpec((zt.shape[0], tm), lambda i: (0, i)), full(w1t), full(b1), full(f1),
                  full(w2t), full(b2), full(f2), full(w3t), full(dl)],
        out_specs=[pl.BlockSpec((2 * HY_WIDTH, tm), lambda i: (0, i)),
                   pl.BlockSpec((HY_WIDTH, 1), lambda i: (0, 0))],
        out_shape=[jax.ShapeDtypeStruct((2 * HY_WIDTH, L), BF16),
                   jax.ShapeDtypeStruct((HY_WIDTH, 1), F32)],
        compiler_params=_cparams("arbitrary"),
        name="filt",
    )(zt, w1t, b1, f1, w2t, b2, f2, w3t, dl)


def _hyena_kernel(x0_ref, x1_ref, v_ref, hf_ref, hb_ref, prm_ref, f1_ref, tr_ref, ti_ref, f2_ref,
                  f2i_ref, cm_ref, sg_ref, o_ref, *, inv_n):
    cb, n1h, w = x0_ref.shape
    k1p = tr_ref.shape[0]
    lane = lax.broadcasted_iota(jnp.int32, (n1h, w), 1)
    rowi = lax.broadcasted_iota(jnp.int32, (n1h, w), 0)
    tr = tr_ref[...]
    ti = ti_ref[...]

    def conv(x, p, j):
        r = pltpu.roll(x, 1, 1)
        up = jnp.where(lane == 0, jnp.where(rowi == 0, 0.0, pltpu.roll(r, 1, 0)), r)
        l = pltpu.roll(x, w - 1, 1)
        dn = jnp.where(lane == w - 1, jnp.where(rowi == n1h - 1, 0.0, pltpu.roll(l, n1h - 1, 0)), l)
        return up * p[j:j + 1] + x * p[j + 1:j + 2] + dn * p[j + 2:j + 3] + p[j + 3:j + 4]

    def twiddle(a, s):
        ar = a[0:k1p, s * w:(s + 1) * w]
        ai = a[k1p:2 * k1p, s * w:(s + 1) * w]
        return jnp.concatenate([ar * tr - ai * ti, ar * ti + ai * tr], axis=1).astype(BF16)

    prm = [prm_ref[c] for c in range(cb)]
    x0s, us, sigs = [], [], []
    for c in range(cb):
        p = prm[c]
        x0s.append(conv(x0_ref[c].astype(F32), p, 0))
        u = conv(v_ref[c].astype(F32), p, 8) * conv(x1_ref[c].astype(F32), p, 4)
        us.append(u)
        sigs += [u.astype(BF16), hf_ref[c], hb_ref[c]]
    a = jnp.dot(f1_ref[...], jnp.concatenate(sigs, axis=1), preferred_element_type=F32)
    b = jnp.concatenate([twiddle(a, s) for s in range(3 * cb)], axis=0)
    s = jnp.dot(b, f2_ref[...], preferred_element_type=F32)
    ys = []
    for c in range(cb):
        su = s[(3 * c) * k1p:(3 * c + 1) * k1p]
        sf = s[(3 * c + 1) * k1p:(3 * c + 2) * k1p]
        sb = s[(3 * c + 2) * k1p:(3 * c + 3) * k1p]
        inv = 1.0 / prm[c][13:14]
        gr = (sf[:, 0:w] + sb[:, 0:w]) * inv
        gi = (sf[:, w:2 * w] - sb[:, w:2 * w]) * inv
        ur, ui = su[:, 0:w], su[:, w:2 * w]
        ys.append(jnp.concatenate([ur * gr - ui * gi, ur * gi + ui * gr], axis=1).astype(BF16))
    zt = jnp.dot(jnp.concatenate(ys, axis=0), f2i_ref[...], preferred_element_type=F32)
    zss, nyq = [], []
    for c in range(cb):
        ztr = zt[c * k1p:(c + 1) * k1p, 0:w]
        zti = zt[c * k1p:(c + 1) * k1p, w:2 * w]
        zr = ztr * tr + zti * ti
        zi = zti * tr - ztr * ti
        zss.append(jnp.concatenate([zr[0:n1h], zi[0:n1h]], axis=0).astype(BF16))
        nyq.append(zr[n1h:n1h + 1])
    yy = jnp.dot(cm_ref[...], jnp.concatenate(zss, axis=1), preferred_element_type=F32)
    for c in range(cb):
        conv_out = (yy[:, c * w:(c + 1) * w] + sg_ref[...] * nyq[c]) * inv_n
        o_ref[c] = (x0s[c] * (conv_out + prm[c][12:13] * us[c])).astype(BF16)


def _hyena(hyp3, h3, prm, tabs, cb, inv_n):
    c3, n1h, w = hyp3.shape
    c = c3 // 3
    nb = c // cb
    f1m, tr, ti, f2, f2i, cmat, sgn = tabs
    full = lambda a: pl.BlockSpec(a.shape, lambda i: (0,) * a.ndim)
    blk = lambda off: pl.BlockSpec((cb, n1h, w), lambda i: (off * nb + i, 0, 0))
    return pl.pallas_call(
        functools.partial(_hyena_kernel, inv_n=inv_n),
        grid=(nb,),
        in_specs=[blk(0), blk(1), blk(2), blk(0), blk(1),
                  pl.BlockSpec((cb, HY_PARAM_ROWS, w), lambda i: (i, 0, 0)),
                  full(f1m), full(tr), full(ti), full(f2), full(f2i), full(cmat), full(sgn)],
        out_specs=pl.BlockSpec((cb, n1h, w), lambda i: (i, 0, 0)),
        out_shape=jax.ShapeDtypeStruct((c, n1h, w), BF16),
        compiler_params=_cparams("parallel"),
        name="hyena",
    )(hyp3, hyp3, hyp3, h3, h3, prm, f1m, tr, ti, f2, f2i, cmat, sgn)


def _attn_kernel(q_ref, k_ref, vt_ref, kn2_ref, o_ref, *, unroll):
    tq = q_ref.shape[0]
    nk, _, tk = vt_ref.shape[1:]
    q = q_ref[...]

    def finish(acc):
        o_ref[...] = (acc / acc[ONES_LANE:ONES_LANE + 1, :]).astype(BF16)

    qf = q.astype(F32)
    qn2 = _dot_nt(jnp.ones((SUBLANES, HEAD_SLAB), BF16), (qf * qf).astype(BF16))[0:1]
    bound = jnp.sqrt(qn2 * kn2_ref[0][0:1, 0:1]) * SHIFT_SLACK + SHIFT_EPS
    static_shift_ok = jnp.max(bound) <= MAX_STATIC_SHIFT

    @pl.when(static_shift_ok)
    def _():
        def body(j, acc):
            off = pl.multiple_of(j * tk, tk)
            st = _dot_nt(k_ref[pl.ds(off, tk), :], q)
            p = jnp.exp2(st - bound).astype(BF16)
            return acc + jnp.dot(vt_ref[0, j], p, preferred_element_type=F32)

        finish(lax.fori_loop(0, nk, body, jnp.zeros((V_ROWS, tq), F32), unroll=unroll))

    @pl.when(jnp.logical_not(static_shift_ok))
    def _():
        def body(j, carry):
            m, acc = carry
            off = pl.multiple_of(j * tk, tk)
            st = _dot_nt(k_ref[pl.ds(off, tk), :], q)
            m_new = jnp.maximum(m, jnp.max(st, axis=0, keepdims=True))
            alpha = jnp.exp2(m - m_new)
            p = jnp.exp2(st - m_new)
            acc = alpha * acc + jnp.dot(vt_ref[0, j], p.astype(BF16), preferred_element_type=F32)
            return m_new, acc

        m0 = jnp.full((1, tq), NEG_BIG, F32)
        acc0 = jnp.zeros((V_ROWS, tq), F32)
        finish(lax.fori_loop(0, nk, body, (m0, acc0), unroll=unroll)[1])


def _attn(q, k, vt, kn2, tq, unroll):
    L = q.shape[0]
    _, nk, _, tk = vt.shape
    return pl.pallas_call(
        functools.partial(_attn_kernel, unroll=unroll),
        grid=(N_HEADS, L // tq),
        in_specs=[pl.BlockSpec((tq, HEAD_SLAB), lambda h, i: (i, h)),
                  pl.BlockSpec((L, HEAD_SLAB), lambda h, i: (0, h)),
                  pl.BlockSpec((1, nk, V_ROWS, tk), lambda h, i: (h, 0, 0, 0)),
                  pl.BlockSpec((1, SUBLANES, LANES), lambda h, i: (h, 0, 0))],
        out_specs=pl.BlockSpec((V_ROWS, tq), lambda h, i: (h, i)),
        out_shape=jax.ShapeDtypeStruct((V_ALL, L), BF16),
        compiler_params=_cparams("parallel", "parallel"),
        name="attn",
    )(q, k, vt, kn2)


def _merge_kernel(x_ref, yht_ref, ymt_ref, gate_ref, mod_ref, why_ref, wmla_ref, wmix_ref, g2_ref,
                  wrt_ref, brt_ref, x1_ref, h2_ref, comb_ref, gsel_ref, gcnt_ref):
    d = D_MODEL
    a = _dot_tn(yht_ref[...], why_ref[...])
    b = _dot_tn(ymt_ref[...], wmla_ref[...])
    mix = gate_ref[:, 0:d].astype(F32) * a + gate_ref[:, d:2 * d].astype(F32) * b
    o = jnp.dot(mix.astype(BF16), wmix_ref[...], preferred_element_type=F32)
    gt1 = mod_ref[0:1, 2 * d:3 * d]
    sh2 = mod_ref[0:1, 3 * d:4 * d]
    sc2 = mod_ref[0:1, 4 * d:5 * d]
    x1 = x_ref[...] + gt1 * o
    x1_ref[...] = x1
    h2 = _rms(x1) * g2_ref[...] * (1.0 + sc2) + sh2
    h2_hi = h2.astype(BF16)
    h2_ref[...] = h2_hi
    h2_lo = (h2 - h2_hi.astype(F32)).astype(BF16)
    wrt = wrt_ref[...]
    w_hi = wrt.astype(BF16)
    w_lo = (wrt - w_hi.astype(F32)).astype(BF16)
    e = (jnp.dot(h2_hi, w_hi, preferred_element_type=F32)
         + (jnp.dot(h2_lo, w_hi, preferred_element_type=F32)
            + jnp.dot(h2_hi, w_lo, preferred_element_type=F32))) + brt_ref[...]
    lane = lax.broadcasted_iota(jnp.int32, e.shape, 1).astype(F32)
    gvalid = (lane >= N_EXPERTS) & (lane < N_EXPERTS + N_GROUPS)
    glm = jnp.where(gvalid, e, NEG_BIG)
    gmax = jnp.max(glm, axis=1, keepdims=True)
    gidx = jnp.min(jnp.where(glm == gmax, lane, float(LANES)), axis=1, keepdims=True) - float(N_EXPERTS)
    psum = jnp.sum(jnp.where(gvalid, jnp.exp(glm - gmax), 0.0), axis=1, keepdims=True)
    p_sel = 1.0 / psum
    lo = gidx * EXPERTS_PER_GROUP
    em = jnp.where((lane >= lo) & (lane < lo + EXPERTS_PER_GROUP), e, NEG_BIG)
    v1 = jnp.max(em, axis=1, keepdims=True)
    i1 = jnp.min(jnp.where(em == v1, lane, float(LANES)), axis=1, keepdims=True)
    em2 = jnp.where(lane == i1, NEG_BIG, em)
    v2 = jnp.max(em2, axis=1, keepdims=True)
    i2 = jnp.min(jnp.where(em2 == v2, lane, float(LANES)), axis=1, keepdims=True)
    t = jnp.exp(v2 - v1)
    w1 = p_sel / (1.0 + t)
    w2 = p_sel * t / (1.0 + t)
    comb_ref[...] = jnp.where(lane == i1, w1, 0.0) + jnp.where(lane == i2, w2, 0.0)
    gsel = jnp.where(lane == gidx, 1.0, 0.0)
    gsel_ref[...] = gsel.astype(BF16)
    srow = lax.broadcasted_iota(jnp.int32, (SUBLANES, LANES), 0)
    gcnt_ref[0] = jnp.where(srow == 0, jnp.sum(gsel, axis=0, keepdims=True), 0.0)


def _merge(x2, yht, ym, gates, mod, why, wmla, wmix, g2, wrt, brt, tm):
    L, d = x2.shape
    full = lambda a: pl.BlockSpec(a.shape, lambda i: (0,) * a.ndim)
    row = lambda w: pl.BlockSpec((tm, w), lambda i: (i, 0))
    return pl.pallas_call(
        _merge_kernel,
        grid=(L // tm,),
        in_specs=[row(d), pl.BlockSpec((HY_WIDTH, tm), lambda i: (0, i)),
                  pl.BlockSpec((V_ALL, tm), lambda i: (0, i)), row(2 * d),
                  full(mod), full(why), full(wmla), full(wmix), full(g2), full(wrt), full(brt)],
        out_specs=[row(d), row(d), row(LANES), row(LANES),
                   pl.BlockSpec((1, SUBLANES, LANES), lambda i: (i, 0, 0))],
        out_shape=[jax.ShapeDtypeStruct((L, d), F32),
                   jax.ShapeDtypeStruct((L, d), BF16),
                   jax.ShapeDtypeStruct((L, LANES), F32),
                   jax.ShapeDtypeStruct((L, LANES), BF16),
                   jax.ShapeDtypeStruct((L // tm, SUBLANES, LANES), F32)],
        compiler_params=_cparams("parallel"),
        name="merge",
    )(x2, yht, ym, gates, mod, why, wmla, wmix, g2, wrt, brt)


MOE_TILE = 2048
MOE_ROWS = 256
MOE_BLOCKS = MOE_TILE // MOE_ROWS + N_GROUPS
POS_RADIX = 64


def _moe_kernel(bg_ref, nbu_ref, roff_ref, h_ref, comb_ref, gsel_ref, wgu_ref, wd_ref, o_ref,
                pos_ref, cs_ref, acc_ref):
    i = pl.program_id(0)
    b = pl.program_id(1)
    t = h_ref.shape[0]
    lane = lax.broadcasted_iota(jnp.int32, (1, LANES), 1)

    @pl.when(b == 0)
    def _():
        oh = gsel_ref[...]
        sub = MOE_ROWS
        tri = jnp.where(lax.broadcasted_iota(jnp.int32, (sub, sub), 1)
                        < lax.broadcasted_iota(jnp.int32, (sub, sub), 0), 1.0, 0.0).astype(BF16)
        carry = jnp.zeros((1, LANES), F32)
        cums = []
        for s in range(t // sub):
            blk = oh[s * sub:(s + 1) * sub]
            cums.append(jnp.dot(tri, blk, preferred_element_type=F32) + carry)
            carry = carry + jnp.sum(blk.astype(F32), axis=0, keepdims=True)
        cum = jnp.concatenate(cums, axis=0)
        offs = jnp.zeros((1, LANES), F32)
        for g in range(N_GROUPS):
            offs = jnp.where(lane == g, roff_ref[i * N_GROUPS + g].astype(F32), offs)
        pos = jnp.sum(jnp.where(oh > 0, cum + offs, 0.0), axis=1, keepdims=True)
        hi = jnp.floor(pos * (1.0 / POS_RADIX))
        lo = pos - hi * POS_RADIX
        digits = jnp.where(lane == 0, hi, jnp.where(lane == 1, lo, 0.0)).astype(BF16)
        srow = lax.broadcasted_iota(jnp.int32, (SUBLANES, LANES), 0)
        scol = lax.broadcasted_iota(jnp.int32, (SUBLANES, LANES), 1)
        pick = jnp.where(srow == scol, 1.0, 0.0).astype(BF16)
        rows = _dot_nt(pick, digits)
        pos_ref[...] = jnp.broadcast_to(rows[0:1] * POS_RADIX + rows[1:2], pos_ref.shape)
        comb = comb_ref[...]
        comb_hi = comb.astype(BF16)
        cs_ref[:, 0:LANES] = comb_hi
        cs_ref[:, LANES:2 * LANES] = (comb - comb_hi.astype(F32)).astype(BF16)
        acc_ref[...] = jnp.zeros_like(acc_ref)

    @pl.when(b < nbu_ref[i])
    def _():
        g = bg_ref[i * MOE_BLOCKS + b]
        rid = (lax.broadcasted_iota(jnp.int32, (MOE_ROWS, 1), 0) + b * MOE_ROWS).astype(F32)
        sel = jnp.where(rid == pos_ref[0:1, :], 1.0, 0.0).astype(BF16)
        xs = jnp.dot(sel, h_ref[...], preferred_element_type=F32).astype(BF16)
        cs = jnp.dot(sel, cs_ref[...], preferred_element_type=F32)
        comb_s = cs[:, 0:LANES] + cs[:, LANES:2 * LANES]
        parts = []
        for j in range(EXPERTS_PER_GROUP):
            gu = jnp.dot(xs, wgu_ref[j], preferred_element_type=F32)
            gt = gu[:, 0:D_EXPERT]
            a = gt * jax.nn.sigmoid(gt) * gu[:, D_EXPERT:2 * D_EXPERT]
            col = jnp.sum(jnp.where(lane == g * EXPERTS_PER_GROUP + j, comb_s, 0.0), axis=1, keepdims=True)
            parts.append((a * col).astype(BF16))
        y = jnp.dot(jnp.concatenate(parts, axis=1), wd_ref[...], preferred_element_type=F32)
        acc_ref[...] += _dot_tn(sel, y.astype(BF16))

    @pl.when(b == pl.num_programs(1) - 1)
    def _():
        o_ref[...] = acc_ref[...].astype(BF16)


def _moe(bg, nbu, roff, h2, comb, gsel, wgu, wd):
    L, d = h2.shape
    t = MOE_TILE
    once = dict(pipeline_mode=pl.Buffered(1))
    wsel = lambda i, b, bg_r, nbu_r, roff_r: (bg_r[i * MOE_BLOCKS + b], 0, 0)
    grid_spec = pltpu.PrefetchScalarGridSpec(
        num_scalar_prefetch=3,
        grid=(L // t, MOE_BLOCKS),
        in_specs=[pl.BlockSpec((t, d), lambda i, b, *_: (i, 0), **once),
                  pl.BlockSpec((t, LANES), lambda i, b, *_: (i, 0), **once),
                  pl.BlockSpec((t, LANES), lambda i, b, *_: (i, 0), **once),
                  pl.BlockSpec((EXPERTS_PER_GROUP, d, 2 * D_EXPERT), wsel),
                  pl.BlockSpec((EXPERTS_PER_GROUP * D_EXPERT, d), lambda i, b, bg_r, *_: (bg_r[i * MOE_BLOCKS + b], 0))],
        out_specs=pl.BlockSpec((t, d), lambda i, b, *_: (i, 0)),
        scratch_shapes=[pltpu.VMEM((SUBLANES, t), F32),
                        pltpu.VMEM((t, 2 * LANES), BF16),
                        pltpu.VMEM((t, d), F32)])
    return pl.pallas_call(
        _moe_kernel,
        grid_spec=grid_spec,
        out_shape=jax.ShapeDtypeStruct((L, d), BF16),
        compiler_params=_cparams("parallel", "arbitrary"),
        name="moe",
    )(bg, nbu, roff, h2, comb, gsel, wgu, wd)


def _moe_plan(gcnt, n_tiles):
    cnt = gcnt[:, 0, 0:N_GROUPS].reshape(n_tiles, -1, N_GROUPS).sum(axis=1).astype(jnp.int32)
    nblk = (cnt + MOE_ROWS - 1) // MOE_ROWS
    bend = jnp.cumsum(nblk, axis=1)
    roff = (bend - nblk) * MOE_ROWS
    nbu = bend[:, -1]
    bidx = jnp.arange(MOE_BLOCKS, dtype=jnp.int32)[None, :]
    grp = jnp.sum((bidx[:, :, None] >= bend[:, None, :]).astype(jnp.int32), axis=2)
    last = jnp.take_along_axis(grp, jnp.maximum(nbu - 1, 0)[:, None], axis=1)
    bg = jnp.where(bidx < nbu[:, None], grp, last)
    bg = jnp.minimum(bg, N_GROUPS - 1)
    return bg.reshape(-1), nbu, roff.reshape(-1)


def _final_kernel(x1_ref, m_ref, mod_ref, fg_ref, o_ref):
    d = D_MODEL
    x2 = x1_ref[...] + mod_ref[0:1, 5 * d:6 * d] * m_ref[...].astype(F32)
    o_ref[...] = _rms(x2) * fg_ref[...]


def _final(x1, m, mod, fg, tm):
    L, d = x1.shape
    return pl.pallas_call(
        _final_kernel,
        grid=(L // tm,),
        in_specs=[pl.BlockSpec((tm, d), lambda i: (i, 0)), pl.BlockSpec((tm, d), lambda i: (i, 0)),
                  pl.BlockSpec(mod.shape, lambda i: (0, 0)), pl.BlockSpec(fg.shape, lambda i: (0, 0))],
        out_specs=pl.BlockSpec((tm, d), lambda i: (i, 0)),
        out_shape=jax.ShapeDtypeStruct((L, d), F32),
        compiler_params=_cparams("parallel"),
        name="final",
    )(x1, m, mod, fg)


def _pad_heads(w, width, slab=HEAD_SLAB):
    lead = w.shape[:-1]
    w = w.reshape(lead + (N_HEADS, width))
    w = jnp.pad(w, [(0, 0)] * len(lead) + [(0, 0), (0, slab - width)])
    return w.reshape(lead + (N_HEADS * slab,))


@functools.lru_cache(maxsize=None)
def _rope_slab_tables(L):
    half = QK_ROPE // 2
    pos = np.arange(L, dtype=np.float64)
    inv = ROPE_THETA ** (-np.arange(0, QK_ROPE, 2, dtype=np.float64) / QK_ROPE)
    ang = pos[:, None] * inv[None, :]
    cos, sin = np.cos(ang), np.sin(ang)
    z = lambda w: np.zeros((L, w))
    cos_t = np.concatenate([np.ones((L, QK_NOPE)), cos, cos, z(HEAD_SLAB - QK_NOPE - QK_ROPE)], axis=1)
    sin_a = np.concatenate([z(QK_NOPE), -sin, z(HEAD_SLAB - QK_NOPE - half)], axis=1)
    sin_b = np.concatenate([z(QK_NOPE + half), sin, z(HEAD_SLAB - QK_NOPE - QK_ROPE)], axis=1)
    return cos_t.astype(np.float32), sin_a.astype(np.float32), sin_b.astype(np.float32)


@functools.lru_cache(maxsize=None)
def _filter_features(L):
    t = np.linspace(0.0, 1.0, L)[None, :]
    t_r = np.arange(L, dtype=np.float64)[None, :]
    bands = np.linspace(1e-4, FILTER_BANDS - 1, FILTER_BANDS)[:, None]
    ang = 2.0 * math.pi * bands * t_r / L
    z = np.concatenate([t, np.cos(ang), -np.sin(ang)], axis=0)
    return np.pad(z, ((0, FILTER_HIDDEN - FILTER_EMB), (0, 0))).astype(np.float32)


@functools.lru_cache(maxsize=None)
def _dft_tables(L):
    n = 2 * L
    n2 = DFT_N2
    n1 = n // n2
    n1h = n1 // 2
    k1n = n1h + 1
    k1p = -(-k1n // SUBLANES) * SUBLANES
    two_pi = 2.0 * math.pi
    k1 = np.arange(k1p, dtype=np.int64)[:, None]
    live = (k1 < k1n).astype(np.float64)
    m1 = np.arange(n1h, dtype=np.int64)[None, :]
    ang1 = two_pi * ((k1 * m1) % n1) / n1
    f1m = np.concatenate([np.cos(ang1) * live, -np.sin(ang1) * live], axis=0).astype(BF16)
    m2 = np.arange(n2, dtype=np.int64)[None, :]
    ang2 = two_pi * ((k1 * m2) % n) / n
    tr = (np.cos(ang2) * live).astype(np.float32)
    ti = (-np.sin(ang2) * live).astype(np.float32)
    ang3 = two_pi * ((m2.T * m2) % n2) / n2
    f2r, f2i = np.cos(ang3), -np.sin(ang3)
    f2 = np.block([[f2r, f2i], [-f2i, f2r]]).astype(BF16)
    f2inv = np.block([[f2r, -f2i], [f2i, f2r]]).astype(BF16)
    o1 = np.arange(n1h, dtype=np.int64)[:, None]
    q1 = np.arange(n1h, dtype=np.int64)[None, :]
    ang4 = two_pi * ((o1 * q1) % n1) / n1
    wgt = np.where(q1 == 0, 1.0, 2.0)
    cmat = np.concatenate([wgt * np.cos(ang4), -wgt * np.sin(ang4)], axis=1).astype(BF16)
    sgn = np.broadcast_to(np.where(o1 % 2 == 0, 1.0, -1.0), (n1h, n2)).astype(np.float32)
    return (f1m, tr, ti, f2, f2inv, cmat, sgn), n1h


def _pick(n, pref):
    t = pref
    while n % t:
        t //= 2
    return t


def kernel(x, c, ada_w, ada_b, norm_mix_g, w_in, hy_conv_w, hy_conv_b, filt_w1, filt_b1, filt_freq1, filt_w2, filt_b2, filt_freq2, filt_w3, hy_bias, q_norm_g, kv_norm_g, w_uq, w_uk, w_uv, w_hy_out, w_mla_out, w_mix_out, norm_ffn_g, w_group, b_group, w_router, b_router, w_gate, w_up, w_down, final_norm_g):
    B, L, d = x.shape
    assert B == 1 and d == D_MODEL and ada_w.shape[0] == 1 and L % (2 * DFT_N2) == 0
    x2 = x.reshape(L, d)
    row = lambda v: v.reshape(1, -1)
    col = lambda v: v.reshape(-1, 1)

    mod = _ada(jnp.pad(c, ((0, 8 - B), (0, 0))), ada_w[0], row(ada_b[0]))

    wi = w_in[0]
    kr_slab = jnp.pad(wi[:, OFF_KR:OFF_GATE], ((0, 0), (QK_NOPE, HEAD_SLAB - QK_NOPE - QK_ROPE)))
    wcat = jnp.concatenate([wi[:, OFF_Q:OFF_KV], wi[:, OFF_KV:OFF_KR], wi[:, OFF_GATE:], kr_slab],
                           axis=1).astype(BF16)
    whyt = wi[:, 0:OFF_Q].T.astype(BF16)
    wuq = _pad_heads(w_uq[0], QK_NOPE + QK_ROPE).astype(BF16)
    wuk = _pad_heads(w_uk[0].reshape(KV_LORA, N_HEADS * QK_NOPE), QK_NOPE).astype(BF16)
    wuvt = _pad_heads(w_uv[0].reshape(KV_LORA, N_HEADS * V_HEAD), V_HEAD, V_ROWS).T.astype(BF16)
    tkv = _pick(L, 1024)
    cos_t, sin_a, sin_b = _rope_slab_tables(L)

    hypt, gates, q, k, vt, kn2_tiles = _inproj(x2, mod, row(norm_mix_g[0]), wcat, whyt, row(q_norm_g[0]),
                                    row(kv_norm_g[0]), wuq, wuk, wuvt, cos_t, sin_a, sin_b,
                                    _pick(L, 512), tkv)

    zt = _filter_features(L)
    w1t = jnp.pad(filt_w1[0], ((0, FILTER_HIDDEN - FILTER_EMB), (0, 0))).T
    dl = np.abs(np.linspace(MIN_DECAY, MAX_DECAY, HY_WIDTH))[:, None].astype(np.float32)
    ht, nrm = _filt(zt, w1t, col(filt_b1[0]), col(filt_freq1[0]), filt_w2[0].T, col(filt_b2[0]),
                    col(filt_freq2[0]), filt_w3[0].T.astype(BF16), dl, _pick(L, 1024))
    tabs, n1h = _dft_tables(L)
    cw, cbias = hy_conv_w[0], hy_conv_b[0]
    w = HY_WIDTH
    taps = lambda s: [cw[0, s * w:(s + 1) * w], cw[1, s * w:(s + 1) * w], cw[2, s * w:(s + 1) * w],
                      cbias[s * w:(s + 1) * w]]
    prm = jnp.stack(taps(0) + taps(1) + taps(2) + [hy_bias[0], nrm[:, 0]]
                    + [jnp.zeros((w,), F32)] * (HY_PARAM_ROWS - 14), axis=1)
    prm = jnp.broadcast_to(prm[:, :, None], (w, HY_PARAM_ROWS, DFT_N2))
    yht = _hyena(hypt.reshape(3 * w, n1h, DFT_N2), ht.reshape(2 * w, n1h, DFT_N2), prm, tabs,
                 _pick(w, 16), 1.0 / (2 * L)).reshape(w, L)

    kn2 = jnp.broadcast_to(jnp.max(kn2_tiles, axis=0)[:, None, :], (N_HEADS, SUBLANES, LANES))
    ym = _attn(q, k, vt, kn2, _pick(L, 1024), 16)

    wmla = jnp.pad(w_mla_out[0].reshape(N_HEADS, V_HEAD, d),
                   ((0, 0), (0, V_ROWS - V_HEAD), (0, 0))).reshape(V_ALL, d).astype(BF16)
    wrt = jnp.pad(jnp.concatenate([w_router[0], w_group[0]], axis=1),
                  ((0, 0), (0, LANES - N_EXPERTS - N_GROUPS)))
    brt = jnp.pad(jnp.concatenate([b_router[0], b_group[0]]), (0, LANES - N_EXPERTS - N_GROUPS))[None, :]
    x1, h2, comb, gsel, gcnt = _merge(x2, yht, ym, gates, mod, w_hy_out[0].astype(BF16), wmla,
                                      w_mix_out[0].astype(BF16), row(norm_ffn_g[0]), wrt, brt,
                                      _pick(L, 1024))

    assert L % MOE_TILE == 0
    bg, nbu, roff = _moe_plan(gcnt, L // MOE_TILE)
    wgu = jnp.concatenate([w_gate[0], w_up[0]], axis=2).astype(BF16)
    wd = w_down[0].reshape(N_EXPERTS * D_EXPERT, d).astype(BF16)
    moe = _moe(bg, nbu, roff, h2, comb, gsel, wgu, wd)
    out = _final(x1, moe, mod, row(final_norm_g), _pick(L, 512))
    return out.reshape(B, L, d)
```

```python
import functools
import math

import numpy as np
import jax
import jax.numpy as jnp
from jax import lax
from jax.experimental import pallas as pl
from jax.experimental.pallas import tpu as pltpu

F32 = jnp.float32
BF16 = jnp.bfloat16
HIGHEST = lax.Precision.HIGHEST

D_MODEL = 1024
EPS = 1e-6
HY_WIDTH = 512
FILTER_BANDS = 16
FILTER_EMB = 1 + 2 * FILTER_BANDS
FILTER_HIDDEN = 64
FILTER_DECAY_TARGET = 1e-2
FAST_DECAY_PCT = 0.3
SLOW_DECAY_PCT = 1.5
FILTER_SHIFT = 0.05
MIN_DECAY = math.log(FILTER_DECAY_TARGET) / SLOW_DECAY_PCT
MAX_DECAY = math.log(FILTER_DECAY_TARGET) / FAST_DECAY_PCT
N_HEADS = 8
QK_NOPE = 64
QK_ROPE = 32
V_HEAD = 64
Q_LORA = 384
KV_LORA = 256
ROPE_THETA = 10000.0
MLA_WIDTH = N_HEADS * V_HEAD
OFF_Q = 3 * HY_WIDTH
OFF_KV = OFF_Q + Q_LORA
OFF_KR = OFF_KV + KV_LORA
OFF_GATE = OFF_KR + QK_ROPE
N_GROUPS = 4
EXPERTS_PER_GROUP = 8
N_EXPERTS = N_GROUPS * EXPERTS_PER_GROUP
D_EXPERT = 256

LANES = 128
SUBLANES = 8
HEAD_SLAB = LANES
QK_SLABS = N_HEADS * HEAD_SLAB
ONES_LANE = V_HEAD
DFT_N2 = LANES
HY_PARAM_ROWS = 16
NEG_BIG = -1e30
VMEM_LIMIT = 56 * 1024 * 1024
MAX_STATIC_SHIFT = 60.0
SHIFT_SLACK = 1.01
SHIFT_EPS = 1e-3

C_Q = 0
C_KV = C_Q + Q_LORA
C_GATE = C_KV + KV_LORA
C_KR = C_GATE + 2 * D_MODEL
C_END = C_KR + HEAD_SLAB


def _cparams(*sem):
    return pltpu.CompilerParams(dimension_semantics=sem, vmem_limit_bytes=VMEM_LIMIT)


def _rms(x):
    return x * lax.rsqrt(jnp.mean(x * x, axis=-1, keepdims=True) + EPS)


def _dot_nt(a, b):
    return lax.dot_general(a, b, (((1,), (1,)), ((), ())), preferred_element_type=F32)


def _dot_tn(a, b):
    return lax.dot_general(a, b, (((0,), (0,)), ((), ())), preferred_element_type=F32)


def _ada_kernel(c_ref, w_ref, b_ref, o_ref):
    c = c_ref[...]
    ca = c * jax.nn.sigmoid(c)
    o_ref[...] = jnp.dot(ca, w_ref[...], preferred_element_type=F32, precision=HIGHEST) + b_ref[...]


def _ada(c8, w, b):
    d, n = w.shape
    tn = 1024
    return pl.pallas_call(
        _ada_kernel,
        grid=(n // tn,),
        in_specs=[pl.BlockSpec((8, d), lambda j: (0, 0)),
                  pl.BlockSpec((d, tn), lambda j: (0, j)),
                  pl.BlockSpec((1, tn), lambda j: (0, j))],
        out_specs=pl.BlockSpec((8, tn), lambda j: (0, j)),
        out_shape=jax.ShapeDtypeStruct((8, n), F32),
        compiler_params=_cparams("parallel"),
        name="ada",
    )(c8, w, b)


def _rope_slab(z, cos_t, sin_a, sin_b):
    return z * cos_t + pltpu.roll(z, HEAD_SLAB - QK_ROPE // 2, 1) * sin_a + pltpu.roll(z, QK_ROPE // 2, 1) * sin_b


def _inproj_kernel(x_ref, mod_ref, g_ref, wcat_ref, whyt_ref, gq_ref, gkv_ref, wuq_ref, wuk_ref, wuvt_ref,
                   cos_ref, sina_ref, sinb_ref,
                   hypt_ref, gate_ref, q_ref, k_ref, vt_ref, kn2_ref, *, qscale):
    d = D_MODEL
    x = x_ref[...]
    sh1 = mod_ref[0:1, 0:d]
    sc1 = mod_ref[0:1, d:2 * d]
    hb = (_rms(x) * g_ref[...] * (1.0 + sc1) + sh1).astype(BF16)
    hypt_ref[...] = _dot_nt(whyt_ref[...], hb).astype(BF16)
    proj = jnp.dot(hb, wcat_ref[...], preferred_element_type=F32)
    gate_ref[...] = jax.nn.sigmoid(proj[:, C_GATE:C_KR]).astype(BF16)
    cq = _rms(proj[:, C_Q:C_KV]) * gq_ref[...]
    ckv = (_rms(proj[:, C_KV:C_GATE]) * gkv_ref[...]).astype(BF16)
    q = jnp.dot(cq.astype(BF16), wuq_ref[...], preferred_element_type=F32)
    kn = jnp.dot(ckv, wuk_ref[...], preferred_element_type=F32)
    vt = _dot_nt(wuvt_ref[...], ckv)
    cos_t = cos_ref[...]
    sin_a = sina_ref[...]
    sin_b = sinb_ref[...]
    kr = _rope_slab(proj[:, C_KR:C_END], cos_t, sin_a, sin_b)
    srow = lax.broadcasted_iota(jnp.int32, (HEAD_SLAB, 1), 0)
    ones_row = jnp.where(srow == ONES_LANE, 1.0, 0.0).astype(F32)
    kn2 = []
    for hd in range(N_HEADS):
        sl = slice(hd * HEAD_SLAB, (hd + 1) * HEAD_SLAB)
        q_ref[:, sl] = (_rope_slab(q[:, sl], cos_t, sin_a, sin_b) * qscale).astype(BF16)
        kb = (kn[:, sl] + kr).astype(BF16)
        k_ref[:, sl] = kb
        vt_ref[hd, 0] = (vt[sl] + ones_row).astype(BF16)
        kf = kb.astype(F32)
        big = jnp.max(jnp.sum(kf * kf, axis=1, keepdims=True), axis=0, keepdims=True)
        kn2.append(jnp.broadcast_to(big, (1, HEAD_SLAB)))
    kn2_ref[0] = jnp.concatenate(kn2, axis=0)


def _inproj(x2, mod, g1, wcat, whyt, gq, gkv, wuq, wuk, wuvt, cos_t, sin_a, sin_b, tm, tkv):
    L, d = x2.shape
    qscale = (QK_NOPE + QK_ROPE) ** -0.5 * math.log2(math.e)
    full = lambda a: pl.BlockSpec(a.shape, lambda i: (0,) * a.ndim)
    row = lambda w: pl.BlockSpec((tm, w), lambda i: (i, 0))
    per = tkv // tm
    return pl.pallas_call(
        functools.partial(_inproj_kernel, qscale=qscale),
        grid=(L // tm,),
        in_specs=[row(d), full(mod), full(g1), full(wcat), full(whyt), full(gq), full(gkv), full(wuq),
                  full(wuk), full(wuvt), row(HEAD_SLAB), row(HEAD_SLAB), row(HEAD_SLAB)],
        out_specs=[pl.BlockSpec((3 * HY_WIDTH, tm), lambda i: (0, i)),
                   row(2 * D_MODEL), row(QK_SLABS), row(QK_SLABS),
                   pl.BlockSpec((N_HEADS, 1, HEAD_SLAB, tm), lambda i: (0, i // per, 0, i % per)),
                   pl.BlockSpec((1, N_HEADS, HEAD_SLAB), lambda i: (i, 0, 0))],
        out_shape=[jax.ShapeDtypeStruct((3 * HY_WIDTH, L), BF16),
                   jax.ShapeDtypeStruct((L, 2 * D_MODEL), BF16),
                   jax.ShapeDtypeStruct((L, QK_SLABS), BF16),
                   jax.ShapeDtypeStruct((L, QK_SLABS), BF16),
                   jax.ShapeDtypeStruct((N_HEADS, L // tkv, HEAD_SLAB, tkv), BF16),
                   jax.ShapeDtypeStruct((L // tm, N_HEADS, HEAD_SLAB), F32)],
        compiler_params=_cparams("parallel"),
        name="inproj",
    )(x2, mod, g1, wcat, whyt, gq, gkv, wuq, wuk, wuvt, cos_t, sin_a, sin_b)


def _filt_kernel(z_ref, w1_ref, b1_ref, f1_ref, w2_ref, b2_ref, f2_ref, w3_ref, dl_ref,
                 h_ref, nrm_ref):
    i = pl.program_id(0)
    z = z_ref[...]
    tm = z.shape[1]
    h = jnp.sin(f1_ref[...] * (jnp.dot(w1_ref[...], z, preferred_element_type=F32, precision=HIGHEST)
                               + b1_ref[...]))
    h = jnp.sin(f2_ref[...] * (jnp.dot(w2_ref[...], h, preferred_element_type=F32, precision=HIGHEST)
                               + b2_ref[...]))
    h3 = jnp.dot(w3_ref[...], h.astype(BF16), preferred_element_type=F32)
    t = z[0:1, :]
    window = jnp.exp(-dl_ref[...] * t) + FILTER_SHIFT
    hf = h3[0:HY_WIDTH] * window
    hb = h3[HY_WIDTH:2 * HY_WIDTH] * window
    col = lax.broadcasted_iota(jnp.int32, (1, tm), 1)
    hb = jnp.where((col == 0) & (i == 0), 0.0, hb)
    h_ref[0:HY_WIDTH, :] = hf.astype(BF16)
    h_ref[HY_WIDTH:2 * HY_WIDTH, :] = hb.astype(BF16)
    part = jnp.sum(jnp.abs(hf) + jnp.abs(hb), axis=1, keepdims=True)

    @pl.when(i == 0)
    def _():
        nrm_ref[...] = jnp.zeros_like(nrm_ref)

    nrm_ref[...] += part


def _filt(zt, w1t, b1, f1, w2t, b2, f2, w3t, dl, tm):
    L = zt.shape[1]
    full = lambda a: pl.BlockSpec(a.shape, lambda i: (0,) * a.ndim)
    return pl.pallas_call(
        _filt_kernel,
        grid=(L // tm,),
        in_specs=[pl.BlockSpec((zt.shape[0], tm), lambda i: (0, i)), full(w1t), full(b1), full(f1),
                  full(w2t), full(b2), full(f2), full(w3t), full(dl)],
        out_specs=[pl.BlockSpec((2 * HY_WIDTH, tm), lambda i: (0, i)),
                   pl.BlockSpec((HY_WIDTH, 1), lambda i: (0, 0))],
        out_shape=[jax.ShapeDtypeStruct((2 * HY_WIDTH, L), BF16),
                   jax.ShapeDtypeStruct((HY_WIDTH, 1), F32)],
        compiler_params=_cparams("arbitrary"),
        name="filt",
    )(zt, w1t, b1, f1, w2t, b2, f2, w3t, dl)


def _hyena_kernel(x0_ref, x1_ref, v_ref, hf_ref, hb_ref, prm_ref, f1_ref, tr_ref, ti_ref, f2_ref,
                  f2i_ref, cm_ref, sg_ref, o_ref, *, inv_n):
    cb, n1h, w = x0_ref.shape
    k1p = tr_ref.shape[0]
    lane = lax.broadcasted_iota(jnp.int32, (n1h, w), 1)
    rowi = lax.broadcasted_iota(jnp.int32, (n1h, w), 0)
    tr = tr_ref[...]
    ti = ti_ref[...]

    def conv(x, p, j):
        r = pltpu.roll(x, 1, 1)
        up = jnp.where(lane == 0, jnp.where(rowi == 0, 0.0, pltpu.roll(r, 1, 0)), r)
        l = pltpu.roll(x, w - 1, 1)
        dn = jnp.where(lane == w - 1, jnp.where(rowi == n1h - 1, 0.0, pltpu.roll(l, n1h - 1, 0)), l)
        return up * p[j:j + 1] + x * p[j + 1:j + 2] + dn * p[j + 2:j + 3] + p[j + 3:j + 4]

    def twiddle(a, s):
        ar = a[0:k1p, s * w:(s + 1) * w]
        ai = a[k1p:2 * k1p, s * w:(s + 1) * w]
        return jnp.concatenate([ar * tr - ai * ti, ar * ti + ai * tr], axis=1).astype(BF16)

    prm = [prm_ref[c] for c in range(cb)]
    x0s, us, sigs = [], [], []
    for c in range(cb):
        p = prm[c]
        x0s.append(conv(x0_ref[c].astype(F32), p, 0))
        u = conv(v_ref[c].astype(F32), p, 8) * conv(x1_ref[c].astype(F32), p, 4)
        us.append(u)
        sigs += [u.astype(BF16), hf_ref[c], hb_ref[c]]
    a = jnp.dot(f1_ref[...], jnp.concatenate(sigs, axis=1), preferred_element_type=F32)
    b = jnp.concatenate([twiddle(a, s) for s in range(3 * cb)], axis=0)
    s = jnp.dot(b, f2_ref[...], preferred_element_type=F32)
    ys = []
    for c in range(cb):
        su = s[(3 * c) * k1p:(3 * c + 1) * k1p]
        sf = s[(3 * c + 1) * k1p:(3 * c + 2) * k1p]
        sb = s[(3 * c + 2) * k1p:(3 * c + 3) * k1p]
        inv = 1.0 / prm[c][13:14]
        gr = (sf[:, 0:w] + sb[:, 0:w]) * inv
        gi = (sf[:, w:2 * w] - sb[:, w:2 * w]) * inv
        ur, ui = su[:, 0:w], su[:, w:2 * w]
        ys.append(jnp.concatenate([ur * gr - ui * gi, ur * gi + ui * gr], axis=1).astype(BF16))
    zt = jnp.dot(jnp.concatenate(ys, axis=0), f2i_ref[...], preferred_element_type=F32)
    zss, nyq = [], []
    for c in range(cb):
        ztr = zt[c * k1p:(c + 1) * k1p, 0:w]
        zti = zt[c * k1p:(c + 1) * k1p, w:2 * w]
        zr = ztr * tr + zti * ti
        zi = zti * tr - ztr * ti
        zss.append(jnp.concatenate([zr[0:n1h], zi[0:n1h]], axis=0).astype(BF16))
        nyq.append(zr[n1h:n1h + 1])
    yy = jnp.dot(cm_ref[...], jnp.concatenate(zss, axis=1), preferred_element_type=F32)
    for c in range(cb):
        conv_out = (yy[:, c * w:(c + 1) * w] + sg_ref[...] * nyq[c]) * inv_n
        o_ref[c] = (x0s[c] * (conv_out + prm[c][12:13] * us[c])).astype(BF16)


def _hyena(hyp3, h3, prm, tabs, cb, inv_n):
    c3, n1h, w = hyp3.shape
    c = c3 // 3
    nb = c // cb
    f1m, tr, ti, f2, f2i, cmat, sgn = tabs
    full = lambda a: pl.BlockSpec(a.shape, lambda i: (0,) * a.ndim)
    blk = lambda off: pl.BlockSpec((cb, n1h, w), lambda i: (off * nb + i, 0, 0))
    return pl.pallas_call(
        functools.partial(_hyena_kernel, inv_n=inv_n),
        grid=(nb,),
        in_specs=[blk(0), blk(1), blk(2), blk(0), blk(1),
                  pl.BlockSpec((cb, HY_PARAM_ROWS, w), lambda i: (i, 0, 0)),
                  full(f1m), full(tr), full(ti), full(f2), full(f2i), full(cmat), full(sgn)],
        out_specs=pl.BlockSpec((cb, n1h, w), lambda i: (i, 0, 0)),
        out_shape=jax.ShapeDtypeStruct((c, n1h, w), BF16),
        compiler_params=_cparams("parallel"),
        name="hyena",
    )(hyp3, hyp3, hyp3, h3, h3, prm, f1m, tr, ti, f2, f2i, cmat, sgn)


def _attn_kernel(q_ref, k_ref, vt_ref, kn2_ref, o_ref, *, unroll):
    tq = q_ref.shape[0]
    nk, _, tk = vt_ref.shape[1:]
    q = q_ref[...]

    def finish(acc):
        o_ref[...] = (acc / acc[ONES_LANE:ONES_LANE + 1, :]).astype(BF16)

    qf = q.astype(F32)
    qn2 = _dot_nt(jnp.ones((SUBLANES, HEAD_SLAB), BF16), (qf * qf).astype(BF16))[0:1]
    bound = jnp.sqrt(qn2 * kn2_ref[0][0:1, 0:1]) * SHIFT_SLACK + SHIFT_EPS
    static_shift_ok = jnp.max(bound) <= MAX_STATIC_SHIFT

    @pl.when(static_shift_ok)
    def _():
        def body(j, acc):
            off = pl.multiple_of(j * tk, tk)
            st = _dot_nt(k_ref[pl.ds(off, tk), :], q)
            p = jnp.exp2(st - bound).astype(BF16)
            return acc + jnp.dot(vt_ref[0, j], p, preferred_element_type=F32)

        finish(lax.fori_loop(0, nk, body, jnp.zeros((HEAD_SLAB, tq), F32), unroll=unroll))

    @pl.when(jnp.logical_not(static_shift_ok))
    def _():
        def body(j, carry):
            m, acc = carry
            off = pl.multiple_of(j * tk, tk)
            st = _dot_nt(k_ref[pl.ds(off, tk), :], q)
            m_new = jnp.maximum(m, jnp.max(st, axis=0, keepdims=True))
            alpha = jnp.exp2(m - m_new)
            p = jnp.exp2(st - m_new)
            acc = alpha * acc + jnp.dot(vt_ref[0, j], p.astype(BF16), preferred_element_type=F32)
            return m_new, acc

        m0 = jnp.full((1, tq), NEG_BIG, F32)
        acc0 = jnp.zeros((HEAD_SLAB, tq), F32)
        finish(lax.fori_loop(0, nk, body, (m0, acc0), unroll=unroll)[1])


def _attn(q, k, vt, kn2, tq, unroll):
    L = q.shape[0]
    _, nk, _, tk = vt.shape
    return pl.pallas_call(
        functools.partial(_attn_kernel, unroll=unroll),
        grid=(N_HEADS, L // tq),
        in_specs=[pl.BlockSpec((tq, HEAD_SLAB), lambda h, i: (i, h)),
                  pl.BlockSpec((L, HEAD_SLAB), lambda h, i: (0, h)),
                  pl.BlockSpec((1, nk, HEAD_SLAB, tk), lambda h, i: (h, 0, 0, 0)),
                  pl.BlockSpec((1, SUBLANES, LANES), lambda h, i: (h, 0, 0))],
        out_specs=pl.BlockSpec((HEAD_SLAB, tq), lambda h, i: (h, i)),
        out_shape=jax.ShapeDtypeStruct((QK_SLABS, L), BF16),
        compiler_params=_cparams("parallel", "parallel"),
        name="attn",
    )(q, k, vt, kn2)


def _merge_kernel(x_ref, yht_ref, ymt_ref, gate_ref, mod_ref, why_ref, wmla_ref, wmix_ref, g2_ref,
                  wrt_ref, brt_ref, x1_ref, h2_ref, comb_ref, gsel_ref, gcnt_ref):
    d = D_MODEL
    a = _dot_tn(yht_ref[...], why_ref[...])
    b = _dot_tn(ymt_ref[...], wmla_ref[...])
    mix = gate_ref[:, 0:d].astype(F32) * a + gate_ref[:, d:2 * d].astype(F32) * b
    o = jnp.dot(mix.astype(BF16), wmix_ref[...], preferred_element_type=F32)
    gt1 = mod_ref[0:1, 2 * d:3 * d]
    sh2 = mod_ref[0:1, 3 * d:4 * d]
    sc2 = mod_ref[0:1, 4 * d:5 * d]
    x1 = x_ref[...] + gt1 * o
    x1_ref[...] = x1
    h2 = _rms(x1) * g2_ref[...] * (1.0 + sc2) + sh2
    h2_hi = h2.astype(BF16)
    h2_ref[...] = h2_hi
    h2_lo = (h2 - h2_hi.astype(F32)).astype(BF16)
    wrt = wrt_ref[...]
    w_hi = wrt.astype(BF16)
    w_lo = (wrt - w_hi.astype(F32)).astype(BF16)
    e = (jnp.dot(h2_hi, w_hi, preferred_element_type=F32)
         + (jnp.dot(h2_lo, w_hi, preferred_element_type=F32)
            + jnp.dot(h2_hi, w_lo, preferred_element_type=F32))) + brt_ref[...]
    lane = lax.broadcasted_iota(jnp.int32, e.shape, 1).astype(F32)
    gvalid = (lane >= N_EXPERTS) & (lane < N_EXPERTS + N_GROUPS)
    glm = jnp.where(gvalid, e, NEG_BIG)
    gmax = jnp.max(glm, axis=1, keepdims=True)
    gidx = jnp.min(jnp.where(glm == gmax, lane, float(LANES)), axis=1, keepdims=True) - float(N_EXPERTS)
    psum = jnp.sum(jnp.where(gvalid, jnp.exp(glm - gmax), 0.0), axis=1, keepdims=True)
    p_sel = 1.0 / psum
    lo = gidx * EXPERTS_PER_GROUP
    em = jnp.where((lane >= lo) & (lane < lo + EXPERTS_PER_GROUP), e, NEG_BIG)
    v1 = jnp.max(em, axis=1, keepdims=True)
    i1 = jnp.min(jnp.where(em == v1, lane, float(LANES)), axis=1, keepdims=True)
    em2 = jnp.where(lane == i1, NEG_BIG, em)
    v2 = jnp.max(em2, axis=1, keepdims=True)
    i2 = jnp.min(jnp.where(em2 == v2, lane, float(LANES)), axis=1, keepdims=True)
    t = jnp.exp(v2 - v1)
    w1 = p_sel / (1.0 + t)
    w2 = p_sel * t / (1.0 + t)
    comb_ref[...] = jnp.where(lane == i1, w1, 0.0) + jnp.where(lane == i2, w2, 0.0)
    gsel = jnp.where(lane == gidx, 1.0, 0.0)
    gsel_ref[...] = gsel.astype(BF16)
    srow = lax.broadcasted_iota(jnp.int32, (SUBLANES, LANES), 0)
    gcnt_ref[0] = jnp.where(srow == 0, jnp.sum(gsel, axis=0, keepdims=True), 0.0)


def _merge(x2, yht, ym, gates, mod, why, wmla, wmix, g2, wrt, brt, tm):
    L, d = x2.shape
    full = lambda a: pl.BlockSpec(a.shape, lambda i: (0,) * a.ndim)
    row = lambda w: pl.BlockSpec((tm, w), lambda i: (i, 0))
    return pl.pallas_call(
        _merge_kernel,
        grid=(L // tm,),
        in_specs=[row(d), pl.BlockSpec((HY_WIDTH, tm), lambda i: (0, i)),
                  pl.BlockSpec((QK_SLABS, tm), lambda i: (0, i)), row(2 * d),
                  full(mod), full(why), full(wmla), full(wmix), full(g2), full(wrt), full(brt)],
        out_specs=[row(d), row(d), row(LANES), row(LANES),
                   pl.BlockSpec((1, SUBLANES, LANES), lambda i: (i, 0, 0))],
        out_shape=[jax.ShapeDtypeStruct((L, d), F32),
                   jax.ShapeDtypeStruct((L, d), BF16),
                   jax.ShapeDtypeStruct((L, LANES), F32),
                   jax.ShapeDtypeStruct((L, LANES), BF16),
                   jax.ShapeDtypeStruct((L // tm, SUBLANES, LANES), F32)],
        compiler_params=_cparams("parallel"),
        name="merge",
    )(x2, yht, ym, gates, mod, why, wmla, wmix, g2, wrt, brt)


MOE_TILE = 1024
MOE_ROWS = 256
MOE_BLOCKS = MOE_TILE // MOE_ROWS + N_GROUPS
POS_RADIX = 64


def _moe_kernel(bg_ref, nbu_ref, roff_ref, h_ref, comb_ref, gsel_ref, wgu_ref, wd_ref, o_ref,
                pos_ref, cs_ref, acc_ref):
    i = pl.program_id(0)
    b = pl.program_id(1)
    t = h_ref.shape[0]
    lane = lax.broadcasted_iota(jnp.int32, (1, LANES), 1)

    @pl.when(b == 0)
    def _():
        oh = gsel_ref[...]
        sub = MOE_ROWS
        tri = jnp.where(lax.broadcasted_iota(jnp.int32, (sub, sub), 1)
                        < lax.broadcasted_iota(jnp.int32, (sub, sub), 0), 1.0, 0.0).astype(BF16)
        carry = jnp.zeros((1, LANES), F32)
        cums = []
        for s in range(t // sub):
            blk = oh[s * sub:(s + 1) * sub]
            cums.append(jnp.dot(tri, blk, preferred_element_type=F32) + carry)
            carry = carry + jnp.sum(blk.astype(F32), axis=0, keepdims=True)
        cum = jnp.concatenate(cums, axis=0)
        offs = jnp.zeros((1, LANES), F32)
        for g in range(N_GROUPS):
            offs = jnp.where(lane == g, roff_ref[i * N_GROUPS + g].astype(F32), offs)
        pos = jnp.sum(jnp.where(oh > 0, cum + offs, 0.0), axis=1, keepdims=True)
        hi = jnp.floor(pos * (1.0 / POS_RADIX))
        lo = pos - hi * POS_RADIX
        digits = jnp.where(lane == 0, hi, jnp.where(lane == 1, lo, 0.0)).astype(BF16)
        srow = lax.broadcasted_iota(jnp.int32, (SUBLANES, LANES), 0)
        scol = lax.broadcasted_iota(jnp.int32, (SUBLANES, LANES), 1)
        pick = jnp.where(srow == scol, 1.0, 0.0).astype(BF16)
        rows = _dot_nt(pick, digits)
        pos_ref[...] = jnp.broadcast_to(rows[0:1] * POS_RADIX + rows[1:2], pos_ref.shape)
        comb = comb_ref[...]
        comb_hi = comb.astype(BF16)
        cs_ref[:, 0:LANES] = comb_hi
        cs_ref[:, LANES:2 * LANES] = (comb - comb_hi.astype(F32)).astype(BF16)
        acc_ref[...] = jnp.zeros_like(acc_ref)

    @pl.when(b < nbu_ref[i])
    def _():
        g = bg_ref[i * MOE_BLOCKS + b]
        rid = (lax.broadcasted_iota(jnp.int32, (MOE_ROWS, 1), 0) + b * MOE_ROWS).astype(F32)
        sel = jnp.where(rid == pos_ref[0:1, :], 1.0, 0.0).astype(BF16)
        xs = jnp.dot(sel, h_ref[...], preferred_element_type=F32).astype(BF16)
        cs = jnp.dot(sel, cs_ref[...], preferred_element_type=F32)
        comb_s = cs[:, 0:LANES] + cs[:, LANES:2 * LANES]
        parts = []
        for j in range(EXPERTS_PER_GROUP):
            gu = jnp.dot(xs, wgu_ref[j], preferred_element_type=F32)
            gt = gu[:, 0:D_EXPERT]
            a = gt * jax.nn.sigmoid(gt) * gu[:, D_EXPERT:2 * D_EXPERT]
            col = jnp.sum(jnp.where(lane == g * EXPERTS_PER_GROUP + j, comb_s, 0.0), axis=1, keepdims=True)
            parts.append((a * col).astype(BF16))
        y = jnp.dot(jnp.concatenate(parts, axis=1), wd_ref[...], preferred_element_type=F32)
        acc_ref[...] += _dot_tn(sel, y.astype(BF16))

    @pl.when(b == pl.num_programs(1) - 1)
    def _():
        o_ref[...] = acc_ref[...].astype(BF16)


def _moe(bg, nbu, roff, h2, comb, gsel, wgu, wd):
    L, d = h2.shape
    t = MOE_TILE
    once = dict(pipeline_mode=pl.Buffered(1))
    wsel = lambda i, b, bg_r, nbu_r, roff_r: (bg_r[i * MOE_BLOCKS + b], 0, 0)
    grid_spec = pltpu.PrefetchScalarGridSpec(
        num_scalar_prefetch=3,
        grid=(L // t, MOE_BLOCKS),
        in_specs=[pl.BlockSpec((t, d), lambda i, b, *_: (i, 0), **once),
                  pl.BlockSpec((t, LANES), lambda i, b, *_: (i, 0), **once),
                  pl.BlockSpec((t, LANES), lambda i, b, *_: (i, 0), **once),
                  pl.BlockSpec((EXPERTS_PER_GROUP, d, 2 * D_EXPERT), wsel),
                  pl.BlockSpec((EXPERTS_PER_GROUP * D_EXPERT, d), lambda i, b, bg_r, *_: (bg_r[i * MOE_BLOCKS + b], 0))],
        out_specs=pl.BlockSpec((t, d), lambda i, b, *_: (i, 0)),
        scratch_shapes=[pltpu.VMEM((SUBLANES, t), F32),
                        pltpu.VMEM((t, 2 * LANES), BF16),
                        pltpu.VMEM((t, d), F32)])
    return pl.pallas_call(
        _moe_kernel,
        grid_spec=grid_spec,
        out_shape=jax.ShapeDtypeStruct((L, d), BF16),
        compiler_params=_cparams("parallel", "arbitrary"),
        name="moe",
    )(bg, nbu, roff, h2, comb, gsel, wgu, wd)


def _moe_plan(gcnt, n_tiles):
    cnt = gcnt[:, 0, 0:N_GROUPS].reshape(n_tiles, -1, N_GROUPS).sum(axis=1).astype(jnp.int32)
    nblk = (cnt + MOE_ROWS - 1) // MOE_ROWS
    bend = jnp.cumsum(nblk, axis=1)
    roff = (bend - nblk) * MOE_ROWS
    nbu = bend[:, -1]
    bidx = jnp.arange(MOE_BLOCKS, dtype=jnp.int32)[None, :]
    grp = jnp.sum((bidx[:, :, None] >= bend[:, None, :]).astype(jnp.int32), axis=2)
    last = jnp.take_along_axis(grp, jnp.maximum(nbu - 1, 0)[:, None], axis=1)
    bg = jnp.where(bidx < nbu[:, None], grp, last)
    bg = jnp.minimum(bg, N_GROUPS - 1)
    return bg.reshape(-1), nbu, roff.reshape(-1)


def _final_kernel(x1_ref, m_ref, mod_ref, fg_ref, o_ref):
    d = D_MODEL
    x2 = x1_ref[...] + mod_ref[0:1, 5 * d:6 * d] * m_ref[...].astype(F32)
    o_ref[...] = _rms(x2) * fg_ref[...]


def _final(x1, m, mod, fg, tm):
    L, d = x1.shape
    return pl.pallas_call(
        _final_kernel,
        grid=(L // tm,),
        in_specs=[pl.BlockSpec((tm, d), lambda i: (i, 0)), pl.BlockSpec((tm, d), lambda i: (i, 0)),
                  pl.BlockSpec(mod.shape, lambda i: (0, 0)), pl.BlockSpec(fg.shape, lambda i: (0, 0))],
        out_specs=pl.BlockSpec((tm, d), lambda i: (i, 0)),
        out_shape=jax.ShapeDtypeStruct((L, d), F32),
        compiler_params=_cparams("parallel"),
        name="final",
    )(x1, m, mod, fg)


def _pad_heads(w, width):
    lead = w.shape[:-1]
    w = w.reshape(lead + (N_HEADS, width))
    w = jnp.pad(w, [(0, 0)] * len(lead) + [(0, 0), (0, HEAD_SLAB - width)])
    return w.reshape(lead + (N_HEADS * HEAD_SLAB,))


@functools.lru_cache(maxsize=None)
def _rope_slab_tables(L):
    half = QK_ROPE // 2
    pos = np.arange(L, dtype=np.float64)
    inv = ROPE_THETA ** (-np.arange(0, QK_ROPE, 2, dtype=np.float64) / QK_ROPE)
    ang = pos[:, None] * inv[None, :]
    cos, sin = np.cos(ang), np.sin(ang)
    z = lambda w: np.zeros((L, w))
    cos_t = np.concatenate([np.ones((L, QK_NOPE)), cos, cos, z(HEAD_SLAB - QK_NOPE - QK_ROPE)], axis=1)
    sin_a = np.concatenate([z(QK_NOPE), -sin, z(HEAD_SLAB - QK_NOPE - half)], axis=1)
    sin_b = np.concatenate([z(QK_NOPE + half), sin, z(HEAD_SLAB - QK_NOPE - QK_ROPE)], axis=1)
    return cos_t.astype(np.float32), sin_a.astype(np.float32), sin_b.astype(np.float32)


@functools.lru_cache(maxsize=None)
def _filter_features(L):
    t = np.linspace(0.0, 1.0, L)[None, :]
    t_r = np.arange(L, dtype=np.float64)[None, :]
    bands = np.linspace(1e-4, FILTER_BANDS - 1, FILTER_BANDS)[:, None]
    ang = 2.0 * math.pi * bands * t_r / L
    z = np.concatenate([t, np.cos(ang), -np.sin(ang)], axis=0)
    return np.pad(z, ((0, FILTER_HIDDEN - FILTER_EMB), (0, 0))).astype(np.float32)


@functools.lru_cache(maxsize=None)
def _dft_tables(L):
    n = 2 * L
    n2 = DFT_N2
    n1 = n // n2
    n1h = n1 // 2
    k1n = n1h + 1
    k1p = -(-k1n // SUBLANES) * SUBLANES
    two_pi = 2.0 * math.pi
    k1 = np.arange(k1p, dtype=np.int64)[:, None]
    live = (k1 < k1n).astype(np.float64)
    m1 = np.arange(n1h, dtype=np.int64)[None, :]
    ang1 = two_pi * ((k1 * m1) % n1) / n1
    f1m = np.concatenate([np.cos(ang1) * live, -np.sin(ang1) * live], axis=0).astype(BF16)
    m2 = np.arange(n2, dtype=np.int64)[None, :]
    ang2 = two_pi * ((k1 * m2) % n) / n
    tr = (np.cos(ang2) * live).astype(np.float32)
    ti = (-np.sin(ang2) * live).astype(np.float32)
    ang3 = two_pi * ((m2.T * m2) % n2) / n2
    f2r, f2i = np.cos(ang3), -np.sin(ang3)
    f2 = np.block([[f2r, f2i], [-f2i, f2r]]).astype(BF16)
    f2inv = np.block([[f2r, -f2i], [f2i, f2r]]).astype(BF16)
    o1 = np.arange(n1h, dtype=np.int64)[:, None]
    q1 = np.arange(n1h, dtype=np.int64)[None, :]
    ang4 = two_pi * ((o1 * q1) % n1) / n1
    wgt = np.where(q1 == 0, 1.0, 2.0)
    cmat = np.concatenate([wgt * np.cos(ang4), -wgt * np.sin(ang4)], axis=1).astype(BF16)
    sgn = np.broadcast_to(np.where(o1 % 2 == 0, 1.0, -1.0), (n1h, n2)).astype(np.float32)
    return (f1m, tr, ti, f2, f2inv, cmat, sgn), n1h


def _pick(n, pref):
    t = pref
    while n % t:
        t //= 2
    return t


def kernel(x, c, ada_w, ada_b, norm_mix_g, w_in, hy_conv_w, hy_conv_b, filt_w1, filt_b1, filt_freq1, filt_w2, filt_b2, filt_freq2, filt_w3, hy_bias, q_norm_g, kv_norm_g, w_uq, w_uk, w_uv, w_hy_out, w_mla_out, w_mix_out, norm_ffn_g, w_group, b_group, w_router, b_router, w_gate, w_up, w_down, final_norm_g):
    B, L, d = x.shape
    assert B == 1 and d == D_MODEL and ada_w.shape[0] == 1 and L % (2 * DFT_N2) == 0
    x2 = x.reshape(L, d)
    row = lambda v: v.reshape(1, -1)
    col = lambda v: v.reshape(-1, 1)

    mod = _ada(jnp.pad(c, ((0, 8 - B), (0, 0))), ada_w[0], row(ada_b[0]))

    wi = w_in[0]
    kr_slab = jnp.pad(wi[:, OFF_KR:OFF_GATE], ((0, 0), (QK_NOPE, HEAD_SLAB - QK_NOPE - QK_ROPE)))
    wcat = jnp.concatenate([wi[:, OFF_Q:OFF_KV], wi[:, OFF_KV:OFF_KR], wi[:, OFF_GATE:], kr_slab],
                           axis=1).astype(BF16)
    whyt = wi[:, 0:OFF_Q].T.astype(BF16)
    wuq = _pad_heads(w_uq[0], QK_NOPE + QK_ROPE).astype(BF16)
    wuk = _pad_heads(w_uk[0].reshape(KV_LORA, N_HEADS * QK_NOPE), QK_NOPE).astype(BF16)
    wuvt = _pad_heads(w_uv[0].reshape(KV_LORA, N_HEADS * V_HEAD), V_HEAD).T.astype(BF16)
    tkv = _pick(L, 1024)
    cos_t, sin_a, sin_b = _rope_slab_tables(L)

    hypt, gates, q, k, vt, kn2_tiles = _inproj(x2, mod, row(norm_mix_g[0]), wcat, whyt, row(q_norm_g[0]),
                                    row(kv_norm_g[0]), wuq, wuk, wuvt, cos_t, sin_a, sin_b,
                                    _pick(L, 512), tkv)

    zt = _filter_features(L)
    w1t = jnp.pad(filt_w1[0], ((0, FILTER_HIDDEN - FILTER_EMB), (0, 0))).T
    dl = np.abs(np.linspace(MIN_DECAY, MAX_DECAY, HY_WIDTH))[:, None].astype(np.float32)
    ht, nrm = _filt(zt, w1t, col(filt_b1[0]), col(filt_freq1[0]), filt_w2[0].T, col(filt_b2[0]),
                    col(filt_freq2[0]), filt_w3[0].T.astype(BF16), dl, _pick(L, 1024))
    tabs, n1h = _dft_tables(L)
    cw, cbias = hy_conv_w[0], hy_conv_b[0]
    w = HY_WIDTH
    taps = lambda s: [cw[0, s * w:(s + 1) * w], cw[1, s * w:(s + 1) * w], cw[2, s * w:(s + 1) * w],
                      cbias[s * w:(s + 1) * w]]
    prm = jnp.stack(taps(0) + taps(1) + taps(2) + [hy_bias[0], nrm[:, 0]]
                    + [jnp.zeros((w,), F32)] * (HY_PARAM_ROWS - 14), axis=1)
    prm = jnp.broadcast_to(prm[:, :, None], (w, HY_PARAM_ROWS, DFT_N2))
    yht = _hyena(hypt.reshape(3 * w, n1h, DFT_N2), ht.reshape(2 * w, n1h, DFT_N2), prm, tabs,
                 _pick(w, 16), 1.0 / (2 * L)).reshape(w, L)

    kn2 = jnp.broadcast_to(jnp.max(kn2_tiles, axis=0)[:, None, :], (N_HEADS, SUBLANES, LANES))
    ym = _attn(q, k, vt, kn2, _pick(L, 1024), 16)

    wmla = jnp.pad(w_mla_out[0].reshape(N_HEADS, V_HEAD, d),
                   ((0, 0), (0, HEAD_SLAB - V_HEAD), (0, 0))).reshape(QK_SLABS, d).astype(BF16)
    wrt = jnp.pad(jnp.concatenate([w_router[0], w_group[0]], axis=1),
                  ((0, 0), (0, LANES - N_EXPERTS - N_GROUPS)))
    brt = jnp.pad(jnp.concatenate([b_router[0], b_group[0]]), (0, LANES - N_EXPERTS - N_GROUPS))[None, :]
    x1, h2, comb, gsel, gcnt = _merge(x2, yht, ym, gates, mod, w_hy_out[0].astype(BF16), wmla,
                                      w_mix_out[0].astype(BF16), row(norm_ffn_g[0]), wrt, brt,
                                      _pick(L, 1024))

    assert L % MOE_TILE == 0
    bg, nbu, roff = _moe_plan(gcnt, L // MOE_TILE)
    wgu = jnp.concatenate([w_gate[0], w_up[0]], axis=2).astype(BF16)
    wd = w_down[0].reshape(N_EXPERTS * D_EXPERT, d).astype(BF16)
    moe = _moe(bg, nbu, roff, h2, comb, gsel, wgu, wd)
    out = _final(x1, moe, mod, row(final_norm_g), _pick(L, 512))
    return out.reshape(B, L, d)
```

```python
import functools
import math

import numpy as np
import jax
import jax.numpy as jnp
from jax import lax
from jax.experimental import pallas as pl
from jax.experimental.pallas import tpu as pltpu

F32 = jnp.float32
BF16 = jnp.bfloat16
HIGHEST = lax.Precision.HIGHEST

D_MODEL = 1024
EPS = 1e-6
HY_WIDTH = 512
FILTER_BANDS = 16
FILTER_EMB = 1 + 2 * FILTER_BANDS
FILTER_HIDDEN = 64
FILTER_DECAY_TARGET = 1e-2
FAST_DECAY_PCT = 0.3
SLOW_DECAY_PCT = 1.5
FILTER_SHIFT = 0.05
MIN_DECAY = math.log(FILTER_DECAY_TARGET) / SLOW_DECAY_PCT
MAX_DECAY = math.log(FILTER_DECAY_TARGET) / FAST_DECAY_PCT
N_HEADS = 8
QK_NOPE = 64
QK_ROPE = 32
V_HEAD = 64
Q_LORA = 384
KV_LORA = 256
ROPE_THETA = 10000.0
MLA_WIDTH = N_HEADS * V_HEAD
OFF_Q = 3 * HY_WIDTH
OFF_KV = OFF_Q + Q_LORA
OFF_KR = OFF_KV + KV_LORA
OFF_GATE = OFF_KR + QK_ROPE
N_GROUPS = 4
EXPERTS_PER_GROUP = 8
N_EXPERTS = N_GROUPS * EXPERTS_PER_GROUP
D_EXPERT = 256

LANES = 128
SUBLANES = 8
HEAD_SLAB = LANES
QK_SLABS = N_HEADS * HEAD_SLAB
ONES_LANE = V_HEAD
DFT_N2 = LANES
HY_PARAM_ROWS = 16
NEG_BIG = -1e30
VMEM_LIMIT = 56 * 1024 * 1024
MAX_STATIC_SHIFT = 60.0
SHIFT_SLACK = 1.01
SHIFT_EPS = 1e-3

C_Q = 0
C_KV = C_Q + Q_LORA
C_GATE = C_KV + KV_LORA
C_KR = C_GATE + 2 * D_MODEL
C_END = C_KR + HEAD_SLAB


def _cparams(*sem):
    return pltpu.CompilerParams(dimension_semantics=sem, vmem_limit_bytes=VMEM_LIMIT)


def _rms(x):
    return x * lax.rsqrt(jnp.mean(x * x, axis=-1, keepdims=True) + EPS)


def _dot_nt(a, b):
    return lax.dot_general(a, b, (((1,), (1,)), ((), ())), preferred_element_type=F32)


def _dot_tn(a, b):
    return lax.dot_general(a, b, (((0,), (0,)), ((), ())), preferred_element_type=F32)


def _ada_kernel(c_ref, w_ref, b_ref, o_ref):
    c = c_ref[...]
    ca = c * jax.nn.sigmoid(c)
    o_ref[...] = jnp.dot(ca, w_ref[...], preferred_element_type=F32, precision=HIGHEST) + b_ref[...]


def _ada(c8, w, b):
    d, n = w.shape
    tn = 1024
    return pl.pallas_call(
        _ada_kernel,
        grid=(n // tn,),
        in_specs=[pl.BlockSpec((8, d), lambda j: (0, 0)),
                  pl.BlockSpec((d, tn), lambda j: (0, j)),
                  pl.BlockSpec((1, tn), lambda j: (0, j))],
        out_specs=pl.BlockSpec((8, tn), lambda j: (0, j)),
        out_shape=jax.ShapeDtypeStruct((8, n), F32),
        compiler_params=_cparams("parallel"),
        name="ada",
    )(c8, w, b)


def _rope_slab(z, cos_t, sin_a, sin_b):
    return z * cos_t + pltpu.roll(z, HEAD_SLAB - QK_ROPE // 2, 1) * sin_a + pltpu.roll(z, QK_ROPE // 2, 1) * sin_b


def _inproj_kernel(x_ref, mod_ref, g_ref, wcat_ref, whyt_ref, gq_ref, gkv_ref, wuq_ref, wuk_ref, wuvt_ref,
                   cos_ref, sina_ref, sinb_ref,
                   hypt_ref, gate_ref, q_ref, k_ref, vt_ref, kn2_ref, *, qscale):
    d = D_MODEL
    x = x_ref[...]
    sh1 = mod_ref[0:1, 0:d]
    sc1 = mod_ref[0:1, d:2 * d]
    hb = (_rms(x) * g_ref[...] * (1.0 + sc1) + sh1).astype(BF16)
    hypt_ref[...] = _dot_nt(whyt_ref[...], hb).astype(BF16)
    proj = jnp.dot(hb, wcat_ref[...], preferred_element_type=F32)
    gate_ref[...] = jax.nn.sigmoid(proj[:, C_GATE:C_KR]).astype(BF16)
    cq = _rms(proj[:, C_Q:C_KV]) * gq_ref[...]
    ckv = (_rms(proj[:, C_KV:C_GATE]) * gkv_ref[...]).astype(BF16)
    q = jnp.dot(cq.astype(BF16), wuq_ref[...], preferred_element_type=F32)
    kn = jnp.dot(ckv, wuk_ref[...], preferred_element_type=F32)
    vt = _dot_nt(wuvt_ref[...], ckv)
    cos_t = cos_ref[...]
    sin_a = sina_ref[...]
    sin_b = sinb_ref[...]
    kr = _rope_slab(proj[:, C_KR:C_END], cos_t, sin_a, sin_b)
    srow = lax.broadcasted_iota(jnp.int32, (HEAD_SLAB, 1), 0)
    ones_row = jnp.where(srow == ONES_LANE, 1.0, 0.0).astype(F32)
    kn2 = []
    for hd in range(N_HEADS):
        sl = slice(hd * HEAD_SLAB, (hd + 1) * HEAD_SLAB)
        q_ref[:, sl] = (_rope_slab(q[:, sl], cos_t, sin_a, sin_b) * qscale).astype(BF16)
        kb = (kn[:, sl] + kr).astype(BF16)
        k_ref[:, sl] = kb
        vt_ref[hd, 0] = (vt[sl] + ones_row).astype(BF16)
        kf = kb.astype(F32)
        big = jnp.max(jnp.sum(kf * kf, axis=1, keepdims=True), axis=0, keepdims=True)
        kn2.append(jnp.broadcast_to(big, (1, HEAD_SLAB)))
    kn2_ref[0] = jnp.concatenate(kn2, axis=0)


def _inproj(x2, mod, g1, wcat, whyt, gq, gkv, wuq, wuk, wuvt, cos_t, sin_a, sin_b, tm, tkv):
    L, d = x2.shape
    qscale = (QK_NOPE + QK_ROPE) ** -0.5 * math.log2(math.e)
    full = lambda a: pl.BlockSpec(a.shape, lambda i: (0,) * a.ndim)
    row = lambda w: pl.BlockSpec((tm, w), lambda i: (i, 0))
    per = tkv // tm
    return pl.pallas_call(
        functools.partial(_inproj_kernel, qscale=qscale),
        grid=(L // tm,),
        in_specs=[row(d), full(mod), full(g1), full(wcat), full(whyt), full(gq), full(gkv), full(wuq),
                  full(wuk), full(wuvt), row(HEAD_SLAB), row(HEAD_SLAB), row(HEAD_SLAB)],
        out_specs=[pl.BlockSpec((3 * HY_WIDTH, tm), lambda i: (0, i)),
                   row(2 * D_MODEL), row(QK_SLABS), row(QK_SLABS),
                   pl.BlockSpec((N_HEADS, 1, HEAD_SLAB, tm), lambda i: (0, i // per, 0, i % per)),
                   pl.BlockSpec((1, N_HEADS, HEAD_SLAB), lambda i: (i, 0, 0))],
        out_shape=[jax.ShapeDtypeStruct((3 * HY_WIDTH, L), BF16),
                   jax.ShapeDtypeStruct((L, 2 * D_MODEL), BF16),
                   jax.ShapeDtypeStruct((L, QK_SLABS), BF16),
                   jax.ShapeDtypeStruct((L, QK_SLABS), BF16),
                   jax.ShapeDtypeStruct((N_HEADS, L // tkv, HEAD_SLAB, tkv), BF16),
                   jax.ShapeDtypeStruct((L // tm, N_HEADS, HEAD_SLAB), F32)],
        compiler_params=_cparams("parallel"),
        name="inproj",
    )(x2, mod, g1, wcat, whyt, gq, gkv, wuq, wuk, wuvt, cos_t, sin_a, sin_b)


def _filt_kernel(z_ref, w1_ref, b1_ref, f1_ref, w2_ref, b2_ref, f2_ref, w3_ref, dl_ref,
                 h_ref, nrm_ref):
    i = pl.program_id(0)
    z = z_ref[...]
    tm = z.shape[1]
    h = jnp.sin(f1_ref[...] * (jnp.dot(w1_ref[...], z, preferred_element_type=F32, precision=HIGHEST)
                               + b1_ref[...]))
    h = jnp.sin(f2_ref[...] * (jnp.dot(w2_ref[...], h, preferred_element_type=F32, precision=HIGHEST)
                               + b2_ref[...]))
    h3 = jnp.dot(w3_ref[...], h.astype(BF16), preferred_element_type=F32)
    t = z[0:1, :]
    window = jnp.exp(-dl_ref[...] * t) + FILTER_SHIFT
    hf = h3[0:HY_WIDTH] * window
    hb = h3[HY_WIDTH:2 * HY_WIDTH] * window
    col = lax.broadcasted_iota(jnp.int32, (1, tm), 1)
    hb = jnp.where((col == 0) & (i == 0), 0.0, hb)
    h_ref[0:HY_WIDTH, :] = hf.astype(BF16)
    h_ref[HY_WIDTH:2 * HY_WIDTH, :] = hb.astype(BF16)
    part = jnp.sum(jnp.abs(hf) + jnp.abs(hb), axis=1, keepdims=True)

    @pl.when(i == 0)
    def _():
        nrm_ref[...] = jnp.zeros_like(nrm_ref)

    nrm_ref[...] += part


def _filt(zt, w1t, b1, f1, w2t, b2, f2, w3t, dl, tm):
    L = zt.shape[1]
    full = lambda a: pl.BlockSpec(a.shape, lambda i: (0,) * a.ndim)
    return pl.pallas_call(
        _filt_kernel,
        grid=(L // tm,),
        in_specs=[pl.BlockSpec((zt.shape[0], tm), lambda i: (0, i)), full(w1t), full(b1), full(f1),
                  full(w2t), full(b2), full(f2), full(w3t), full(dl)],
        out_specs=[pl.BlockSpec((2 * HY_WIDTH, tm), lambda i: (0, i)),
                   pl.BlockSpec((HY_WIDTH, 1), lambda i: (0, 0))],
        out_shape=[jax.ShapeDtypeStruct((2 * HY_WIDTH, L), BF16),
                   jax.ShapeDtypeStruct((HY_WIDTH, 1), F32)],
        compiler_params=_cparams("arbitrary"),
        name="filt",
    )(zt, w1t, b1, f1, w2t, b2, f2, w3t, dl)


def _hyena_kernel(x0_ref, x1_ref, v_ref, hf_ref, hb_ref, prm_ref, f1_ref, tr_ref, ti_ref, f2_ref,
                  f2i_ref, cm_ref, sg_ref, o_ref, *, inv_n):
    cb, n1h, w = x0_ref.shape
    k1p = tr_ref.shape[0]
    lane = lax.broadcasted_iota(jnp.int32, (n1h, w), 1)
    rowi = lax.broadcasted_iota(jnp.int32, (n1h, w), 0)
    tr = tr_ref[...]
    ti = ti_ref[...]

    def conv(x, p, j):
        r = pltpu.roll(x, 1, 1)
        up = jnp.where(lane == 0, jnp.where(rowi == 0, 0.0, pltpu.roll(r, 1, 0)), r)
        l = pltpu.roll(x, w - 1, 1)
        dn = jnp.where(lane == w - 1, jnp.where(rowi == n1h - 1, 0.0, pltpu.roll(l, n1h - 1, 0)), l)
        return up * p[j:j + 1] + x * p[j + 1:j + 2] + dn * p[j + 2:j + 3] + p[j + 3:j + 4]

    def twiddle(a, s):
        ar = a[0:k1p, s * w:(s + 1) * w]
        ai = a[k1p:2 * k1p, s * w:(s + 1) * w]
        return jnp.concatenate([ar * tr - ai * ti, ar * ti + ai * tr], axis=1).astype(BF16)

    prm = [prm_ref[c] for c in range(cb)]
    x0s, us, sigs = [], [], []
    for c in range(cb):
        p = prm[c]
        x0s.append(conv(x0_ref[c].astype(F32), p, 0))
        u = conv(v_ref[c].astype(F32), p, 8) * conv(x1_ref[c].astype(F32), p, 4)
        us.append(u)
        sigs += [u.astype(BF16), hf_ref[c], hb_ref[c]]
    a = jnp.dot(f1_ref[...], jnp.concatenate(sigs, axis=1), preferred_element_type=F32)
    b = jnp.concatenate([twiddle(a, s) for s in range(3 * cb)], axis=0)
    s = jnp.dot(b, f2_ref[...], preferred_element_type=F32)
    ys = []
    for c in range(cb):
        su = s[(3 * c) * k1p:(3 * c + 1) * k1p]
        sf = s[(3 * c + 1) * k1p:(3 * c + 2) * k1p]
        sb = s[(3 * c + 2) * k1p:(3 * c + 3) * k1p]
        inv = 1.0 / prm[c][13:14]
        gr = (sf[:, 0:w] + sb[:, 0:w]) * inv
        gi = (sf[:, w:2 * w] - sb[:, w:2 * w]) * inv
        ur, ui = su[:, 0:w], su[:, w:2 * w]
        ys.append(jnp.concatenate([ur * gr - ui * gi, ur * gi + ui * gr], axis=1).astype(BF16))
    zt = jnp.dot(jnp.concatenate(ys, axis=0), f2i_ref[...], preferred_element_type=F32)
    zss, nyq = [], []
    for c in range(cb):
        ztr = zt[c * k1p:(c + 1) * k1p, 0:w]
        zti = zt[c * k1p:(c + 1) * k1p, w:2 * w]
        zr = ztr * tr + zti * ti
        zi = zti * tr - ztr * ti
        zss.append(jnp.concatenate([zr[0:n1h], zi[0:n1h]], axis=0).astype(BF16))
        nyq.append(zr[n1h:n1h + 1])
    yy = jnp.dot(cm_ref[...], jnp.concatenate(zss, axis=1), preferred_element_type=F32)
    for c in range(cb):
        conv_out = (yy[:, c * w:(c + 1) * w] + sg_ref[...] * nyq[c]) * inv_n
        o_ref[c] = (x0s[c] * (conv_out + prm[c][12:13] * us[c])).astype(BF16)


def _hyena(hyp3, h3, prm, tabs, cb, inv_n):
    c3, n1h, w = hyp3.shape
    c = c3 // 3
    nb = c // cb
    f1m, tr, ti, f2, f2i, cmat, sgn = tabs
    full = lambda a: pl.BlockSpec(a.shape, lambda i: (0,) * a.ndim)
    blk = lambda off: pl.BlockSpec((cb, n1h, w), lambda i: (off * nb + i, 0, 0))
    return pl.pallas_call(
        functools.partial(_hyena_kernel, inv_n=inv_n),
        grid=(nb,),
        in_specs=[blk(0), blk(1), blk(2), blk(0), blk(1),
                  pl.BlockSpec((cb, HY_PARAM_ROWS, w), lambda i: (i, 0, 0)),
                  full(f1m), full(tr), full(ti), full(f2), full(f2i), full(cmat), full(sgn)],
        out_specs=pl.BlockSpec((cb, n1h, w), lambda i: (i, 0, 0)),
        out_shape=jax.ShapeDtypeStruct((c, n1h, w), BF16),
        compiler_params=_cparams("parallel"),
        name="hyena",
    )(hyp3, hyp3, hyp3, h3, h3, prm, f1m, tr, ti, f2, f2i, cmat, sgn)


def _attn_kernel(q_ref, k_ref, vt_ref, kn2_ref, o_ref, *, unroll):
    tq = q_ref.shape[0]
    nk, _, tk = vt_ref.shape[1:]
    q = q_ref[...]

    def finish(acc):
        o_ref[...] = (acc / acc[ONES_LANE:ONES_LANE + 1, :]).astype(BF16)

    qf = q.astype(F32)
    qn2 = _dot_nt(jnp.ones((SUBLANES, HEAD_SLAB), BF16), (qf * qf).astype(BF16))[0:1]
    bound = jnp.sqrt(qn2 * kn2_ref[0][0:1, 0:1]) * SHIFT_SLACK + SHIFT_EPS
    static_shift_ok = jnp.max(bound) <= MAX_STATIC_SHIFT

    @pl.when(static_shift_ok)
    def _():
        def body(j, acc):
            off = pl.multiple_of(j * tk, tk)
            st = _dot_nt(k_ref[pl.ds(off, tk), :], q)
            p = jnp.exp2(st - bound).astype(BF16)
            return acc + jnp.dot(vt_ref[0, j], p, preferred_element_type=F32)

        finish(lax.fori_loop(0, nk, body, jnp.zeros((HEAD_SLAB, tq), F32), unroll=unroll))

    @pl.when(jnp.logical_not(static_shift_ok))
    def _():
        def body(j, carry):
            m, acc = carry
            off = pl.multiple_of(j * tk, tk)
            st = _dot_nt(k_ref[pl.ds(off, tk), :], q)
            m_new = jnp.maximum(m, jnp.max(st, axis=0, keepdims=True))
            alpha = jnp.exp2(m - m_new)
            p = jnp.exp2(st - m_new)
            acc = alpha * acc + jnp.dot(vt_ref[0, j], p.astype(BF16), preferred_element_type=F32)
            return m_new, acc

        m0 = jnp.full((1, tq), NEG_BIG, F32)
        acc0 = jnp.zeros((HEAD_SLAB, tq), F32)
        finish(lax.fori_loop(0, nk, body, (m0, acc0), unroll=unroll)[1])


def _attn(q, k, vt, kn2, tq, unroll):
    L = q.shape[0]
    _, nk, _, tk = vt.shape
    return pl.pallas_call(
        functools.partial(_attn_kernel, unroll=unroll),
        grid=(N_HEADS, L // tq),
        in_specs=[pl.BlockSpec((tq, HEAD_SLAB), lambda h, i: (i, h)),
                  pl.BlockSpec((L, HEAD_SLAB), lambda h, i: (0, h)),
                  pl.BlockSpec((1, nk, HEAD_SLAB, tk), lambda h, i: (h, 0, 0, 0)),
                  pl.BlockSpec((1, SUBLANES, LANES), lambda h, i: (h, 0, 0))],
        out_specs=pl.BlockSpec((HEAD_SLAB, tq), lambda h, i: (h, i)),
        out_shape=jax.ShapeDtypeStruct((QK_SLABS, L), BF16),
        compiler_params=_cparams("parallel", "parallel"),
        name="attn",
    )(q, k, vt, kn2)


def _merge_kernel(x_ref, yht_ref, ymt_ref, gate_ref, mod_ref, why_ref, wmla_ref, wmix_ref, g2_ref,
                  wrt_ref, brt_ref, x1_ref, h2_ref, comb_ref, gsel_ref, gcnt_ref):
    d = D_MODEL
    a = _dot_tn(yht_ref[...], why_ref[...])
    ymt = jnp.concatenate([ymt_ref[hd * HEAD_SLAB:hd * HEAD_SLAB + V_HEAD, :] for hd in range(N_HEADS)],
                          axis=0)
    b = _dot_tn(ymt, wmla_ref[...])
    mix = gate_ref[:, 0:d].astype(F32) * a + gate_ref[:, d:2 * d].astype(F32) * b
    o = jnp.dot(mix.astype(BF16), wmix_ref[...], preferred_element_type=F32)
    gt1 = mod_ref[0:1, 2 * d:3 * d]
    sh2 = mod_ref[0:1, 3 * d:4 * d]
    sc2 = mod_ref[0:1, 4 * d:5 * d]
    x1 = x_ref[...] + gt1 * o
    x1_ref[...] = x1
    h2 = _rms(x1) * g2_ref[...] * (1.0 + sc2) + sh2
    h2_hi = h2.astype(BF16)
    h2_ref[...] = h2_hi
    h2_lo = (h2 - h2_hi.astype(F32)).astype(BF16)
    wrt = wrt_ref[...]
    w_hi = wrt.astype(BF16)
    w_lo = (wrt - w_hi.astype(F32)).astype(BF16)
    e = (jnp.dot(h2_hi, w_hi, preferred_element_type=F32)
         + (jnp.dot(h2_lo, w_hi, preferred_element_type=F32)
            + jnp.dot(h2_hi, w_lo, preferred_element_type=F32))) + brt_ref[...]
    lane = lax.broadcasted_iota(jnp.int32, e.shape, 1).astype(F32)
    gvalid = (lane >= N_EXPERTS) & (lane < N_EXPERTS + N_GROUPS)
    glm = jnp.where(gvalid, e, NEG_BIG)
    gmax = jnp.max(glm, axis=1, keepdims=True)
    gidx = jnp.min(jnp.where(glm == gmax, lane, float(LANES)), axis=1, keepdims=True) - float(N_EXPERTS)
    psum = jnp.sum(jnp.where(gvalid, jnp.exp(glm - gmax), 0.0), axis=1, keepdims=True)
    p_sel = 1.0 / psum
    lo = gidx * EXPERTS_PER_GROUP
    em = jnp.where((lane >= lo) & (lane < lo + EXPERTS_PER_GROUP), e, NEG_BIG)
    v1 = jnp.max(em, axis=1, keepdims=True)
    i1 = jnp.min(jnp.where(em == v1, lane, float(LANES)), axis=1, keepdims=True)
    em2 = jnp.where(lane == i1, NEG_BIG, em)
    v2 = jnp.max(em2, axis=1, keepdims=True)
    i2 = jnp.min(jnp.where(em2 == v2, lane, float(LANES)), axis=1, keepdims=True)
    t = jnp.exp(v2 - v1)
    w1 = p_sel / (1.0 + t)
    w2 = p_sel * t / (1.0 + t)
    comb_ref[...] = jnp.where(lane == i1, w1, 0.0) + jnp.where(lane == i2, w2, 0.0)
    gsel = jnp.where(lane == gidx, 1.0, 0.0)
    gsel_ref[...] = gsel.astype(BF16)
    srow = lax.broadcasted_iota(jnp.int32, (SUBLANES, LANES), 0)
    gcnt_ref[0] = jnp.where(srow == 0, jnp.sum(gsel, axis=0, keepdims=True), 0.0)


def _merge(x2, yht, ym, gates, mod, why, wmla, wmix, g2, wrt, brt, tm):
    L, d = x2.shape
    full = lambda a: pl.BlockSpec(a.shape, lambda i: (0,) * a.ndim)
    row = lambda w: pl.BlockSpec((tm, w), lambda i: (i, 0))
    return pl.pallas_call(
        _merge_kernel,
        grid=(L // tm,),
        in_specs=[row(d), pl.BlockSpec((HY_WIDTH, tm), lambda i: (0, i)),
                  pl.BlockSpec((QK_SLABS, tm), lambda i: (0, i)), row(2 * d),
                  full(mod), full(why), full(wmla), full(wmix), full(g2), full(wrt), full(brt)],
        out_specs=[row(d), row(d), row(LANES), row(LANES),
                   pl.BlockSpec((1, SUBLANES, LANES), lambda i: (i, 0, 0))],
        out_shape=[jax.ShapeDtypeStruct((L, d), F32),
                   jax.ShapeDtypeStruct((L, d), BF16),
                   jax.ShapeDtypeStruct((L, LANES), F32),
                   jax.ShapeDtypeStruct((L, LANES), BF16),
                   jax.ShapeDtypeStruct((L // tm, SUBLANES, LANES), F32)],
        compiler_params=_cparams("parallel"),
        name="merge",
    )(x2, yht, ym, gates, mod, why, wmla, wmix, g2, wrt, brt)


MOE_TILE = 2048
MOE_ROWS = 256
MOE_BLOCKS = MOE_TILE // MOE_ROWS + N_GROUPS
POS_RADIX = 64


def _moe_kernel(bg_ref, nbu_ref, roff_ref, h_ref, comb_ref, gsel_ref, wgu_ref, wd_ref, o_ref,
                pos_ref, cs_ref, acc_ref):
    i = pl.program_id(0)
    b = pl.program_id(1)
    t = h_ref.shape[0]
    lane = lax.broadcasted_iota(jnp.int32, (1, LANES), 1)

    @pl.when(b == 0)
    def _():
        oh = gsel_ref[...]
        sub = MOE_ROWS
        tri = jnp.where(lax.broadcasted_iota(jnp.int32, (sub, sub), 1)
                        < lax.broadcasted_iota(jnp.int32, (sub, sub), 0), 1.0, 0.0).astype(BF16)
        carry = jnp.zeros((1, LANES), F32)
        cums = []
        for s in range(t // sub):
            blk = oh[s * sub:(s + 1) * sub]
            cums.append(jnp.dot(tri, blk, preferred_element_type=F32) + carry)
            carry = carry + jnp.sum(blk.astype(F32), axis=0, keepdims=True)
        cum = jnp.concatenate(cums, axis=0)
        offs = jnp.zeros((1, LANES), F32)
        for g in range(N_GROUPS):
            offs = jnp.where(lane == g, roff_ref[i * N_GROUPS + g].astype(F32), offs)
        pos = jnp.sum(jnp.where(oh > 0, cum + offs, 0.0), axis=1, keepdims=True)
        hi = jnp.floor(pos * (1.0 / POS_RADIX))
        lo = pos - hi * POS_RADIX
        digits = jnp.where(lane == 0, hi, jnp.where(lane == 1, lo, 0.0)).astype(BF16)
        srow = lax.broadcasted_iota(jnp.int32, (SUBLANES, LANES), 0)
        scol = lax.broadcasted_iota(jnp.int32, (SUBLANES, LANES), 1)
        pick = jnp.where(srow == scol, 1.0, 0.0).astype(BF16)
        rows = _dot_nt(pick, digits)
        pos_ref[...] = jnp.broadcast_to(rows[0:1] * POS_RADIX + rows[1:2], pos_ref.shape)
        comb = comb_ref[...]
        comb_hi = comb.astype(BF16)
        cs_ref[:, 0:LANES] = comb_hi
        cs_ref[:, LANES:2 * LANES] = (comb - comb_hi.astype(F32)).astype(BF16)
        acc_ref[...] = jnp.zeros_like(acc_ref)

    @pl.when(b < nbu_ref[i])
    def _():
        g = bg_ref[i * MOE_BLOCKS + b]
        rid = (lax.broadcasted_iota(jnp.int32, (MOE_ROWS, 1), 0) + b * MOE_ROWS).astype(F32)
        sel = jnp.where(rid == pos_ref[0:1, :], 1.0, 0.0).astype(BF16)
        xs = jnp.dot(sel, h_ref[...], preferred_element_type=F32).astype(BF16)
        cs = jnp.dot(sel, cs_ref[...], preferred_element_type=F32)
        comb_s = cs[:, 0:LANES] + cs[:, LANES:2 * LANES]
        parts = []
        for j in range(EXPERTS_PER_GROUP):
            gu = jnp.dot(xs, wgu_ref[j], preferred_element_type=F32)
            gt = gu[:, 0:D_EXPERT]
            a = gt * jax.nn.sigmoid(gt) * gu[:, D_EXPERT:2 * D_EXPERT]
            col = jnp.sum(jnp.where(lane == g * EXPERTS_PER_GROUP + j, comb_s, 0.0), axis=1, keepdims=True)
            parts.append((a * col).astype(BF16))
        y = jnp.dot(jnp.concatenate(parts, axis=1), wd_ref[...], preferred_element_type=F32)
        acc_ref[...] += _dot_tn(sel, y.astype(BF16))

    @pl.when(b == pl.num_programs(1) - 1)
    def _():
        o_ref[...] = acc_ref[...].astype(BF16)


def _moe(bg, nbu, roff, h2, comb, gsel, wgu, wd):
    L, d = h2.shape
    t = MOE_TILE
    once = dict(pipeline_mode=pl.Buffered(1))
    wsel = lambda i, b, bg_r, nbu_r, roff_r: (bg_r[i * MOE_BLOCKS + b], 0, 0)
    grid_spec = pltpu.PrefetchScalarGridSpec(
        num_scalar_prefetch=3,
        grid=(L // t, MOE_BLOCKS),
        in_specs=[pl.BlockSpec((t, d), lambda i, b, *_: (i, 0), **once),
                  pl.BlockSpec((t, LANES), lambda i, b, *_: (i, 0), **once),
                  pl.BlockSpec((t, LANES), lambda i, b, *_: (i, 0), **once),
                  pl.BlockSpec((EXPERTS_PER_GROUP, d, 2 * D_EXPERT), wsel),
                  pl.BlockSpec((EXPERTS_PER_GROUP * D_EXPERT, d), lambda i, b, bg_r, *_: (bg_r[i * MOE_BLOCKS + b], 0))],
        out_specs=pl.BlockSpec((t, d), lambda i, b, *_: (i, 0)),
        scratch_shapes=[pltpu.VMEM((SUBLANES, t), F32),
                        pltpu.VMEM((t, 2 * LANES), BF16),
                        pltpu.VMEM((t, d), F32)])
    return pl.pallas_call(
        _moe_kernel,
        grid_spec=grid_spec,
        out_shape=jax.ShapeDtypeStruct((L, d), BF16),
        compiler_params=_cparams("parallel", "arbitrary"),
        name="moe",
    )(bg, nbu, roff, h2, comb, gsel, wgu, wd)


def _moe_plan(gcnt, n_tiles):
    cnt = gcnt[:, 0, 0:N_GROUPS].reshape(n_tiles, -1, N_GROUPS).sum(axis=1).astype(jnp.int32)
    nblk = (cnt + MOE_ROWS - 1) // MOE_ROWS
    bend = jnp.cumsum(nblk, axis=1)
    roff = (bend - nblk) * MOE_ROWS
    nbu = bend[:, -1]
    bidx = jnp.arange(MOE_BLOCKS, dtype=jnp.int32)[None, :]
    grp = jnp.sum((bidx[:, :, None] >= bend[:, None, :]).astype(jnp.int32), axis=2)
    last = jnp.take_along_axis(grp, jnp.maximum(nbu - 1, 0)[:, None], axis=1)
    bg = jnp.where(bidx < nbu[:, None], grp, last)
    bg = jnp.minimum(bg, N_GROUPS - 1)
    return bg.reshape(-1), nbu, roff.reshape(-1)


def _final_kernel(x1_ref, m_ref, mod_ref, fg_ref, o_ref):
    d = D_MODEL
    x2 = x1_ref[...] + mod_ref[0:1, 5 * d:6 * d] * m_ref[...].astype(F32)
    o_ref[...] = _rms(x2) * fg_ref[...]


def _final(x1, m, mod, fg, tm):
    L, d = x1.shape
    return pl.pallas_call(
        _final_kernel,
        grid=(L // tm,),
        in_specs=[pl.BlockSpec((tm, d), lambda i: (i, 0)), pl.BlockSpec((tm, d), lambda i: (i, 0)),
                  pl.BlockSpec(mod.shape, lambda i: (0, 0)), pl.BlockSpec(fg.shape, lambda i: (0, 0))],
        out_specs=pl.BlockSpec((tm, d), lambda i: (i, 0)),
        out_shape=jax.ShapeDtypeStruct((L, d), F32),
        compiler_params=_cparams("parallel"),
        name="final",
    )(x1, m, mod, fg)


def _pad_heads(w, width):
    lead = w.shape[:-1]
    w = w.reshape(lead + (N_HEADS, width))
    w = jnp.pad(w, [(0, 0)] * len(lead) + [(0, 0), (0, HEAD_SLAB - width)])
    return w.reshape(lead + (N_HEADS * HEAD_SLAB,))


@functools.lru_cache(maxsize=None)
def _rope_slab_tables(L):
    half = QK_ROPE // 2
    pos = np.arange(L, dtype=np.float64)
    inv = ROPE_THETA ** (-np.arange(0, QK_ROPE, 2, dtype=np.float64) / QK_ROPE)
    ang = pos[:, None] * inv[None, :]
    cos, sin = np.cos(ang), np.sin(ang)
    z = lambda w: np.zeros((L, w))
    cos_t = np.concatenate([np.ones((L, QK_NOPE)), cos, cos, z(HEAD_SLAB - QK_NOPE - QK_ROPE)], axis=1)
    sin_a = np.concatenate([z(QK_NOPE), -sin, z(HEAD_SLAB - QK_NOPE - half)], axis=1)
    sin_b = np.concatenate([z(QK_NOPE + half), sin, z(HEAD_SLAB - QK_NOPE - QK_ROPE)], axis=1)
    return cos_t.astype(np.float32), sin_a.astype(np.float32), sin_b.astype(np.float32)


@functools.lru_cache(maxsize=None)
def _filter_features(L):
    t = np.linspace(0.0, 1.0, L)[None, :]
    t_r = np.arange(L, dtype=np.float64)[None, :]
    bands = np.linspace(1e-4, FILTER_BANDS - 1, FILTER_BANDS)[:, None]
    ang = 2.0 * math.pi * bands * t_r / L
    z = np.concatenate([t, np.cos(ang), -np.sin(ang)], axis=0)
    return np.pad(z, ((0, FILTER_HIDDEN - FILTER_EMB), (0, 0))).astype(np.float32)


@functools.lru_cache(maxsize=None)
def _dft_tables(L):
    n = 2 * L
    n2 = DFT_N2
    n1 = n // n2
    n1h = n1 // 2
    k1n = n1h + 1
    k1p = -(-k1n // SUBLANES) * SUBLANES
    two_pi = 2.0 * math.pi
    k1 = np.arange(k1p, dtype=np.int64)[:, None]
    live = (k1 < k1n).astype(np.float64)
    m1 = np.arange(n1h, dtype=np.int64)[None, :]
    ang1 = two_pi * ((k1 * m1) % n1) / n1
    f1m = np.concatenate([np.cos(ang1) * live, -np.sin(ang1) * live], axis=0).astype(BF16)
    m2 = np.arange(n2, dtype=np.int64)[None, :]
    ang2 = two_pi * ((k1 * m2) % n) / n
    tr = (np.cos(ang2) * live).astype(np.float32)
    ti = (-np.sin(ang2) * live).astype(np.float32)
    ang3 = two_pi * ((m2.T * m2) % n2) / n2
    f2r, f2i = np.cos(ang3), -np.sin(ang3)
    f2 = np.block([[f2r, f2i], [-f2i, f2r]]).astype(BF16)
    f2inv = np.block([[f2r, -f2i], [f2i, f2r]]).astype(BF16)
    o1 = np.arange(n1h, dtype=np.int64)[:, None]
    q1 = np.arange(n1h, dtype=np.int64)[None, :]
    ang4 = two_pi * ((o1 * q1) % n1) / n1
    wgt = np.where(q1 == 0, 1.0, 2.0)
    cmat = np.concatenate([wgt * np.cos(ang4), -wgt * np.sin(ang4)], axis=1).astype(BF16)
    sgn = np.broadcast_to(np.where(o1 % 2 == 0, 1.0, -1.0), (n1h, n2)).astype(np.float32)
    return (f1m, tr, ti, f2, f2inv, cmat, sgn), n1h


def _pick(n, pref):
    t = pref
    while n % t:
        t //= 2
    return t


def kernel(x, c, ada_w, ada_b, norm_mix_g, w_in, hy_conv_w, hy_conv_b, filt_w1, filt_b1, filt_freq1, filt_w2, filt_b2, filt_freq2, filt_w3, hy_bias, q_norm_g, kv_norm_g, w_uq, w_uk, w_uv, w_hy_out, w_mla_out, w_mix_out, norm_ffn_g, w_group, b_group, w_router, b_router, w_gate, w_up, w_down, final_norm_g):
    B, L, d = x.shape
    assert B == 1 and d == D_MODEL and ada_w.shape[0] == 1 and L % (2 * DFT_N2) == 0
    x2 = x.reshape(L, d)
    row = lambda v: v.reshape(1, -1)
    col = lambda v: v.reshape(-1, 1)

    mod = _ada(jnp.pad(c, ((0, 8 - B), (0, 0))), ada_w[0], row(ada_b[0]))

    wi = w_in[0]
    kr_slab = jnp.pad(wi[:, OFF_KR:OFF_GATE], ((0, 0), (QK_NOPE, HEAD_SLAB - QK_NOPE - QK_ROPE)))
    wcat = jnp.concatenate([wi[:, OFF_Q:OFF_KV], wi[:, OFF_KV:OFF_KR], wi[:, OFF_GATE:], kr_slab],
                           axis=1).astype(BF16)
    whyt = wi[:, 0:OFF_Q].T.astype(BF16)
    wuq = _pad_heads(w_uq[0], QK_NOPE + QK_ROPE).astype(BF16)
    wuk = _pad_heads(w_uk[0].reshape(KV_LORA, N_HEADS * QK_NOPE), QK_NOPE).astype(BF16)
    wuvt = _pad_heads(w_uv[0].reshape(KV_LORA, N_HEADS * V_HEAD), V_HEAD).T.astype(BF16)
    tkv = _pick(L, 1024)
    cos_t, sin_a, sin_b = _rope_slab_tables(L)

    hypt, gates, q, k, vt, kn2_tiles = _inproj(x2, mod, row(norm_mix_g[0]), wcat, whyt, row(q_norm_g[0]),
                                    row(kv_norm_g[0]), wuq, wuk, wuvt, cos_t, sin_a, sin_b,
                                    _pick(L, 512), tkv)

    zt = _filter_features(L)
    w1t = jnp.pad(filt_w1[0], ((0, FILTER_HIDDEN - FILTER_EMB), (0, 0))).T
    dl = np.abs(np.linspace(MIN_DECAY, MAX_DECAY, HY_WIDTH))[:, None].astype(np.float32)
    ht, nrm = _filt(zt, w1t, col(filt_b1[0]), col(filt_freq1[0]), filt_w2[0].T, col(filt_b2[0]),
                    col(filt_freq2[0]), filt_w3[0].T.astype(BF16), dl, _pick(L, 1024))
    tabs, n1h = _dft_tables(L)
    cw, cbias = hy_conv_w[0], hy_conv_b[0]
    w = HY_WIDTH
    taps = lambda s: [cw[0, s * w:(s + 1) * w], cw[1, s * w:(s + 1) * w], cw[2, s * w:(s + 1) * w],
                      cbias[s * w:(s + 1) * w]]
    prm = jnp.stack(taps(0) + taps(1) + taps(2) + [hy_bias[0], nrm[:, 0]]
                    + [jnp.zeros((w,), F32)] * (HY_PARAM_ROWS - 14), axis=1)
    prm = jnp.broadcast_to(prm[:, :, None], (w, HY_PARAM_ROWS, DFT_N2))
    yht = _hyena(hypt.reshape(3 * w, n1h, DFT_N2), ht.reshape(2 * w, n1h, DFT_N2), prm, tabs,
                 _pick(w, 16), 1.0 / (2 * L)).reshape(w, L)

    kn2 = jnp.broadcast_to(jnp.max(kn2_tiles, axis=0)[:, None, :], (N_HEADS, SUBLANES, LANES))
    ym = _attn(q, k, vt, kn2, _pick(L, 2048), 8)

    wmla = w_mla_out[0].astype(BF16)
    wrt = jnp.pad(jnp.concatenate([w_router[0], w_group[0]], axis=1),
                  ((0, 0), (0, LANES - N_EXPERTS - N_GROUPS)))
    brt = jnp.pad(jnp.concatenate([b_router[0], b_group[0]]), (0, LANES - N_EXPERTS - N_GROUPS))[None, :]
    x1, h2, comb, gsel, gcnt = _merge(x2, yht, ym, gates, mod, w_hy_out[0].astype(BF16), wmla,
                                      w_mix_out[0].astype(BF16), row(norm_ffn_g[0]), wrt, brt,
                                      _pick(L, 1024))

    assert L % MOE_TILE == 0
    bg, nbu, roff = _moe_plan(gcnt, L // MOE_TILE)
    wgu = jnp.concatenate([w_gate[0], w_up[0]], axis=2).astype(BF16)
    wd = w_down[0].reshape(N_EXPERTS * D_EXPERT, d).astype(BF16)
    moe = _moe(bg, nbu, roff, h2, comb, gsel, wgu, wd)
    out = _final(x1, moe, mod, row(final_norm_g), _pick(L, 512))
    return out.reshape(B, L, d)
```

```python
import functools
import math

import numpy as np
import jax
import jax.numpy as jnp
from jax import lax
from jax.experimental import pallas as pl
from jax.experimental.pallas import tpu as pltpu

F32 = jnp.float32
BF16 = jnp.bfloat16
HIGHEST = lax.Precision.HIGHEST

D_MODEL = 1024
EPS = 1e-6
HY_WIDTH = 512
FILTER_BANDS = 16
FILTER_EMB = 1 + 2 * FILTER_BANDS
FILTER_HIDDEN = 64
FILTER_DECAY_TARGET = 1e-2
FAST_DECAY_PCT = 0.3
SLOW_DECAY_PCT = 1.5
FILTER_SHIFT = 0.05
MIN_DECAY = math.log(FILTER_DECAY_TARGET) / SLOW_DECAY_PCT
MAX_DECAY = math.log(FILTER_DECAY_TARGET) / FAST_DECAY_PCT
N_HEADS = 8
QK_NOPE = 64
QK_ROPE = 32
V_HEAD = 64
Q_LORA = 384
KV_LORA = 256
ROPE_THETA = 10000.0
MLA_WIDTH = N_HEADS * V_HEAD
OFF_Q = 3 * HY_WIDTH
OFF_KV = OFF_Q + Q_LORA
OFF_KR = OFF_KV + KV_LORA
OFF_GATE = OFF_KR + QK_ROPE
N_GROUPS = 4
EXPERTS_PER_GROUP = 8
N_EXPERTS = N_GROUPS * EXPERTS_PER_GROUP
D_EXPERT = 256

LANES = 128
SUBLANES = 8
HEAD_SLAB = LANES
QK_SLABS = N_HEADS * HEAD_SLAB
ONES_LANE = V_HEAD
DFT_N2 = LANES
HY_PARAM_ROWS = 16
NEG_BIG = -1e30
VMEM_LIMIT = 56 * 1024 * 1024
MAX_STATIC_SHIFT = 60.0
SHIFT_SLACK = 1.01
SHIFT_EPS = 1e-3

C_Q = 0
C_KV = C_Q + Q_LORA
C_GATE = C_KV + KV_LORA
C_KR = C_GATE + 2 * D_MODEL
C_END = C_KR + HEAD_SLAB


def _cparams(*sem):
    return pltpu.CompilerParams(dimension_semantics=sem, vmem_limit_bytes=VMEM_LIMIT)


def _rms(x):
    return x * lax.rsqrt(jnp.mean(x * x, axis=-1, keepdims=True) + EPS)


def _dot_nt(a, b):
    return lax.dot_general(a, b, (((1,), (1,)), ((), ())), preferred_element_type=F32)


def _dot_tn(a, b):
    return lax.dot_general(a, b, (((0,), (0,)), ((), ())), preferred_element_type=F32)


def _ada_kernel(c_ref, w_ref, b_ref, o_ref):
    c = c_ref[...]
    ca = c * jax.nn.sigmoid(c)
    o_ref[...] = jnp.dot(ca, w_ref[...], preferred_element_type=F32, precision=HIGHEST) + b_ref[...]


def _ada(c8, w, b):
    d, n = w.shape
    tn = 1024
    return pl.pallas_call(
        _ada_kernel,
        grid=(n // tn,),
        in_specs=[pl.BlockSpec((8, d), lambda j: (0, 0)),
                  pl.BlockSpec((d, tn), lambda j: (0, j)),
                  pl.BlockSpec((1, tn), lambda j: (0, j))],
        out_specs=pl.BlockSpec((8, tn), lambda j: (0, j)),
        out_shape=jax.ShapeDtypeStruct((8, n), F32),
        compiler_params=_cparams("parallel"),
        name="ada",
    )(c8, w, b)


def _rope_slab(z, cos_t, sin_a, sin_b):
    return z * cos_t + pltpu.roll(z, HEAD_SLAB - QK_ROPE // 2, 1) * sin_a + pltpu.roll(z, QK_ROPE // 2, 1) * sin_b


def _inproj_kernel(x_ref, mod_ref, g_ref, wcat_ref, whyt_ref, gq_ref, gkv_ref, wuq_ref, wuk_ref, wuvt_ref,
                   cos_ref, sina_ref, sinb_ref,
                   hypt_ref, gate_ref, q_ref, k_ref, vt_ref, kn2_ref, *, qscale):
    d = D_MODEL
    x = x_ref[...]
    sh1 = mod_ref[0:1, 0:d]
    sc1 = mod_ref[0:1, d:2 * d]
    hb = (_rms(x) * g_ref[...] * (1.0 + sc1) + sh1).astype(BF16)
    hypt_ref[...] = _dot_nt(whyt_ref[...], hb).astype(BF16)
    proj = jnp.dot(hb, wcat_ref[...], preferred_element_type=F32)
    gate_ref[...] = jax.nn.sigmoid(proj[:, C_GATE:C_KR]).astype(BF16)
    cq = _rms(proj[:, C_Q:C_KV]) * gq_ref[...]
    ckv = (_rms(proj[:, C_KV:C_GATE]) * gkv_ref[...]).astype(BF16)
    q = jnp.dot(cq.astype(BF16), wuq_ref[...], preferred_element_type=F32)
    kn = jnp.dot(ckv, wuk_ref[...], preferred_element_type=F32)
    vt = _dot_nt(wuvt_ref[...], ckv)
    cos_t = cos_ref[...]
    sin_a = sina_ref[...]
    sin_b = sinb_ref[...]
    kr = _rope_slab(proj[:, C_KR:C_END], cos_t, sin_a, sin_b)
    srow = lax.broadcasted_iota(jnp.int32, (HEAD_SLAB, 1), 0)
    ones_row = jnp.where(srow == ONES_LANE, 1.0, 0.0).astype(F32)
    kn2 = []
    for hd in range(N_HEADS):
        sl = slice(hd * HEAD_SLAB, (hd + 1) * HEAD_SLAB)
        q_ref[:, sl] = (_rope_slab(q[:, sl], cos_t, sin_a, sin_b) * qscale).astype(BF16)
        kb = (kn[:, sl] + kr).astype(BF16)
        k_ref[:, sl] = kb
        vt_ref[hd, 0] = (vt[sl] + ones_row).astype(BF16)
        kf = kb.astype(F32)
        big = jnp.max(jnp.sum(kf * kf, axis=1, keepdims=True), axis=0, keepdims=True)
        kn2.append(jnp.broadcast_to(big, (1, HEAD_SLAB)))
    kn2_ref[0] = jnp.concatenate(kn2, axis=0)


def _inproj(x2, mod, g1, wcat, whyt, gq, gkv, wuq, wuk, wuvt, cos_t, sin_a, sin_b, tm, tkv):
    L, d = x2.shape
    qscale = (QK_NOPE + QK_ROPE) ** -0.5 * math.log2(math.e)
    full = lambda a: pl.BlockSpec(a.shape, lambda i: (0,) * a.ndim)
    row = lambda w: pl.BlockSpec((tm, w), lambda i: (i, 0))
    per = tkv // tm
    return pl.pallas_call(
        functools.partial(_inproj_kernel, qscale=qscale),
        grid=(L // tm,),
        in_specs=[row(d), full(mod), full(g1), full(wcat), full(whyt), full(gq), full(gkv), full(wuq),
                  full(wuk), full(wuvt), row(HEAD_SLAB), row(HEAD_SLAB), row(HEAD_SLAB)],
        out_specs=[pl.BlockSpec((3 * HY_WIDTH, tm), lambda i: (0, i)),
                   row(2 * D_MODEL), row(QK_SLABS), row(QK_SLABS),
                   pl.BlockSpec((N_HEADS, 1, HEAD_SLAB, tm), lambda i: (0, i // per, 0, i % per)),
                   pl.BlockSpec((1, N_HEADS, HEAD_SLAB), lambda i: (i, 0, 0))],
        out_shape=[jax.ShapeDtypeStruct((3 * HY_WIDTH, L), BF16),
                   jax.ShapeDtypeStruct((L, 2 * D_MODEL), BF16),
                   jax.ShapeDtypeStruct((L, QK_SLABS), BF16),
                   jax.ShapeDtypeStruct((L, QK_SLABS), BF16),
                   jax.ShapeDtypeStruct((N_HEADS, L // tkv, HEAD_SLAB, tkv), BF16),
                   jax.ShapeDtypeStruct((L // tm, N_HEADS, HEAD_SLAB), F32)],
        compiler_params=_cparams("parallel"),
        name="inproj",
    )(x2, mod, g1, wcat, whyt, gq, gkv, wuq, wuk, wuvt, cos_t, sin_a, sin_b)


def _filt_kernel(z_ref, w1_ref, b1_ref, f1_ref, w2_ref, b2_ref, f2_ref, w3_ref, dl_ref,
                 h_ref, nrm_ref):
    i = pl.program_id(0)
    z = z_ref[...]
    tm = z.shape[1]
    h = jnp.sin(f1_ref[...] * (jnp.dot(w1_ref[...], z, preferred_element_type=F32, precision=HIGHEST)
                               + b1_ref[...]))
    h = jnp.sin(f2_ref[...] * (jnp.dot(w2_ref[...], h, preferred_element_type=F32, precision=HIGHEST)
                               + b2_ref[...]))
    h3 = jnp.dot(w3_ref[...], h.astype(BF16), preferred_element_type=F32)
    t = z[0:1, :]
    window = jnp.exp(-dl_ref[...] * t) + FILTER_SHIFT
    hf = h3[0:HY_WIDTH] * window
    hb = h3[HY_WIDTH:2 * HY_WIDTH] * window
    col = lax.broadcasted_iota(jnp.int32, (1, tm), 1)
    hb = jnp.where((col == 0) & (i == 0), 0.0, hb)
    h_ref[0:HY_WIDTH, :] = hf.astype(BF16)
    h_ref[HY_WIDTH:2 * HY_WIDTH, :] = hb.astype(BF16)
    part = jnp.sum(jnp.abs(hf) + jnp.abs(hb), axis=1, keepdims=True)

    @pl.when(i == 0)
    def _():
        nrm_ref[...] = jnp.zeros_like(nrm_ref)

    nrm_ref[...] += part


def _filt(zt, w1t, b1, f1, w2t, b2, f2, w3t, dl, tm):
    L = zt.shape[1]
    full = lambda a: pl.BlockSpec(a.shape, lambda i: (0,) * a.ndim)
    return pl.pallas_call(
        _filt_kernel,
        grid=(L // tm,),
        in_specs=[pl.BlockSpec((zt.shape[0], tm), lambda i: (0, i)), full(w1t), full(b1), full(f1),
                  full(w2t), full(b2), full(f2), full(w3t), full(dl)],
        out_specs=[pl.BlockSpec((2 * HY_WIDTH, tm), lambda i: (0, i)),
                   pl.BlockSpec((HY_WIDTH, 1), lambda i: (0, 0))],
        out_shape=[jax.ShapeDtypeStruct((2 * HY_WIDTH, L), BF16),
                   jax.ShapeDtypeStruct((HY_WIDTH, 1), F32)],
        compiler_params=_cparams("arbitrary"),
        name="filt",
    )(zt, w1t, b1, f1, w2t, b2, f2, w3t, dl)


def _hyena_kernel(x0_ref, x1_ref, v_ref, hf_ref, hb_ref, prm_ref, f1_ref, tr_ref, ti_ref, f2_ref,
                  f2i_ref, cm_ref, sg_ref, o_ref, *, inv_n):
    cb, n1h, w = x0_ref.shape
    k1p = tr_ref.shape[0]
    lane = lax.broadcasted_iota(jnp.int32, (n1h, w), 1)
    rowi = lax.broadcasted_iota(jnp.int32, (n1h, w), 0)
    tr = tr_ref[...]
    ti = ti_ref[...]

    def conv(x, p, j):
        r = pltpu.roll(x, 1, 1)
        up = jnp.where(lane == 0, jnp.where(rowi == 0, 0.0, pltpu.roll(r, 1, 0)), r)
        l = pltpu.roll(x, w - 1, 1)
        dn = jnp.where(lane == w - 1, jnp.where(rowi == n1h - 1, 0.0, pltpu.roll(l, n1h - 1, 0)), l)
        return up * p[j:j + 1] + x * p[j + 1:j + 2] + dn * p[j + 2:j + 3] + p[j + 3:j + 4]

    def twiddle(a, s):
        ar = a[0:k1p, s * w:(s + 1) * w]
        ai = a[k1p:2 * k1p, s * w:(s + 1) * w]
        return jnp.concatenate([ar * tr - ai * ti, ar * ti + ai * tr], axis=1).astype(BF16)

    prm = [prm_ref[c] for c in range(cb)]
    x0s, us, sigs = [], [], []
    for c in range(cb):
        p = prm[c]
        x0s.append(conv(x0_ref[c].astype(F32), p, 0))
        u = conv(v_ref[c].astype(F32), p, 8) * conv(x1_ref[c].astype(F32), p, 4)
        us.append(u)
        sigs += [u.astype(BF16), hf_ref[c], hb_ref[c]]
    a = jnp.dot(f1_ref[...], jnp.concatenate(sigs, axis=1), preferred_element_type=F32)
    b = jnp.concatenate([twiddle(a, s) for s in range(3 * cb)], axis=0)
    s = jnp.dot(b, f2_ref[...], preferred_element_type=F32)
    ys = []
    for c in range(cb):
        su = s[(3 * c) * k1p:(3 * c + 1) * k1p]
        sf = s[(3 * c + 1) * k1p:(3 * c + 2) * k1p]
        sb = s[(3 * c + 2) * k1p:(3 * c + 3) * k1p]
        inv = 1.0 / prm[c][13:14]
        gr = (sf[:, 0:w] + sb[:, 0:w]) * inv
        gi = (sf[:, w:2 * w] - sb[:, w:2 * w]) * inv
        ur, ui = su[:, 0:w], su[:, w:2 * w]
        ys.append(jnp.concatenate([ur * gr - ui * gi, ur * gi + ui * gr], axis=1).astype(BF16))
    zt = jnp.dot(jnp.concatenate(ys, axis=0), f2i_ref[...], preferred_element_type=F32)
    zss, nyq = [], []
    for c in range(cb):
        ztr = zt[c * k1p:(c + 1) * k1p, 0:w]
        zti = zt[c * k1p:(c + 1) * k1p, w:2 * w]
        zr = ztr * tr + zti * ti
        zi = zti * tr - ztr * ti
        zss.append(jnp.concatenate([zr[0:n1h], zi[0:n1h]], axis=0).astype(BF16))
        nyq.append(zr[n1h:n1h + 1])
    yy = jnp.dot(cm_ref[...], jnp.concatenate(zss, axis=1), preferred_element_type=F32)
    for c in range(cb):
        conv_out = (yy[:, c * w:(c + 1) * w] + sg_ref[...] * nyq[c]) * inv_n
        o_ref[c] = (x0s[c] * (conv_out + prm[c][12:13] * us[c])).astype(BF16)


def _hyena(hyp3, h3, prm, tabs, cb, inv_n):
    c3, n1h, w = hyp3.shape
    c = c3 // 3
    nb = c // cb
    f1m, tr, ti, f2, f2i, cmat, sgn = tabs
    full = lambda a: pl.BlockSpec(a.shape, lambda i: (0,) * a.ndim)
    blk = lambda off: pl.BlockSpec((cb, n1h, w), lambda i: (off * nb + i, 0, 0))
    return pl.pallas_call(
        functools.partial(_hyena_kernel, inv_n=inv_n),
        grid=(nb,),
        in_specs=[blk(0), blk(1), blk(2), blk(0), blk(1),
                  pl.BlockSpec((cb, HY_PARAM_ROWS, w), lambda i: (i, 0, 0)),
                  full(f1m), full(tr), full(ti), full(f2), full(f2i), full(cmat), full(sgn)],
        out_specs=pl.BlockSpec((cb, n1h, w), lambda i: (i, 0, 0)),
        out_shape=jax.ShapeDtypeStruct((c, n1h, w), BF16),
        compiler_params=_cparams("parallel"),
        name="hyena",
    )(hyp3, hyp3, hyp3, h3, h3, prm, f1m, tr, ti, f2, f2i, cmat, sgn)


def _attn_kernel(q_ref, k_ref, vt_ref, kn2_ref, o_ref, *, unroll):
    tq = q_ref.shape[0]
    nk, _, tk = vt_ref.shape[1:]
    q = q_ref[...]

    def finish(acc):
        o_ref[...] = (acc / acc[ONES_LANE:ONES_LANE + 1, :]).astype(BF16)

    qf = q.astype(F32)
    qn2 = _dot_nt(jnp.ones((SUBLANES, HEAD_SLAB), BF16), (qf * qf).astype(BF16))[0:1]
    bound = jnp.sqrt(qn2 * kn2_ref[0][0:1, 0:1]) * SHIFT_SLACK + SHIFT_EPS
    static_shift_ok = jnp.max(bound) <= MAX_STATIC_SHIFT

    @pl.when(static_shift_ok)
    def _():
        def body(j, acc):
            off = pl.multiple_of(j * tk, tk)
            st = _dot_nt(k_ref[pl.ds(off, tk), :], q)
            p = jnp.exp2(st - bound).astype(BF16)
            return acc + jnp.dot(vt_ref[0, j], p, preferred_element_type=F32)

        finish(lax.fori_loop(0, nk, body, jnp.zeros((HEAD_SLAB, tq), F32), unroll=unroll))

    @pl.when(jnp.logical_not(static_shift_ok))
    def _():
        def body(j, carry):
            m, acc = carry
            off = pl.multiple_of(j * tk, tk)
            st = _dot_nt(k_ref[pl.ds(off, tk), :], q)
            m_new = jnp.maximum(m, jnp.max(st, axis=0, keepdims=True))
            alpha = jnp.exp2(m - m_new)
            p = jnp.exp2(st - m_new)
            acc = alpha * acc + jnp.dot(vt_ref[0, j], p.astype(BF16), preferred_element_type=F32)
            return m_new, acc

        m0 = jnp.full((1, tq), NEG_BIG, F32)
        acc0 = jnp.zeros((HEAD_SLAB, tq), F32)
        finish(lax.fori_loop(0, nk, body, (m0, acc0), unroll=unroll)[1])


def _attn(q, k, vt, kn2, tq, unroll):
    L = q.shape[0]
    _, nk, _, tk = vt.shape
    return pl.pallas_call(
        functools.partial(_attn_kernel, unroll=unroll),
        grid=(N_HEADS, L // tq),
        in_specs=[pl.BlockSpec((tq, HEAD_SLAB), lambda h, i: (i, h)),
                  pl.BlockSpec((L, HEAD_SLAB), lambda h, i: (0, h)),
                  pl.BlockSpec((1, nk, HEAD_SLAB, tk), lambda h, i: (h, 0, 0, 0)),
                  pl.BlockSpec((1, SUBLANES, LANES), lambda h, i: (h, 0, 0))],
        out_specs=pl.BlockSpec((HEAD_SLAB, tq), lambda h, i: (h, i)),
        out_shape=jax.ShapeDtypeStruct((QK_SLABS, L), BF16),
        compiler_params=_cparams("parallel", "parallel"),
        name="attn",
    )(q, k, vt, kn2)


def _merge_kernel(x_ref, yht_ref, ymt_ref, gate_ref, mod_ref, why_ref, wmla_ref, wmix_ref, g2_ref,
                  wrt_ref, brt_ref, x1_ref, h2_ref, comb_ref, gsel_ref, gcnt_ref):
    d = D_MODEL
    a = _dot_tn(yht_ref[...], why_ref[...])
    ymt = jnp.concatenate([ymt_ref[hd * HEAD_SLAB:hd * HEAD_SLAB + V_HEAD, :] for hd in range(N_HEADS)],
                          axis=0)
    b = _dot_tn(ymt, wmla_ref[...])
    mix = gate_ref[:, 0:d].astype(F32) * a + gate_ref[:, d:2 * d].astype(F32) * b
    o = jnp.dot(mix.astype(BF16), wmix_ref[...], preferred_element_type=F32)
    gt1 = mod_ref[0:1, 2 * d:3 * d]
    sh2 = mod_ref[0:1, 3 * d:4 * d]
    sc2 = mod_ref[0:1, 4 * d:5 * d]
    x1 = x_ref[...] + gt1 * o
    x1_ref[...] = x1
    h2 = _rms(x1) * g2_ref[...] * (1.0 + sc2) + sh2
    h2_hi = h2.astype(BF16)
    h2_ref[...] = h2_hi
    h2_lo = (h2 - h2_hi.astype(F32)).astype(BF16)
    wrt = wrt_ref[...]
    w_hi = wrt.astype(BF16)
    w_lo = (wrt - w_hi.astype(F32)).astype(BF16)
    e = (jnp.dot(h2_hi, w_hi, preferred_element_type=F32)
         + (jnp.dot(h2_lo, w_hi, preferred_element_type=F32)
            + jnp.dot(h2_hi, w_lo, preferred_element_type=F32))) + brt_ref[...]
    lane = lax.broadcasted_iota(jnp.int32, e.shape, 1).astype(F32)
    gvalid = (lane >= N_EXPERTS) & (lane < N_EXPERTS + N_GROUPS)
    glm = jnp.where(gvalid, e, NEG_BIG)
    gmax = jnp.max(glm, axis=1, keepdims=True)
    gidx = jnp.min(jnp.where(glm == gmax, lane, float(LANES)), axis=1, keepdims=True) - float(N_EXPERTS)
    psum = jnp.sum(jnp.where(gvalid, jnp.exp(glm - gmax), 0.0), axis=1, keepdims=True)
    p_sel = 1.0 / psum
    lo = gidx * EXPERTS_PER_GROUP
    em = jnp.where((lane >= lo) & (lane < lo + EXPERTS_PER_GROUP), e, NEG_BIG)
    v1 = jnp.max(em, axis=1, keepdims=True)
    i1 = jnp.min(jnp.where(em == v1, lane, float(LANES)), axis=1, keepdims=True)
    em2 = jnp.where(lane == i1, NEG_BIG, em)
    v2 = jnp.max(em2, axis=1, keepdims=True)
    i2 = jnp.min(jnp.where(em2 == v2, lane, float(LANES)), axis=1, keepdims=True)
    t = jnp.exp(v2 - v1)
    w1 = p_sel / (1.0 + t)
    w2 = p_sel * t / (1.0 + t)
    comb_ref[...] = jnp.where(lane == i1, w1, 0.0) + jnp.where(lane == i2, w2, 0.0)
    gsel = jnp.where(lane == gidx, 1.0, 0.0)
    gsel_ref[...] = gsel.astype(BF16)
    srow = lax.broadcasted_iota(jnp.int32, (SUBLANES, LANES), 0)
    gcnt_ref[0] = jnp.where(srow == 0, jnp.sum(gsel, axis=0, keepdims=True), 0.0)


def _merge(x2, yht, ym, gates, mod, why, wmla, wmix, g2, wrt, brt, tm):
    L, d = x2.shape
    full = lambda a: pl.BlockSpec(a.shape, lambda i: (0,) * a.ndim)
    row = lambda w: pl.BlockSpec((tm, w), lambda i: (i, 0))
    return pl.pallas_call(
        _merge_kernel,
        grid=(L // tm,),
        in_specs=[row(d), pl.BlockSpec((HY_WIDTH, tm), lambda i: (0, i)),
                  pl.BlockSpec((QK_SLABS, tm), lambda i: (0, i)), row(2 * d),
                  full(mod), full(why), full(wmla), full(wmix), full(g2), full(wrt), full(brt)],
        out_specs=[row(d), row(d), row(LANES), row(LANES),
                   pl.BlockSpec((1, SUBLANES, LANES), lambda i: (i, 0, 0))],
        out_shape=[jax.ShapeDtypeStruct((L, d), F32),
                   jax.ShapeDtypeStruct((L, d), BF16),
                   jax.ShapeDtypeStruct((L, LANES), F32),
                   jax.ShapeDtypeStruct((L, LANES), BF16),
                   jax.ShapeDtypeStruct((L // tm, SUBLANES, LANES), F32)],
        compiler_params=_cparams("parallel"),
        name="merge",
    )(x2, yht, ym, gates, mod, why, wmla, wmix, g2, wrt, brt)


MOE_TILE = 2048
MOE_ROWS = 256
MOE_BLOCKS = MOE_TILE // MOE_ROWS + N_GROUPS
POS_RADIX = 64


def _moe_kernel(bg_ref, nbu_ref, roff_ref, h_ref, comb_ref, gsel_ref, wgu_ref, wd_ref, o_ref,
                pos_ref, cs_ref, acc_ref):
    i = pl.program_id(0)
    b = pl.program_id(1)
    t = h_ref.shape[0]
    lane = lax.broadcasted_iota(jnp.int32, (1, LANES), 1)

    @pl.when(b == 0)
    def _():
        oh = gsel_ref[...]
        sub = MOE_ROWS
        tri = jnp.where(lax.broadcasted_iota(jnp.int32, (sub, sub), 1)
                        < lax.broadcasted_iota(jnp.int32, (sub, sub), 0), 1.0, 0.0).astype(BF16)
        carry = jnp.zeros((1, LANES), F32)
        cums = []
        for s in range(t // sub):
            blk = oh[s * sub:(s + 1) * sub]
            cums.append(jnp.dot(tri, blk, preferred_element_type=F32) + carry)
            carry = carry + jnp.sum(blk.astype(F32), axis=0, keepdims=True)
        cum = jnp.concatenate(cums, axis=0)
        offs = jnp.zeros((1, LANES), F32)
        for g in range(N_GROUPS):
            offs = jnp.where(lane == g, roff_ref[i * N_GROUPS + g].astype(F32), offs)
        pos = jnp.sum(jnp.where(oh > 0, cum + offs, 0.0), axis=1, keepdims=True)
        hi = jnp.floor(pos * (1.0 / POS_RADIX))
        lo = pos - hi * POS_RADIX
        digits = jnp.where(lane == 0, hi, jnp.where(lane == 1, lo, 0.0)).astype(BF16)
        srow = lax.broadcasted_iota(jnp.int32, (SUBLANES, LANES), 0)
        scol = lax.broadcasted_iota(jnp.int32, (SUBLANES, LANES), 1)
        pick = jnp.where(srow == scol, 1.0, 0.0).astype(BF16)
        rows = _dot_nt(pick, digits)
        pos_ref[...] = jnp.broadcast_to(rows[0:1] * POS_RADIX + rows[1:2], pos_ref.shape)
        comb = comb_ref[...]
        comb_hi = comb.astype(BF16)
        cs_ref[:, 0:LANES] = comb_hi
        cs_ref[:, LANES:2 * LANES] = (comb - comb_hi.astype(F32)).astype(BF16)
        acc_ref[...] = jnp.zeros_like(acc_ref)

    @pl.when(b < nbu_ref[i])
    def _():
        g = bg_ref[i * MOE_BLOCKS + b]
        rid = (lax.broadcasted_iota(jnp.int32, (MOE_ROWS, 1), 0) + b * MOE_ROWS).astype(F32)
        sel = jnp.where(rid == pos_ref[0:1, :], 1.0, 0.0).astype(BF16)
        xs = jnp.dot(sel, h_ref[...], preferred_element_type=F32).astype(BF16)
        cs = jnp.dot(sel, cs_ref[...], preferred_element_type=F32)
        comb_s = cs[:, 0:LANES] + cs[:, LANES:2 * LANES]
        parts = []
        for j in range(EXPERTS_PER_GROUP):
            gu = jnp.dot(xs, wgu_ref[j], preferred_element_type=F32)
            gt = gu[:, 0:D_EXPERT]
            a = gt * jax.nn.sigmoid(gt) * gu[:, D_EXPERT:2 * D_EXPERT]
            col = jnp.sum(jnp.where(lane == g * EXPERTS_PER_GROUP + j, comb_s, 0.0), axis=1, keepdims=True)
            parts.append((a * col).astype(BF16))
        y = jnp.dot(jnp.concatenate(parts, axis=1), wd_ref[...], preferred_element_type=F32)
        acc_ref[...] += _dot_tn(sel, y.astype(BF16))

    @pl.when(b == pl.num_programs(1) - 1)
    def _():
        o_ref[...] = acc_ref[...].astype(BF16)


def _moe(bg, nbu, roff, h2, comb, gsel, wgu, wd):
    L, d = h2.shape
    t = MOE_TILE
    once = dict(pipeline_mode=pl.Buffered(1))
    wsel = lambda i, b, bg_r, nbu_r, roff_r: (bg_r[i * MOE_BLOCKS + b], 0, 0)
    grid_spec = pltpu.PrefetchScalarGridSpec(
        num_scalar_prefetch=3,
        grid=(L // t, MOE_BLOCKS),
        in_specs=[pl.BlockSpec((t, d), lambda i, b, *_: (i, 0), **once),
                  pl.BlockSpec((t, LANES), lambda i, b, *_: (i, 0), **once),
                  pl.BlockSpec((t, LANES), lambda i, b, *_: (i, 0), **once),
                  pl.BlockSpec((EXPERTS_PER_GROUP, d, 2 * D_EXPERT), wsel),
                  pl.BlockSpec((EXPERTS_PER_GROUP * D_EXPERT, d), lambda i, b, bg_r, *_: (bg_r[i * MOE_BLOCKS + b], 0))],
        out_specs=pl.BlockSpec((t, d), lambda i, b, *_: (i, 0)),
        scratch_shapes=[pltpu.VMEM((SUBLANES, t), F32),
                        pltpu.VMEM((t, 2 * LANES), BF16),
                        pltpu.VMEM((t, d), F32)])
    return pl.pallas_call(
        _moe_kernel,
        grid_spec=grid_spec,
        out_shape=jax.ShapeDtypeStruct((L, d), BF16),
        compiler_params=_cparams("parallel", "arbitrary"),
        name="moe",
    )(bg, nbu, roff, h2, comb, gsel, wgu, wd)


def _moe_plan(gcnt, n_tiles):
    cnt = gcnt[:, 0, 0:N_GROUPS].reshape(n_tiles, -1, N_GROUPS).sum(axis=1).astype(jnp.int32)
    nblk = (cnt + MOE_ROWS - 1) // MOE_ROWS
    bend = jnp.cumsum(nblk, axis=1)
    roff = (bend - nblk) * MOE_ROWS
    nbu = bend[:, -1]
    bidx = jnp.arange(MOE_BLOCKS, dtype=jnp.int32)[None, :]
    grp = jnp.sum((bidx[:, :, None] >= bend[:, None, :]).astype(jnp.int32), axis=2)
    last = jnp.take_along_axis(grp, jnp.maximum(nbu - 1, 0)[:, None], axis=1)
    bg = jnp.where(bidx < nbu[:, None], grp, last)
    bg = jnp.minimum(bg, N_GROUPS - 1)
    return bg.reshape(-1), nbu, roff.reshape(-1)


def _final_kernel(x1_ref, m_ref, mod_ref, fg_ref, o_ref):
    d = D_MODEL
    x2 = x1_ref[...] + mod_ref[0:1, 5 * d:6 * d] * m_ref[...].astype(F32)
    o_ref[...] = _rms(x2) * fg_ref[...]


def _final(x1, m, mod, fg, tm):
    L, d = x1.shape
    return pl.pallas_call(
        _final_kernel,
        grid=(L // tm,),
        in_specs=[pl.BlockSpec((tm, d), lambda i: (i, 0)), pl.BlockSpec((tm, d), lambda i: (i, 0)),
                  pl.BlockSpec(mod.shape, lambda i: (0, 0)), pl.BlockSpec(fg.shape, lambda i: (0, 0))],
        out_specs=pl.BlockSpec((tm, d), lambda i: (i, 0)),
        out_shape=jax.ShapeDtypeStruct((L, d), F32),
        compiler_params=_cparams("parallel"),
        name="final",
    )(x1, m, mod, fg)


def _pad_heads(w, width):
    lead = w.shape[:-1]
    w = w.reshape(lead + (N_HEADS, width))
    w = jnp.pad(w, [(0, 0)] * len(lead) + [(0, 0), (0, HEAD_SLAB - width)])
    return w.reshape(lead + (N_HEADS * HEAD_SLAB,))


@functools.lru_cache(maxsize=None)
def _rope_slab_tables(L):
    half = QK_ROPE // 2
    pos = np.arange(L, dtype=np.float64)
    inv = ROPE_THETA ** (-np.arange(0, QK_ROPE, 2, dtype=np.float64) / QK_ROPE)
    ang = pos[:, None] * inv[None, :]
    cos, sin = np.cos(ang), np.sin(ang)
    z = lambda w: np.zeros((L, w))
    cos_t = np.concatenate([np.ones((L, QK_NOPE)), cos, cos, z(HEAD_SLAB - QK_NOPE - QK_ROPE)], axis=1)
    sin_a = np.concatenate([z(QK_NOPE), -sin, z(HEAD_SLAB - QK_NOPE - half)], axis=1)
    sin_b = np.concatenate([z(QK_NOPE + half), sin, z(HEAD_SLAB - QK_NOPE - QK_ROPE)], axis=1)
    return cos_t.astype(np.float32), sin_a.astype(np.float32), sin_b.astype(np.float32)


@functools.lru_cache(maxsize=None)
def _filter_features(L):
    t = np.linspace(0.0, 1.0, L)[None, :]
    t_r = np.arange(L, dtype=np.float64)[None, :]
    bands = np.linspace(1e-4, FILTER_BANDS - 1, FILTER_BANDS)[:, None]
    ang = 2.0 * math.pi * bands * t_r / L
    z = np.concatenate([t, np.cos(ang), -np.sin(ang)], axis=0)
    return np.pad(z, ((0, FILTER_HIDDEN - FILTER_EMB), (0, 0))).astype(np.float32)


@functools.lru_cache(maxsize=None)
def _dft_tables(L):
    n = 2 * L
    n2 = DFT_N2
    n1 = n // n2
    n1h = n1 // 2
    k1n = n1h + 1
    k1p = -(-k1n // SUBLANES) * SUBLANES
    two_pi = 2.0 * math.pi
    k1 = np.arange(k1p, dtype=np.int64)[:, None]
    live = (k1 < k1n).astype(np.float64)
    m1 = np.arange(n1h, dtype=np.int64)[None, :]
    ang1 = two_pi * ((k1 * m1) % n1) / n1
    f1m = np.concatenate([np.cos(ang1) * live, -np.sin(ang1) * live], axis=0).astype(BF16)
    m2 = np.arange(n2, dtype=np.int64)[None, :]
    ang2 = two_pi * ((k1 * m2) % n) / n
    tr = (np.cos(ang2) * live).astype(np.float32)
    ti = (-np.sin(ang2) * live).astype(np.float32)
    ang3 = two_pi * ((m2.T * m2) % n2) / n2
    f2r, f2i = np.cos(ang3), -np.sin(ang3)
    f2 = np.block([[f2r, f2i], [-f2i, f2r]]).astype(BF16)
    f2inv = np.block([[f2r, -f2i], [f2i, f2r]]).astype(BF16)
    o1 = np.arange(n1h, dtype=np.int64)[:, None]
    q1 = np.arange(n1h, dtype=np.int64)[None, :]
    ang4 = two_pi * ((o1 * q1) % n1) / n1
    wgt = np.where(q1 == 0, 1.0, 2.0)
    cmat = np.concatenate([wgt * np.cos(ang4), -wgt * np.sin(ang4)], axis=1).astype(BF16)
    sgn = np.broadcast_to(np.where(o1 % 2 == 0, 1.0, -1.0), (n1h, n2)).astype(np.float32)
    return (f1m, tr, ti, f2, f2inv, cmat, sgn), n1h


def _pick(n, pref):
    t = pref
    while n % t:
        t //= 2
    return t


def kernel(x, c, ada_w, ada_b, norm_mix_g, w_in, hy_conv_w, hy_conv_b, filt_w1, filt_b1, filt_freq1, filt_w2, filt_b2, filt_freq2, filt_w3, hy_bias, q_norm_g, kv_norm_g, w_uq, w_uk, w_uv, w_hy_out, w_mla_out, w_mix_out, norm_ffn_g, w_group, b_group, w_router, b_router, w_gate, w_up, w_down, final_norm_g):
    B, L, d = x.shape
    assert B == 1 and d == D_MODEL and ada_w.shape[0] == 1 and L % (2 * DFT_N2) == 0
    x2 = x.reshape(L, d)
    row = lambda v: v.reshape(1, -1)
    col = lambda v: v.reshape(-1, 1)

    mod = _ada(jnp.pad(c, ((0, 8 - B), (0, 0))), ada_w[0], row(ada_b[0]))

    wi = w_in[0]
    kr_slab = jnp.pad(wi[:, OFF_KR:OFF_GATE], ((0, 0), (QK_NOPE, HEAD_SLAB - QK_NOPE - QK_ROPE)))
    wcat = jnp.concatenate([wi[:, OFF_Q:OFF_KV], wi[:, OFF_KV:OFF_KR], wi[:, OFF_GATE:], kr_slab],
                           axis=1).astype(BF16)
    whyt = wi[:, 0:OFF_Q].T.astype(BF16)
    wuq = _pad_heads(w_uq[0], QK_NOPE + QK_ROPE).astype(BF16)
    wuk = _pad_heads(w_uk[0].reshape(KV_LORA, N_HEADS * QK_NOPE), QK_NOPE).astype(BF16)
    wuvt = _pad_heads(w_uv[0].reshape(KV_LORA, N_HEADS * V_HEAD), V_HEAD).T.astype(BF16)
    tkv = _pick(L, 1024)
    cos_t, sin_a, sin_b = _rope_slab_tables(L)

    hypt, gates, q, k, vt, kn2_tiles = _inproj(x2, mod, row(norm_mix_g[0]), wcat, whyt, row(q_norm_g[0]),
                                    row(kv_norm_g[0]), wuq, wuk, wuvt, cos_t, sin_a, sin_b,
                                    _pick(L, 512), tkv)

    zt = _filter_features(L)
    w1t = jnp.pad(filt_w1[0], ((0, FILTER_HIDDEN - FILTER_EMB), (0, 0))).T
    dl = np.abs(np.linspace(MIN_DECAY, MAX_DECAY, HY_WIDTH))[:, None].astype(np.float32)
    ht, nrm = _filt(zt, w1t, col(filt_b1[0]), col(filt_freq1[0]), filt_w2[0].T, col(filt_b2[0]),
                    col(filt_freq2[0]), filt_w3[0].T.astype(BF16), dl, _pick(L, 1024))
    tabs, n1h = _dft_tables(L)
    cw, cbias = hy_conv_w[0], hy_conv_b[0]
    w = HY_WIDTH
    taps = lambda s: [cw[0, s * w:(s + 1) * w], cw[1, s * w:(s + 1) * w], cw[2, s * w:(s + 1) * w],
                      cbias[s * w:(s + 1) * w]]
    prm = jnp.stack(taps(0) + taps(1) + taps(2) + [hy_bias[0], nrm[:, 0]]
                    + [jnp.zeros((w,), F32)] * (HY_PARAM_ROWS - 14), axis=1)
    prm = jnp.broadcast_to(prm[:, :, None], (w, HY_PARAM_ROWS, DFT_N2))
    yht = _hyena(hypt.reshape(3 * w, n1h, DFT_N2), ht.reshape(2 * w, n1h, DFT_N2), prm, tabs,
                 _pick(w, 16), 1.0 / (2 * L)).reshape(w, L)

    kn2 = jnp.broadcast_to(jnp.max(kn2_tiles, axis=0)[:, None, :], (N_HEADS, SUBLANES, LANES))
    ym = _attn(q, k, vt, kn2, _pick(L, 1024), 16)

    wmla = w_mla_out[0].astype(BF16)
    wrt = jnp.pad(jnp.concatenate([w_router[0], w_group[0]], axis=1),
                  ((0, 0), (0, LANES - N_EXPERTS - N_GROUPS)))
    brt = jnp.pad(jnp.concatenate([b_router[0], b_group[0]]), (0, LANES - N_EXPERTS - N_GROUPS))[None, :]
    x1, h2, comb, gsel, gcnt = _merge(x2, yht, ym, gates, mod, w_hy_out[0].astype(BF16), wmla,
                                      w_mix_out[0].astype(BF16), row(norm_ffn_g[0]), wrt, brt,
                                      _pick(L, 1024))

    assert L % MOE_TILE == 0
    bg, nbu, roff = _moe_plan(gcnt, L // MOE_TILE)
    wgu = jnp.concatenate([w_gate[0], w_up[0]], axis=2).astype(BF16)
    wd = w_down[0].reshape(N_EXPERTS * D_EXPERT, d).astype(BF16)
    moe = _moe(bg, nbu, roff, h2, comb, gsel, wgu, wd)
    out = _final(x1, moe, mod, row(final_norm_g), _pick(L, 512))
    return out.reshape(B, L, d)
```
